```python
import jax, jax.numpy as jnp
from jax import lax
import numpy as np

D_MODEL = 1024
BATCH = 8
SEQ = 4096
DEPTH = 4

GRID_W = 64
CTX_LEN = 256
EPS = 1e-6
ROPE_BASE = 10000.0
NEG_INF = -1e30
BLOCK = 128

HEAD_DIM = 64
CONV_DIM = 512
CONV_WIDTH = 3
N_Q_HEADS = 8
N_KV_HEADS = 2
Q_PER_KV = N_Q_HEADS // N_KV_HEADS
WINDOW = 128
ATTN_DIM = N_Q_HEADS * HEAD_DIM
KV_DIM = N_KV_HEADS * HEAD_DIM
EVEN_IN = 3 * CONV_DIM + ATTN_DIM + 2 * KV_DIM
EVEN_OUT = CONV_DIM + ATTN_DIM
KV_OFF = 3 * CONV_DIM + ATTN_DIM

MLA_HEADS = 8
Q_LORA = 256
KV_LORA = 128
QK_NOPE = 64
QK_ROPE = 32
V_HEAD = 64
FOURIER_DIM = 512
FOURIER_GROUPS = 4
FOURIER_GROUP_DIM = FOURIER_DIM // FOURIER_GROUPS
ODD_IN = Q_LORA + KV_LORA + QK_ROPE + FOURIER_DIM
ODD_OUT = MLA_HEADS * V_HEAD + FOURIER_DIM

N_EXPERTS = 16
N_GROUPS = 4
EXPERTS_PER_GROUP = N_EXPERTS // N_GROUPS
TOP_K = 2
D_FF = 384

kernel_name = "hybrid_dit_conv_swa_mla_fnet_moe"


def rms_norm(x, g):
    xf = x.astype(jnp.float32)
    var = jnp.mean(xf * xf, axis=-1, keepdims=True)
    return (xf * lax.rsqrt(var + EPS)).astype(x.dtype) * g


def modulation(cvec, w, b):
    return jnp.split(jax.nn.silu(cvec) @ w + b, 6, axis=-1)


def modulate(h, shift, scale):
    return h * (1 + scale) + shift


def axial_rope_tables(n_tokens, n_rows, rot_dim):
    rows = jnp.repeat(jnp.arange(n_rows, dtype=jnp.float32), GRID_W)[:n_tokens]
    cols = jnp.tile(jnp.arange(GRID_W, dtype=jnp.float32), n_rows)[:n_tokens]
    axis_dim = rot_dim // 2
    inv_freq = ROPE_BASE ** (-jnp.arange(0, axis_dim, 2, dtype=jnp.float32) / axis_dim)
    ang_r = rows[:, None] * inv_freq
    ang_c = cols[:, None] * inv_freq
    ang = jnp.concatenate([ang_r, ang_r, ang_c, ang_c], axis=-1)
    return jnp.cos(ang), jnp.sin(ang)


def apply_rope(x, cos, sin):
    shape = (1, x.shape[1]) + (1,) * (x.ndim - 3) + (x.shape[-1],)
    cos = cos.reshape(shape)
    sin = sin.reshape(shape)
    r1, r2, c1, c2 = jnp.split(x, 4, axis=-1)
    rot = jnp.concatenate([-r2, r1, -c2, c1], axis=-1)
    return (x.astype(jnp.float32) * cos + rot.astype(jnp.float32) * sin).astype(x.dtype)


def short_conv(u, w):
    up = jnp.pad(u, ((0, 0), (1, 1), (0, 0)))
    return up[:, :-2] * w[0] + up[:, 1:-1] * w[1] + up[:, 2:] * w[2]


def gated_short_conv(a_x, a_b, a_c, w):
    return a_b * short_conv(a_c * a_x, w)


def windowed_gqa_latent(q, k, v, kc, vc, sink):
    b, n = q.shape[:2]
    nb = n // BLOCK
    scale = HEAD_DIM ** -0.5
    qb = q.reshape(b, nb, BLOCK, N_KV_HEADS, Q_PER_KV, HEAD_DIM)

    def band(t):
        tp = jnp.pad(t, ((0, 0), (BLOCK, BLOCK), (0, 0), (0, 0))).reshape(b, nb + 2, BLOCK, N_KV_HEADS, HEAD_DIM)
        return jnp.concatenate([tp[:, :-2], tp[:, 1:-1], tp[:, 2:]], axis=2)

    kw, vw = band(k), band(v)
    s_win = jnp.einsum('bnqhgd,bnkhd->bnhgqk', qb, kw).astype(jnp.float32) * scale
    s_ctx = jnp.einsum('bnqhgd,blhd->bnhgql', qb, kc).astype(jnp.float32) * scale
    qi = jnp.arange(BLOCK)
    kj = jnp.arange(3 * BLOCK) - BLOCK
    key_pos = jnp.arange(nb)[:, None] * BLOCK + kj[None, :]
    in_band = jnp.abs(kj[None, :] - qi[:, None]) <= WINDOW
    in_seq = (key_pos >= 0) & (key_pos < n)
    valid = in_band[None] & in_seq[:, None, :]
    s_win = jnp.where(valid[None, :, None, None], s_win, NEG_INF)
    sink_l = jnp.broadcast_to(sink.astype(jnp.float32).reshape(1, 1, N_KV_HEADS, Q_PER_KV, 1, 1), s_win.shape[:-1] + (1,))
    p = jax.nn.softmax(jnp.concatenate([s_win, s_ctx, sink_l], axis=-1), axis=-1)
    w3 = 3 * BLOCK
    p_win = p[..., :w3].astype(v.dtype)
    p_ctx = p[..., w3:w3 + kc.shape[1]].astype(v.dtype)
    out = jnp.einsum('bnhgqk,bnkhd->bnqhgd', p_win, vw) + jnp.einsum('bnhgql,blhd->bnqhgd', p_ctx, vc)
    return out.reshape(b, n, ATTN_DIM)


def ctx_gqa(qc, kc, vc, sink):
    b, l = qc.shape[:2]
    qg = qc.reshape(b, l, N_KV_HEADS, Q_PER_KV, HEAD_DIM)
    s = jnp.einsum('bqhgd,bkhd->bhgqk', qg, kc).astype(jnp.float32) * (HEAD_DIM ** -0.5)
    sink_l = jnp.broadcast_to(sink.astype(jnp.float32).reshape(1, N_KV_HEADS, Q_PER_KV, 1, 1), s.shape[:-1] + (1,))
    p = jax.nn.softmax(jnp.concatenate([s, sink_l], axis=-1), axis=-1)
    out = jnp.einsum('bhgqk,bkhd->bqhgd', p[..., :l].astype(vc.dtype), vc)
    return out.reshape(b, l, ATTN_DIM)


def even_mixer(h, hc, w_in, conv_w, sink, w_out, cos, sin, need_ctx):
    b, n, _ = h.shape
    l = hc.shape[1]
    splits = [CONV_DIM, 2 * CONV_DIM, 3 * CONV_DIM, KV_OFF, KV_OFF + KV_DIM]
    ax, ab, ac, q, k, v = jnp.split(h @ w_in, splits, axis=-1)
    q = apply_rope(q.reshape(b, n, N_Q_HEADS, HEAD_DIM), cos, sin)
    k = apply_rope(k.reshape(b, n, N_KV_HEADS, HEAD_DIM), cos, sin)
    v = v.reshape(b, n, N_KV_HEADS, HEAD_DIM)
    if need_ctx:
        axc, abc, acc, qc, kc, vc = jnp.split(hc @ w_in, splits, axis=-1)
    else:
        kc, vc = jnp.split(hc @ w_in[:, KV_OFF:], [KV_DIM], axis=-1)
    kc = kc.reshape(b, l, N_KV_HEADS, HEAD_DIM)
    vc = vc.reshape(b, l, N_KV_HEADS, HEAD_DIM)
    y_conv = gated_short_conv(ax, ab, ac, conv_w)
    y_attn = windowed_gqa_latent(q, k, v, kc, vc, sink)
    y = jnp.concatenate([y_conv, y_attn], axis=-1) @ w_out
    if not need_ctx:
        return y, None
    yc_conv = gated_short_conv(axc, abc, acc, conv_w)
    yc_attn = ctx_gqa(qc, kc, vc, sink)
    yc = jnp.concatenate([yc_conv, yc_attn], axis=-1) @ w_out
    return y, yc


def mla_queries(c_q, g_q, w_uq):
    b, n = c_q.shape[:2]
    q = (rms_norm(c_q, g_q) @ w_uq).reshape(b, n, MLA_HEADS, QK_NOPE + QK_ROPE)
    return jnp.split(q, [QK_NOPE], axis=-1)


def mla_keys_values(c_kv, g_kv, w_ukv):
    b, n = c_kv.shape[:2]
    kv = (rms_norm(c_kv, g_kv) @ w_ukv).reshape(b, n, MLA_HEADS, QK_NOPE + V_HEAD)
    return jnp.split(kv, [QK_NOPE], axis=-1)


def mla_attend(qn, qr, kn, kr, v):
    s = jnp.einsum('bqhd,bkhd->bhqk', qn, kn) + jnp.einsum('bqhd,bkd->bhqk', qr, kr)
    p = jax.nn.softmax(s.astype(jnp.float32) * ((QK_NOPE + QK_ROPE) ** -0.5), axis=-1).astype(v.dtype)
    return jnp.einsum('bhqk,bkhd->bqhd', p, v)


def mla_latent(q_nope, q_rope, k_nope, k_rope, v):
    b, n = q_nope.shape[:2]
    nb = n // BLOCK

    def blocks(t):
        return jnp.moveaxis(t.reshape((b, nb, BLOCK) + t.shape[2:]), 1, 0)

    out = lax.map(lambda qs: mla_attend(qs[0], qs[1], k_nope, k_rope, v), (blocks(q_nope), blocks(q_rope)))
    return jnp.moveaxis(out, 0, 1).reshape(b, n, MLA_HEADS * V_HEAD)


def fourier_mix(f):
    b, n, _ = f.shape
    fg = f.reshape(b, n, FOURIER_GROUPS, FOURIER_GROUP_DIM).astype(jnp.float32)
    return jnp.real(jnp.fft.fft2(fg, axes=(1, 3), norm='ortho')).astype(f.dtype).reshape(b, n, FOURIER_DIM)


def odd_mixer(h, hc, w_in, g_q, w_uq, g_kv, w_ukv, w_out, cos, sin, need_ctx):
    b, n, _ = h.shape
    l = hc.shape[1]
    splits = [Q_LORA, Q_LORA + KV_LORA, Q_LORA + KV_LORA + QK_ROPE]
    cq, ckv, kr, f = jnp.split(h @ w_in, splits, axis=-1)
    q_nope, q_rope = mla_queries(cq, g_q, w_uq)
    q_rope = apply_rope(q_rope, cos, sin)
    k_nope, v = mla_keys_values(ckv, g_kv, w_ukv)
    kr = apply_rope(kr, cos, sin)
    if need_ctx:
        cqc, ckvc, krc, fc = jnp.split(hc @ w_in, splits, axis=-1)
    else:
        ckvc, krc = jnp.split(hc @ w_in[:, Q_LORA:Q_LORA + KV_LORA + QK_ROPE], [KV_LORA], axis=-1)
    kc_nope, vc = mla_keys_values(ckvc, g_kv, w_ukv)
    y_mla = mla_latent(q_nope, q_rope,
                       jnp.concatenate([k_nope, kc_nope], axis=1),
                       jnp.concatenate([kr, krc], axis=1),
                       jnp.concatenate([v, vc], axis=1))
    y = jnp.concatenate([y_mla, fourier_mix(f)], axis=-1) @ w_out
    if not need_ctx:
        return y, None
    qc_nope, qc_rope = mla_queries(cqc, g_q, w_uq)
    yc_mla = mla_attend(qc_nope, qc_rope, kc_nope, krc, vc).reshape(b, l, MLA_HEADS * V_HEAD)
    yc = jnp.concatenate([yc_mla, fourier_mix(fc)], axis=-1) @ w_out
    return y, yc


def route(h, router_w, router_bias):
    scores = jax.nn.sigmoid((h @ router_w).astype(jnp.float32))
    biased = (scores + router_bias.astype(jnp.float32)).reshape(-1, N_GROUPS, EXPERTS_PER_GROUP)
    top_vals, top_idx = lax.top_k(biased, TOP_K)
    group = jnp.argmax(top_vals.sum(-1), axis=-1)
    local = jnp.take_along_axis(top_idx, group[:, None, None], axis=1)[:, 0]
    experts = group[:, None] * EXPERTS_PER_GROUP + local
    w = jnp.take_along_axis(scores, experts, axis=1)
    w = w / jnp.sum(w, axis=-1, keepdims=True)
    return jnp.einsum('tk,tke->te', w, jax.nn.one_hot(experts, N_EXPERTS, dtype=jnp.float32))


def moe(h, router_w, router_bias, w_gate, w_up, w_down):
    gates = route(h, router_w, router_bias).astype(h.dtype)
    out = jnp.zeros_like(h)
    for e in range(N_EXPERTS):
        act = jax.nn.silu(h @ w_gate[e]) * (h @ w_up[e])
        out = out + gates[:, e:e + 1] * (act @ w_down[e])
    return out


def setup_inputs(seed: int = 0) -> dict:
    key = jax.random.key(seed)
    ks = jax.random.split(key, 24)
    D = D_MODEL
    n_even = (DEPTH + 1) // 2
    n_odd = DEPTH // 2

    def nrm(k, shape, fan_in, gain=1.0):
        return jax.random.normal(k, shape, jnp.float32) * (gain * fan_in ** -0.5)

    def gain(k, shape):
        return 1.0 + 0.05 * jax.random.normal(k, shape, jnp.float32)

    return {
        "x": jax.random.normal(ks[0], (BATCH, SEQ, D), jnp.float32),
        "c": jax.random.normal(ks[1], (BATCH, D), jnp.float32),
        "ctx": jax.random.normal(ks[2], (BATCH, CTX_LEN, D), jnp.float32),
        "c_ctx": jax.random.normal(ks[3], (D,), jnp.float32),
        "ada_w": nrm(ks[4], (DEPTH, D, 6 * D), D, 0.5),
        "ada_b": 0.02 * jax.random.normal(ks[5], (DEPTH, 6 * D), jnp.float32),
        "norm_mix_g": gain(ks[6], (DEPTH, D)),
        "norm_ffn_g": gain(ks[7], (DEPTH, D)),
        "ev_w_in": nrm(ks[8], (n_even, D, EVEN_IN), D),
        "ev_conv_w": nrm(ks[9], (n_even, CONV_WIDTH, CONV_DIM), CONV_WIDTH),
        "ev_sink": jax.random.normal(ks[10], (n_even, N_Q_HEADS), jnp.float32),
        "ev_w_out": nrm(ks[11], (n_even, EVEN_OUT, D), EVEN_OUT),
        "od_w_in": nrm(ks[12], (n_odd, D, ODD_IN), D),
        "od_q_norm_g": gain(ks[13], (n_odd, Q_LORA)),
        "od_w_uq": nrm(ks[14], (n_odd, Q_LORA, MLA_HEADS * (QK_NOPE + QK_ROPE)), Q_LORA),
        "od_kv_norm_g": gain(ks[15], (n_odd, KV_LORA)),
        "od_w_ukv": nrm(ks[16], (n_odd, KV_LORA, MLA_HEADS * (QK_NOPE + V_HEAD)), KV_LORA),
        "od_w_out": nrm(ks[17], (n_odd, ODD_OUT, D), ODD_OUT),
        "router_w": nrm(ks[18], (D, N_EXPERTS), D),
        "router_bias": 0.01 * jax.random.normal(ks[19], (N_EXPERTS,), jnp.float32),
        "ex_w_gate": nrm(ks[20], (DEPTH, N_EXPERTS, D, D_FF), D),
        "ex_w_up": nrm(ks[21], (DEPTH, N_EXPERTS, D, D_FF), D),
        "ex_w_down": nrm(ks[22], (DEPTH, N_EXPERTS, D_FF, D), D_FF),
        "final_norm_g": gain(ks[23], (D,)),
    }


def reference(x, c, ctx, c_ctx, ada_w, ada_b, norm_mix_g, norm_ffn_g,
              ev_w_in, ev_conv_w, ev_sink, ev_w_out,
              od_w_in, od_q_norm_g, od_w_uq, od_kv_norm_g, od_w_ukv, od_w_out,
              router_w, router_bias, ex_w_gate, ex_w_up, ex_w_down, final_norm_g):
    b, n, d = x.shape
    l = ctx.shape[1]
    rows = n // GRID_W
    cos_h, sin_h = axial_rope_tables(n, rows, HEAD_DIM)
    cos_r, sin_r = axial_rope_tables(n, rows, QK_ROPE)
    xc = ctx
    for li in range(DEPTH):
        last = li == DEPTH - 1
        sh1, sc1, g1, sh2, sc2, g2 = [m[:, None, :] for m in modulation(c, ada_w[li], ada_b[li])]
        csh1, csc1, cg1, csh2, csc2, cg2 = modulation(c_ctx, ada_w[li], ada_b[li])
        h = modulate(rms_norm(x, norm_mix_g[li]), sh1, sc1)
        hc = modulate(rms_norm(xc, norm_mix_g[li]), csh1, csc1)
        if li % 2 == 0:
            j = li // 2
            y, yc = even_mixer(h, hc, ev_w_in[j], ev_conv_w[j], ev_sink[j], ev_w_out[j], cos_h, sin_h, not last)
        else:
            j = li // 2
            y, yc = odd_mixer(h, hc, od_w_in[j], od_q_norm_g[j], od_w_uq[j], od_kv_norm_g[j], od_w_ukv[j],
                              od_w_out[j], cos_r, sin_r, not last)
        x = x + g1 * y
        h2 = modulate(rms_norm(x, norm_ffn_g[li]), sh2, sc2).reshape(b * n, d)
        if last:
            f = moe(h2, router_w, router_bias, ex_w_gate[li], ex_w_up[li], ex_w_down[li])
            x = x + g2 * f.reshape(b, n, d)
        else:
            xc = xc + cg1 * yc
            h2c = modulate(rms_norm(xc, norm_ffn_g[li]), csh2, csc2).reshape(b * l, d)
            f = moe(jnp.concatenate([h2, h2c], axis=0), router_w, router_bias,
                    ex_w_gate[li], ex_w_up[li], ex_w_down[li])
            x = x + g2 * f[:b * n].reshape(b, n, d)
            xc = xc + cg2 * f[b * n:].reshape(b, l, d)
    return rms_norm(x, final_norm_g)
```

```python
import functools

import jax
import jax.numpy as jnp
import numpy as np
from jax import lax
from jax.experimental import pallas as pl
from jax.experimental.pallas import tpu as pltpu

F32 = jnp.float32
BF16 = jnp.bfloat16

EPS = 1e-6
ROPE_BASE = 10000.0
GRID_W = 64
NEG_INF = -1e30
LANE = 128
TM = 256
WIN = 128
N_Q_HEADS = 8
N_KV_HEADS = 2
HEAD_DIM = 64
CONV_DIM = 512
MLA_HEADS = 8
Q_LORA = 256
KV_LORA = 128
QK_NOPE = 64
QK_ROPE = 32
V_HEAD = 64
FOURIER_DIM = 512
FOURIER_GROUPS = 4
N_EXPERTS = 16
N_GROUPS = 4
EXPERTS_PER_GROUP = 4
VMEM_LIMIT = 56 * 1024 * 1024


def _cparams(sem):
    return pltpu.CompilerParams(dimension_semantics=sem, vmem_limit_bytes=VMEM_LIMIT)


def _norm_mod(x, g, shift, scale):
    var = jnp.mean(x * x, axis=-1, keepdims=True)
    return (x * lax.rsqrt(var + EPS)) * g * (1.0 + scale) + shift


def _rope(x, cos, sin_a, sin_b, shift):
    w = x.shape[-1]
    return x * cos + pltpu.roll(x, w - shift, 1) * sin_a + pltpu.roll(x, shift, 1) * sin_b


def _mod_kernel(c_ref, w_ref, b_ref, o_ref):
    c = c_ref[...]
    o_ref[...] = jnp.dot(jax.nn.silu(c), w_ref[...], preferred_element_type=F32,
                         precision=lax.Precision.HIGHEST) + b_ref[...]


def _modulation(cvec, ada_w, ada_b):
    depth, d, d6 = ada_w.shape
    tn = d6 // 4
    rows = cvec.shape[0]
    return pl.pallas_call(
        _mod_kernel,
        grid=(depth, d6 // tn),
        in_specs=[pl.BlockSpec((rows, d), lambda l, j: (0, 0)),
                  pl.BlockSpec((None, d, tn), lambda l, j: (l, 0, j)),
                  pl.BlockSpec((None, 1, tn), lambda l, j: (l, 0, j))],
        out_specs=pl.BlockSpec((None, rows, tn), lambda l, j: (l, 0, j)),
        out_shape=jax.ShapeDtypeStruct((depth, rows, d6), F32),
        compiler_params=_cparams(("arbitrary", "arbitrary")),
        name="adaln_modulation",
    )(cvec, ada_w, ada_b.reshape(depth, 1, d6))


def _pre_even_kernel(x_ref, mod_ref, g_ref, w_ref, cos_ref, sa_ref, sb_ref,
                     u_ref, ab_ref, q_ref, k_ref, v_ref):
    mod = mod_ref[...]
    h = _norm_mod(x_ref[...], g_ref[...], mod[0:1], mod[1:2])
    p = jnp.dot(h.astype(BF16), w_ref[...], preferred_element_type=F32)
    c = CONV_DIM
    ax, ab, ac = p[:, 0:c], p[:, c:2 * c], p[:, 2 * c:3 * c]
    qw = N_Q_HEADS * LANE
    q = p[:, 3 * c:3 * c + qw]
    k = p[:, 3 * c + qw:3 * c + qw + LANE]
    v = p[:, 3 * c + qw + LANE:3 * c + qw + 2 * LANE]
    cos, sa, sb = cos_ref[...], sa_ref[...], sb_ref[...]
    cos8 = jnp.concatenate([cos] * N_Q_HEADS, axis=1)
    sa8 = jnp.concatenate([sa] * N_Q_HEADS, axis=1)
    sb8 = jnp.concatenate([sb] * N_Q_HEADS, axis=1)
    u_ref[...] = (ac * ax).astype(BF16)
    ab_ref[...] = ab.astype(BF16)
    q_ref[...] = (_rope(q, cos8, sa8, sb8, 16) * (HEAD_DIM ** -0.5)).astype(BF16)
    k_ref[...] = _rope(k, cos, sa, sb, 16).astype(BF16)
    v_ref[...] = v.astype(BF16)


def _pre_even(xa, mods, g_mix, w_in, tabs, li):
    b, s, d = xa.shape
    nt = s // TM
    nl = nt - 1
    wcols = w_in.shape[1]
    qw = N_Q_HEADS * LANE
    tab_spec = pl.BlockSpec((TM, LANE), lambda i, bb: (i, 0))
    tok = lambda w: pl.BlockSpec((None, TM, w), lambda i, bb: (bb, i, 0))
    return pl.pallas_call(
        _pre_even_kernel,
        grid=(nt, b),
        in_specs=[tok(d),
                  pl.BlockSpec((None, None, None, 6, d), lambda i, bb: (li, bb, i // nl, 0, 0)),
                  pl.BlockSpec((None, 1, d), lambda i, bb: (li, 0, 0)),
                  pl.BlockSpec((d, wcols), lambda i, bb: (0, 0)),
                  tab_spec, tab_spec, tab_spec],
        out_specs=[tok(CONV_DIM), tok(CONV_DIM), tok(qw), tok(LANE), tok(LANE)],
        out_shape=[jax.ShapeDtypeStruct((b, s, CONV_DIM), BF16),
                   jax.ShapeDtypeStruct((b, s, CONV_DIM), BF16),
                   jax.ShapeDtypeStruct((b, s, qw), BF16),
                   jax.ShapeDtypeStruct((b, s, LANE), BF16),
                   jax.ShapeDtypeStruct((b, s, LANE), BF16)],
        compiler_params=_cparams(("arbitrary", "arbitrary")),
        name="even_in_proj",
    )(xa, mods, g_mix, w_in, *tabs)


def _even_mix_kernel(sink_ref, q_ref, k0_ref, k1_ref, k2_ref, k3_ref, kc_ref,
                     v0_ref, v1_ref, v2_ref, v3_ref, vc_ref,
                     u_ref, up_ref, un_ref, ab_ref, cw_ref, yc_ref, ya_ref, *, nl, n_lat):
    i = pl.program_id(1)
    nt = pl.num_programs(1)
    is_lat = i < nl
    u = u_ref[...].astype(F32)
    row = lax.broadcasted_iota(jnp.int32, u.shape, 0)
    first = jnp.logical_or(i == 0, i == nl)
    last = jnp.logical_or(i == nl - 1, i == nt - 1)
    prev_row = jnp.where(first, 0.0, up_ref[15:16, :].astype(F32))
    next_row = jnp.where(last, 0.0, un_ref[0:1, :].astype(F32))
    u_prev = jnp.where(row == 0, prev_row, pltpu.roll(u, 1, 0))
    u_next = jnp.where(row == TM - 1, next_row, pltpu.roll(u, TM - 1, 0))
    cw = cw_ref[...]
    conv = u_prev * cw[0:1] + u * cw[1:2] + u_next * cw[2:3]
    yc_ref[...] = (ab_ref[...].astype(F32) * conv).astype(BF16)

    k_all = jnp.concatenate([k0_ref[...], k1_ref[...], k2_ref[...], k3_ref[...], kc_ref[...]], axis=0)
    v_all = jnp.concatenate([v0_ref[...], v1_ref[...], v2_ref[...], v3_ref[...], vc_ref[...]], axis=0)
    nk = k_all.shape[0]
    nwin = 4 * WIN
    qi = lax.broadcasted_iota(jnp.int32, (TM, nk), 0)
    kj = lax.broadcasted_iota(jnp.int32, (TM, nk), 1)
    kpos = i * TM - WIN + kj
    valid_win = (jnp.abs(kj - WIN - qi) <= WIN) & (kpos >= 0) & (kpos < n_lat) & is_lat
    bias = jnp.where((kj >= nwin) | valid_win, 0.0, NEG_INF).astype(F32)
    lane = lax.broadcasted_iota(jnp.int32, (TM, LANE), 1)
    outs = []
    for pair in range(N_Q_HEADS // 2):
        kvh = (2 * pair) // (N_Q_HEADS // N_KV_HEADS)
        o_pair = []
        for h in (2 * pair, 2 * pair + 1):
            q = q_ref[:, h * LANE:(h + 1) * LANE]
            s = lax.dot_general(q, k_all, (((1,), (1,)), ((), ())), preferred_element_type=F32) + bias
            sink = sink_ref[h]
            m = jnp.maximum(jnp.max(s, axis=1, keepdims=True), sink)
            p = jnp.exp(s - m)
            l = jnp.sum(p, axis=1, keepdims=True) + jnp.exp(sink - m)
            o = jnp.dot(p.astype(BF16), v_all, preferred_element_type=F32) / l
            if kvh == 1:
                o = pltpu.roll(o, HEAD_DIM, 1)
            o_pair.append(o)
        outs.append(jnp.where(lane < HEAD_DIM, o_pair[0], pltpu.roll(o_pair[1], HEAD_DIM, 1)))
    ya_ref[...] = jnp.concatenate(outs, axis=1).astype(BF16)


def _even_mix(u, ab, q, k, v, conv_w, sink, n_lat):
    b, s, _ = u.shape
    nt = s // TM
    nl = n_lat // TM
    nkb = s // WIN
    rpt = TM // WIN
    qw = q.shape[-1]

    def kblk(off):
        return pl.BlockSpec((None, WIN, LANE),
                            lambda bb, i: (bb, jnp.clip(i * rpt + off, 0, nkb - 1), 0))

    ctx_spec = pl.BlockSpec((None, s - n_lat, LANE), lambda bb, i: (bb, n_lat // (s - n_lat), 0))
    halo = TM // 16
    u16 = u.reshape(b, s // 16, 16, CONV_DIM)
    kern = functools.partial(_even_mix_kernel, nl=nl, n_lat=n_lat)
    tok = lambda w: pl.BlockSpec((None, TM, w), lambda bb, i: (bb, i, 0))
    return pl.pallas_call(
        kern,
        grid=(b, nt),
        in_specs=[pl.BlockSpec(memory_space=pltpu.SMEM),
                  tok(qw),
                  kblk(-1), kblk(0), kblk(1), kblk(2), ctx_spec,
                  kblk(-1), kblk(0), kblk(1), kblk(2), ctx_spec,
                  tok(CONV_DIM),
                  pl.BlockSpec((None, None, 16, CONV_DIM),
                               lambda bb, i: (bb, jnp.maximum(i * halo - 1, 0), 0, 0)),
                  pl.BlockSpec((None, None, 16, CONV_DIM),
                               lambda bb, i: (bb, jnp.minimum((i + 1) * halo, s // 16 - 1), 0, 0)),
                  tok(CONV_DIM),
                  pl.BlockSpec((3, CONV_DIM), lambda bb, i: (0, 0))],
        out_specs=[tok(CONV_DIM), tok(N_Q_HEADS * HEAD_DIM)],
        out_shape=[jax.ShapeDtypeStruct((b, s, CONV_DIM), BF16),
                   jax.ShapeDtypeStruct((b, s, N_Q_HEADS * HEAD_DIM), BF16)],
        compiler_params=_cparams(("arbitrary", "arbitrary")),
        name="even_conv_window_attn",
    )(sink, q, k, k, k, k, k, v, v, v, v, v, u, u16, u16, ab, conv_w)


def _pre_odd_kernel(x_ref, mod_ref, g_ref, w_ref, gq_ref, wuq_ref, gkv_ref, wuk_ref, wuv_ref,
                    wf_ref, cos_ref, sa_ref, sb_ref, q_ref, k_ref, v_ref, y_ref):
    mod = mod_ref[...]
    h = _norm_mod(x_ref[...], g_ref[...], mod[0:1], mod[1:2])
    p = jnp.dot(h.astype(BF16), w_ref[...], preferred_element_type=F32)
    cq = p[:, 0:Q_LORA]
    ckv = p[:, Q_LORA:Q_LORA + KV_LORA]
    kr = p[:, Q_LORA + KV_LORA:Q_LORA + KV_LORA + LANE]
    f = p[:, Q_LORA + KV_LORA + LANE:]
    cos, sa, sb = cos_ref[...], sa_ref[...], sb_ref[...]
    cos8 = jnp.concatenate([cos] * MLA_HEADS, axis=1)
    sa8 = jnp.concatenate([sa] * MLA_HEADS, axis=1)
    sb8 = jnp.concatenate([sb] * MLA_HEADS, axis=1)

    cqn = cq * lax.rsqrt(jnp.mean(cq * cq, axis=-1, keepdims=True) + EPS) * gq_ref[...]
    q = jnp.dot(cqn.astype(BF16), wuq_ref[...], preferred_element_type=F32)
    q = _rope(q, cos8, sa8, sb8, 8) * ((QK_NOPE + QK_ROPE) ** -0.5)
    q_ref[...] = q.astype(BF16)

    ckvn = (ckv * lax.rsqrt(jnp.mean(ckv * ckv, axis=-1, keepdims=True) + EPS) * gkv_ref[...]).astype(BF16)
    kn = jnp.dot(ckvn, wuk_ref[...], preferred_element_type=F32)
    krr = _rope(kr, cos, sa, sb, 8)
    k_ref[...] = (kn + jnp.concatenate([krr] * MLA_HEADS, axis=1)).astype(BF16)
    v_ref[...] = jnp.dot(ckvn, wuv_ref[...], preferred_element_type=F32).astype(BF16)
    y_ref[...] = jnp.dot(f.astype(BF16), wf_ref[...], preferred_element_type=F32).astype(BF16)


def _pre_odd(xa, mods, g_mix, w_in, gq, wuq, gkv, wuk, wuv, wf, tabs, li):
    b, s, d = xa.shape
    nt = s // TM
    nl = nt - 1
    hw = MLA_HEADS * LANE
    vw = MLA_HEADS * V_HEAD
    tab_spec = pl.BlockSpec((TM, LANE), lambda i, bb: (i, 0))
    tok = lambda w: pl.BlockSpec((None, TM, w), lambda i, bb: (bb, i, 0))
    full = lambda a: pl.BlockSpec(a.shape, lambda i, bb: (0,) * a.ndim)
    return pl.pallas_call(
        _pre_odd_kernel,
        grid=(nt, b),
        in_specs=[tok(d),
                  pl.BlockSpec((None, None, None, 6, d), lambda i, bb: (li, bb, i // nl, 0, 0)),
                  pl.BlockSpec((None, 1, d), lambda i, bb: (li, 0, 0)),
                  full(w_in), full(gq), full(wuq), full(gkv), full(wuk), full(wuv), full(wf),
                  tab_spec, tab_spec, tab_spec],
        out_specs=[tok(hw), tok(hw), tok(vw), tok(2 * FOURIER_DIM)],
        out_shape=[jax.ShapeDtypeStruct((b, s, hw), BF16),
                   jax.ShapeDtypeStruct((b, s, hw), BF16),
                   jax.ShapeDtypeStruct((b, s, vw), BF16),
                   jax.ShapeDtypeStruct((b, s, 2 * FOURIER_DIM), BF16)],
        compiler_params=_cparams(("arbitrary", "arbitrary")),
        name="odd_in_proj",
    )(xa, mods, g_mix, w_in, gq, wuq, gkv, wuk, wuv, wf, *tabs)


def _mla_kernel(q_ref, k_ref, v_ref, o_ref, *, nl, n_lat):
    i = pl.program_id(1)
    s_all = k_ref.shape[0]
    lane = lax.broadcasted_iota(jnp.int32, (TM, LANE), 1)

    def attend(k0, nk):
        outs = []
        for pair in range(MLA_HEADS // 2):
            vp = v_ref[k0:k0 + nk, pair * LANE:(pair + 1) * LANE]
            o_pair = []
            for h in (2 * pair, 2 * pair + 1):
                q = q_ref[:, h * LANE:(h + 1) * LANE]
                k = k_ref[k0:k0 + nk, h * LANE:(h + 1) * LANE]
                s = lax.dot_general(q, k, (((1,), (1,)), ((), ())), preferred_element_type=F32)
                m = jnp.max(s, axis=1, keepdims=True)
                p = jnp.exp(s - m)
                l = jnp.sum(p, axis=1, keepdims=True)
                o_pair.append(jnp.dot(p.astype(BF16), vp, preferred_element_type=F32) / l)
            outs.append(jnp.where(lane < V_HEAD, o_pair[0], o_pair[1]))
        o_ref[...] = jnp.concatenate(outs, axis=1).astype(BF16)

    @pl.when(i < nl)
    def _():
        attend(0, s_all)

    @pl.when(i >= nl)
    def _():
        attend(n_lat, s_all - n_lat)


def _mla(q, k, v, n_lat):
    b, s, hw = q.shape
    vw = v.shape[-1]
    nt = s // TM
    kern = functools.partial(_mla_kernel, nl=n_lat // TM, n_lat=n_lat)
    return pl.pallas_call(
        kern,
        grid=(b, nt),
        in_specs=[pl.BlockSpec((None, TM, hw), lambda bb, i: (bb, i, 0)),
                  pl.BlockSpec((None, s, hw), lambda bb, i: (bb, 0, 0)),
                  pl.BlockSpec((None, s, vw), lambda bb, i: (bb, 0, 0))],
        out_specs=pl.BlockSpec((None, TM, vw), lambda bb, i: (bb, i, 0)),
        out_shape=jax.ShapeDtypeStruct((b, s, vw), BF16),
        compiler_params=_cparams(("arbitrary", "arbitrary")),
        name="mla_attention",
    )(q, k, v)


def _dft_kernel(ac_ref, as_ref, yc_ref, ys_ref, o_ref):
    o_ref[...] = (jnp.dot(ac_ref[...], yc_ref[...], preferred_element_type=F32)
                  - jnp.dot(as_ref[...], ys_ref[...], preferred_element_type=F32)).astype(BF16)


def _dft_kernel_alias(ac_ref, as_ref, yc_ref, ys_ref, prev_ref, o_ref):
    del prev_ref
    _dft_kernel(ac_ref, as_ref, yc_ref, ys_ref, o_ref)


def _fourier(ycs, dft_lat, dft_ctx, n_lat):
    b, s, _ = ycs.shape
    fd = FOURIER_DIM
    n_ctx = s - n_lat
    tq = 512
    ac, as_ = dft_lat
    lat = pl.pallas_call(
        _dft_kernel,
        grid=(n_lat // tq, b),
        in_specs=[pl.BlockSpec((tq, n_lat), lambda i, bb: (i, 0)),
                  pl.BlockSpec((tq, n_lat), lambda i, bb: (i, 0)),
                  pl.BlockSpec((None, n_lat, fd), lambda i, bb: (bb, 0, 0)),
                  pl.BlockSpec((None, n_lat, fd), lambda i, bb: (bb, 0, 1))],
        out_specs=pl.BlockSpec((None, tq, fd), lambda i, bb: (bb, i, 0)),
        out_shape=jax.ShapeDtypeStruct((b, s, fd), BF16),
        compiler_params=_cparams(("arbitrary", "arbitrary")),
        name="fourier_latent",
    )(ac, as_, ycs, ycs)
    cc, cs = dft_ctx
    cblk = n_lat // n_ctx
    return pl.pallas_call(
        _dft_kernel_alias,
        grid=(b,),
        in_specs=[pl.BlockSpec((n_ctx, n_ctx), lambda bb: (0, 0)),
                  pl.BlockSpec((n_ctx, n_ctx), lambda bb: (0, 0)),
                  pl.BlockSpec((None, n_ctx, fd), lambda bb: (bb, cblk, 0)),
                  pl.BlockSpec((None, n_ctx, fd), lambda bb: (bb, cblk, 1)),
                  pl.BlockSpec(memory_space=pl.ANY)],
        out_specs=pl.BlockSpec((None, n_ctx, fd), lambda bb: (bb, cblk, 0)),
        out_shape=jax.ShapeDtypeStruct((b, s, fd), BF16),
        input_output_aliases={4: 0},
        compiler_params=_cparams(("arbitrary",)),
        name="fourier_context",
    )(cc, cs, ycs, ycs, lat)


def _post_kernel(x_ref, ya_ref, yb_ref, mod_ref, wa_ref, wb_ref, g_ref, rw_ref, rb_ref,
                 xo_ref, h2_ref, gt_ref):
    mod = mod_ref[...]
    y = (jnp.dot(ya_ref[...], wa_ref[...], preferred_element_type=F32)
         + jnp.dot(yb_ref[...], wb_ref[...], preferred_element_type=F32))
    x = x_ref[...] + mod[2:3] * y
    xo_ref[...] = x
    h2 = _norm_mod(x, g_ref[...], mod[3:4], mod[4:5]).astype(BF16)
    h2_ref[...] = h2
    logits = lax.dot_general(rw_ref[...], h2, (((1,), (1,)), ((), ())), preferred_element_type=F32)
    scores = jax.nn.sigmoid(logits)
    biased = scores + rb_ref[...]
    sc = [scores[e:e + 1, :] for e in range(N_EXPERTS)]
    bs = [biased[e:e + 1, :] for e in range(N_EXPERTS)]
    sel, gval = [], []
    for g in range(N_GROUPS):
        ids = range(g * EXPERTS_PER_GROUP, (g + 1) * EXPERTS_PER_GROUP)
        gsum = None
        for e in ids:
            rank = sum(((bs[j] > bs[e]) if j > e else (bs[j] >= bs[e])).astype(jnp.int32)
                       for j in ids if j != e)
            s_e = rank < 2
            sel.append(s_e)
            term = jnp.where(s_e, bs[e], 0.0)
            gsum = term if gsum is None else gsum + term
        gval.append(gsum)
    gsel = []
    for g in range(N_GROUPS):
        ok = None
        for j in range(N_GROUPS):
            if j == g:
                continue
            c = (gval[g] > gval[j]) if j < g else (gval[g] >= gval[j])
            ok = c if ok is None else (ok & c)
        gsel.append(ok)
    chosen = [sel[e] & gsel[e // EXPERTS_PER_GROUP] for e in range(N_EXPERTS)]
    wsum = sum(jnp.where(chosen[e], sc[e], 0.0) for e in range(N_EXPERTS))
    gates = [jnp.where(chosen[e], sc[e], 0.0) / wsum for e in range(N_EXPERTS)]
    gt_ref[...] = jnp.concatenate(gates, axis=0)


def _post(xa, ya, yb, mods, w_out_a, w_out_b, g_ffn, rw_t, rb, li):
    b, s, d = xa.shape
    nt = s // TM
    nl = nt - 1
    half = ya.shape[-1]
    tok = lambda w: pl.BlockSpec((None, TM, w), lambda bb, i: (bb, i, 0))
    return pl.pallas_call(
        _post_kernel,
        grid=(b, nt),
        in_specs=[tok(d), tok(half), tok(half),
                  pl.BlockSpec((None, None, None, 6, d), lambda bb, i: (li, bb, i // nl, 0, 0)),
                  pl.BlockSpec((half, d), lambda bb, i: (0, 0)),
                  pl.BlockSpec((half, d), lambda bb, i: (0, 0)),
                  pl.BlockSpec((None, 1, d), lambda bb, i: (li, 0, 0)),
                  pl.BlockSpec((N_EXPERTS, d), lambda bb, i: (0, 0)),
                  pl.BlockSpec((N_EXPERTS, 1), lambda bb, i: (0, 0))],
        out_specs=[tok(d), tok(d),
                   pl.BlockSpec((None, N_EXPERTS, TM), lambda bb, i: (bb, 0, i))],
        out_shape=[jax.ShapeDtypeStruct((b, s, d), F32),
                   jax.ShapeDtypeStruct((b, s, d), BF16),
                   jax.ShapeDtypeStruct((b, N_EXPERTS, s), F32)],
        input_output_aliases={0: 0},
        compiler_params=_cparams(("arbitrary", "arbitrary")),
        name="out_proj_router",
    )(xa, ya, yb, mods, w_out_a, w_out_b, g_ffn, rw_t, rb)


def _moe_dense_kernel(h_ref, g_ref, wg_ref, wu_ref, wd_ref, o_ref, acc_ref):
    e = pl.program_id(1)

    @pl.when(e == 0)
    def _():
        acc_ref[...] = jnp.zeros_like(acc_ref)

    h = h_ref[...]
    a = jnp.dot(h, wg_ref[...], preferred_element_type=F32)
    u = jnp.dot(h, wu_ref[...], preferred_element_type=F32)
    act = (jax.nn.silu(a) * u).astype(BF16)
    gall = g_ref[...]
    col = lax.broadcasted_iota(jnp.int32, gall.shape, 1)
    gate = jnp.sum(jnp.where(col == e, gall, 0.0), axis=1, keepdims=True)
    acc_ref[...] += gate * jnp.dot(act, wd_ref[...], preferred_element_type=F32)

    @pl.when(e == pl.num_programs(1) - 1)
    def _():
        o_ref[...] = acc_ref[...].astype(BF16)


def _moe_dense(h2, gates, wg, wu, wd, li):
    t, d = h2.shape
    dff = wg.shape[-1]
    tmd = 1024 if t % 1024 == 0 else TM
    return pl.pallas_call(
        _moe_dense_kernel,
        grid=(t // tmd, N_EXPERTS),
        in_specs=[pl.BlockSpec((tmd, d), lambda i, e: (i, 0)),
                  pl.BlockSpec((tmd, N_EXPERTS), lambda i, e: (i, 0)),
                  pl.BlockSpec((None, None, d, dff), lambda i, e: (li, e, 0, 0)),
                  pl.BlockSpec((None, None, d, dff), lambda i, e: (li, e, 0, 0)),
                  pl.BlockSpec((None, None, dff, d), lambda i, e: (li, e, 0, 0))],
        out_specs=pl.BlockSpec((tmd, d), lambda i, e: (i, 0)),
        out_shape=jax.ShapeDtypeStruct((t, d), BF16),
        scratch_shapes=[pltpu.VMEM((tmd, d), F32)],
        compiler_params=_cparams(("arbitrary", "arbitrary")),
        name="moe_dense",
    )(h2, gates, wg, wu, wd)


def _combine_kernel(x_ref, f_ref, mod_ref, xo_ref):
    mod = mod_ref[...]
    xo_ref[...] = x_ref[...] + mod[5:6] * f_ref[...].astype(F32)


def _combine(xa, f, mods, li):
    b, s, d = xa.shape
    nt = s // TM
    nl = nt - 1
    tok = pl.BlockSpec((None, TM, d), lambda bb, i: (bb, i, 0))
    return pl.pallas_call(
        _combine_kernel,
        grid=(b, nt),
        in_specs=[tok, tok,
                  pl.BlockSpec((None, None, None, 6, d), lambda bb, i: (li, bb, i // nl, 0, 0))],
        out_specs=tok,
        out_shape=jax.ShapeDtypeStruct((b, s, d), F32),
        input_output_aliases={0: 0},
        compiler_params=_cparams(("arbitrary", "arbitrary")),
        name="moe_residual",
    )(xa, f, mods)


def _final_kernel(x_ref, f_ref, mod_ref, g_ref, o_ref):
    mod = mod_ref[...]
    x = x_ref[...] + mod[5:6] * f_ref[...].astype(F32)
    var = jnp.mean(x * x, axis=-1, keepdims=True)
    o_ref[...] = (x * lax.rsqrt(var + EPS)) * g_ref[...]


def _final(xa, f, mods, g_final, li, n_lat):
    b, s, d = xa.shape
    tok = pl.BlockSpec((None, TM, d), lambda bb, i: (bb, i, 0))
    return pl.pallas_call(
        _final_kernel,
        grid=(b, n_lat // TM),
        in_specs=[tok, tok,
                  pl.BlockSpec((None, None, None, 6, d), lambda bb, i: (li, bb, 0, 0, 0)),
                  pl.BlockSpec((1, d), lambda bb, i: (0, 0))],
        out_specs=tok,
        out_shape=jax.ShapeDtypeStruct((b, n_lat, d), F32),
        compiler_params=_cparams(("arbitrary", "arbitrary")),
        name="moe_residual_final_norm",
    )(xa, f, mods, g_final)


def _rope_tables(n_lat, n_ctx, rot_dim, lane_off):
    rows = n_lat // GRID_W
    r = jnp.repeat(jnp.arange(rows, dtype=F32), GRID_W)[:n_lat]
    c = jnp.tile(jnp.arange(GRID_W, dtype=F32), rows)[:n_lat]
    axis_dim = rot_dim // 2
    inv_freq = ROPE_BASE ** (-jnp.arange(0, axis_dim, 2, dtype=F32) / axis_dim)
    ang_r = r[:, None] * inv_freq
    ang_c = c[:, None] * inv_freq
    ang = jnp.concatenate([ang_r, ang_r, ang_c, ang_c], axis=-1)
    cos, sin = jnp.cos(ang), jnp.sin(ang)
    grp = (np.arange(rot_dim) // (rot_dim // 4)) % 2
    sin_a = jnp.where(grp == 0, -sin, 0.0)
    sin_b = jnp.where(grp == 1, sin, 0.0)

    def place(t, fill):
        reps = 2 if rot_dim == 64 else 1
        blk = jnp.full((n_lat, LANE), fill, F32)
        for k in range(reps):
            blk = blk.at[:, lane_off + k * rot_dim:lane_off + (k + 1) * rot_dim].set(t)
        ctx = jnp.full((n_ctx, LANE), fill, F32)
        return jnp.concatenate([blk, ctx], axis=0)

    return place(cos, 1.0), place(sin_a, 0.0), place(sin_b, 0.0)


def _dft_mats(n, scale):
    k = jnp.arange(n, dtype=jnp.int32)
    m = (k[:, None] * k[None, :]) % n
    ang = m.astype(F32) * (2.0 * np.pi / n)
    return (jnp.cos(ang) * scale).astype(BF16), (jnp.sin(ang) * scale).astype(BF16)


def _even_w_in(w):
    d = w.shape[0]
    c3 = 3 * CONV_DIM
    qcols = []
    per_kv = N_Q_HEADS // N_KV_HEADS
    for h in range(N_Q_HEADS):
        wq = w[:, c3 + h * HEAD_DIM:c3 + (h + 1) * HEAD_DIM]
        z = jnp.zeros((d, HEAD_DIM), w.dtype)
        qcols.append(jnp.concatenate([wq, z] if h // per_kv == 0 else [z, wq], axis=1))
    return jnp.concatenate([w[:, :c3]] + qcols + [w[:, c3 + N_Q_HEADS * HEAD_DIM:]], axis=1).astype(BF16)


def _odd_w_in(w):
    d = w.shape[0]
    o = Q_LORA + KV_LORA
    kr = jnp.concatenate([jnp.zeros((d, QK_NOPE), w.dtype), w[:, o:o + QK_ROPE],
                          jnp.zeros((d, LANE - QK_NOPE - QK_ROPE), w.dtype)], axis=1)
    return jnp.concatenate([w[:, :o], kr, w[:, o + QK_ROPE:]], axis=1).astype(BF16)


def _pad_heads(w, per_head, keep_from, keep, heads):
    rows = w.shape[0]
    w3 = w.reshape(rows, heads, per_head)[:, :, keep_from:keep_from + keep]
    w3 = jnp.pad(w3, ((0, 0), (0, 0), (0, LANE - keep)))
    return w3.reshape(rows, heads * LANE).astype(BF16)


def _channel_dft():
    gd = FOURIER_DIM // FOURIER_GROUPS
    k = np.arange(gd)
    ang = 2.0 * np.pi * ((k[:, None] * k[None, :]) % gd) / gd
    eye = np.eye(FOURIER_GROUPS)
    wc = np.kron(eye, np.cos(ang))
    ws = np.kron(eye, np.sin(ang))
    return jnp.asarray(np.concatenate([wc, ws], axis=1), BF16)


def kernel(x, c, ctx, c_ctx, ada_w, ada_b, norm_mix_g, norm_ffn_g, ev_w_in, ev_conv_w, ev_sink, ev_w_out,
           od_w_in, od_q_norm_g, od_w_uq, od_kv_norm_g, od_w_ukv, od_w_out, router_w, router_bias,
           ex_w_gate, ex_w_up, ex_w_down, final_norm_g):
    b, n, d = x.shape
    l = ctx.shape[1]
    s = n + l
    depth = ada_w.shape[0]
    assert l == TM and n % 512 == 0

    rows = 8 * ((b + 1 + 7) // 8)
    cvec = jnp.concatenate([c, c_ctx[None, :], jnp.zeros((rows - b - 1, d), F32)], axis=0)
    mod_all = _modulation(cvec, ada_w, ada_b).reshape(depth, rows, 6, d)
    mods = jnp.stack([mod_all[:, :b], jnp.broadcast_to(mod_all[:, b:b + 1], (depth, b, 6, d))], axis=2)

    tabs_even = _rope_tables(n, l, HEAD_DIM, 0)
    tabs_odd = _rope_tables(n, l, QK_ROPE, QK_NOPE)
    gd = FOURIER_DIM // FOURIER_GROUPS
    dft_lat = _dft_mats(n, (n * gd) ** -0.5)
    dft_ctx = _dft_mats(l, (l * gd) ** -0.5)
    wf = _channel_dft()

    g_mix = norm_mix_g.reshape(depth, 1, d)
    g_ffn = norm_ffn_g.reshape(depth, 1, d)
    rw_t = router_w.T.astype(BF16)
    rb = router_bias.reshape(N_EXPERTS, 1).astype(F32)
    wg, wu, wd = ex_w_gate.astype(BF16), ex_w_up.astype(BF16), ex_w_down.astype(BF16)

    xa = jnp.concatenate([x, ctx], axis=1)
    out = None
    for li in range(depth):
        j = li // 2
        if li % 2 == 0:
            u, ab, q, k, v = _pre_even(xa, mods, g_mix, _even_w_in(ev_w_in[j]), tabs_even, li)
            ya, yb = _even_mix(u, ab, q, k, v, ev_conv_w[j], ev_sink[j], n)
            w_out = ev_w_out[j].astype(BF16)
        else:
            q, k, v, ycs = _pre_odd(
                xa, mods, g_mix, _odd_w_in(od_w_in[j]),
                od_q_norm_g[j].reshape(1, Q_LORA),
                _pad_heads(od_w_uq[j], QK_NOPE + QK_ROPE, 0, QK_NOPE + QK_ROPE, MLA_HEADS),
                od_kv_norm_g[j].reshape(1, KV_LORA),
                _pad_heads(od_w_ukv[j], QK_NOPE + V_HEAD, 0, QK_NOPE, MLA_HEADS),
                od_w_ukv[j].reshape(KV_LORA, MLA_HEADS, QK_NOPE + V_HEAD)[:, :, QK_NOPE:]
                .reshape(KV_LORA, MLA_HEADS * V_HEAD).astype(BF16),
                wf, tabs_odd, li)
            ya = _mla(q, k, v, n)
            yb = _fourier(ycs, dft_lat, dft_ctx, n)
            w_out = od_w_out[j].astype(BF16)
        half = w_out.shape[0] // 2
        xa, h2, gates_t = _post(xa, ya, yb, mods, w_out[:half], w_out[half:], g_ffn, rw_t, rb, li)
        gates = jnp.swapaxes(gates_t, 1, 2).reshape(b * s, N_EXPERTS)
        f = _moe_dense(h2.reshape(b * s, d), gates, wg, wu, wd, li).reshape(b, s, d)
        if li == depth - 1:
            out = _final(xa, f, mods, final_norm_g.reshape(1, d), li, n)
        else:
            xa = _combine(xa, f, mods, li)
    return out
```

```python
import functools

import jax
import jax.numpy as jnp
import numpy as np
from jax import lax
from jax.experimental import pallas as pl
from jax.experimental.pallas import tpu as pltpu

F32 = jnp.float32
BF16 = jnp.bfloat16

EPS = 1e-6
ROPE_BASE = 10000.0
GRID_W = 64
NEG_INF = -1e30
LANE = 128
TM = 256
WIN = 128
N_Q_HEADS = 8
N_KV_HEADS = 2
HEAD_DIM = 64
CONV_DIM = 512
MLA_HEADS = 8
Q_LORA = 256
KV_LORA = 128
QK_NOPE = 64
QK_ROPE = 32
V_HEAD = 64
FOURIER_DIM = 512
FOURIER_GROUPS = 4
N_EXPERTS = 16
N_GROUPS = 4
EXPERTS_PER_GROUP = 4
PAIRS = ((0, 1), (0, 2), (0, 3), (1, 2), (1, 3), (2, 3))
NB_PAD = 32
TME = 256
VMEM_LIMIT = 56 * 1024 * 1024


def _cparams(sem):
    return pltpu.CompilerParams(dimension_semantics=sem, vmem_limit_bytes=VMEM_LIMIT)


def _norm_mod(x, g, shift, scale):
    var = jnp.mean(x * x, axis=-1, keepdims=True)
    return (x * lax.rsqrt(var + EPS)) * g * (1.0 + scale) + shift


def _rope(x, cos, sin_a, sin_b, shift):
    w = x.shape[-1]
    return x * cos + pltpu.roll(x, w - shift, 1) * sin_a + pltpu.roll(x, shift, 1) * sin_b


def _mod_kernel(c_ref, w_ref, b_ref, o_ref):
    c = c_ref[...]
    o_ref[...] = jnp.dot(jax.nn.silu(c), w_ref[...], preferred_element_type=F32,
                         precision=lax.Precision.HIGHEST) + b_ref[...]


def _modulation(cvec, ada_w, ada_b):
    depth, d, d6 = ada_w.shape
    tn = d6 // 4
    rows = cvec.shape[0]
    return pl.pallas_call(
        _mod_kernel,
        grid=(depth, d6 // tn),
        in_specs=[pl.BlockSpec((rows, d), lambda l, j: (0, 0)),
                  pl.BlockSpec((None, d, tn), lambda l, j: (l, 0, j)),
                  pl.BlockSpec((None, 1, tn), lambda l, j: (l, 0, j))],
        out_specs=pl.BlockSpec((None, rows, tn), lambda l, j: (l, 0, j)),
        out_shape=jax.ShapeDtypeStruct((depth, rows, d6), F32),
        compiler_params=_cparams(("arbitrary", "arbitrary")),
        name="adaln_modulation",
    )(cvec, ada_w, ada_b.reshape(depth, 1, d6))


def _pre_even_kernel(x_ref, mod_ref, g_ref, w_ref, cos_ref, sa_ref, sb_ref,
                     u_ref, ab_ref, q_ref, k_ref, v_ref):
    mod = mod_ref[...]
    h = _norm_mod(x_ref[...], g_ref[...], mod[0:1], mod[1:2])
    p = jnp.dot(h.astype(BF16), w_ref[...], preferred_element_type=F32)
    c = CONV_DIM
    ax, ab, ac = p[:, 0:c], p[:, c:2 * c], p[:, 2 * c:3 * c]
    qw = N_Q_HEADS * LANE
    q = p[:, 3 * c:3 * c + qw]
    k = p[:, 3 * c + qw:3 * c + qw + LANE]
    v = p[:, 3 * c + qw + LANE:3 * c + qw + 2 * LANE]
    cos, sa, sb = cos_ref[...], sa_ref[...], sb_ref[...]
    cos8 = jnp.concatenate([cos] * N_Q_HEADS, axis=1)
    sa8 = jnp.concatenate([sa] * N_Q_HEADS, axis=1)
    sb8 = jnp.concatenate([sb] * N_Q_HEADS, axis=1)
    u_ref[...] = (ac * ax).astype(BF16)
    ab_ref[...] = ab.astype(BF16)
    q_ref[...] = (_rope(q, cos8, sa8, sb8, 16) * (HEAD_DIM ** -0.5)).astype(BF16)
    k_ref[...] = _rope(k, cos, sa, sb, 16).astype(BF16)
    v_ref[...] = v.astype(BF16)


def _pre_even(xa, mods, g_mix, w_in, tabs, li):
    b, s, d = xa.shape
    nt = s // TM
    nl = nt - 1
    wcols = w_in.shape[1]
    qw = N_Q_HEADS * LANE
    tab_spec = pl.BlockSpec((TM, LANE), lambda i, bb: (i, 0))
    tok = lambda w: pl.BlockSpec((None, TM, w), lambda i, bb: (bb, i, 0))
    return pl.pallas_call(
        _pre_even_kernel,
        grid=(nt, b),
        in_specs=[tok(d),
                  pl.BlockSpec((None, None, None, 6, d), lambda i, bb: (li, bb, i // nl, 0, 0)),
                  pl.BlockSpec((None, 1, d), lambda i, bb: (li, 0, 0)),
                  pl.BlockSpec((d, wcols), lambda i, bb: (0, 0)),
                  tab_spec, tab_spec, tab_spec],
        out_specs=[tok(CONV_DIM), tok(CONV_DIM), tok(qw), tok(LANE), tok(LANE)],
        out_shape=[jax.ShapeDtypeStruct((b, s, CONV_DIM), BF16),
                   jax.ShapeDtypeStruct((b, s, CONV_DIM), BF16),
                   jax.ShapeDtypeStruct((b, s, qw), BF16),
                   jax.ShapeDtypeStruct((b, s, LANE), BF16),
                   jax.ShapeDtypeStruct((b, s, LANE), BF16)],
        compiler_params=_cparams(("arbitrary", "arbitrary")),
        name="even_in_proj",
    )(xa, mods, g_mix, w_in, *tabs)


def _even_mix_kernel(sink_ref, q_ref, k0_ref, k1_ref, k2_ref, k3_ref, kc_ref,
                     v0_ref, v1_ref, v2_ref, v3_ref, vc_ref,
                     u_ref, up_ref, un_ref, ab_ref, cw_ref, yc_ref, ya_ref, *, nl, n_lat):
    i = pl.program_id(1)
    nt = pl.num_programs(1)
    is_lat = i < nl
    u = u_ref[...].astype(F32)
    row = lax.broadcasted_iota(jnp.int32, u.shape, 0)
    first = jnp.logical_or(i == 0, i == nl)
    last = jnp.logical_or(i == nl - 1, i == nt - 1)
    prev_row = jnp.where(first, 0.0, up_ref[15:16, :].astype(F32))
    next_row = jnp.where(last, 0.0, un_ref[0:1, :].astype(F32))
    u_prev = jnp.where(row == 0, prev_row, pltpu.roll(u, 1, 0))
    u_next = jnp.where(row == TM - 1, next_row, pltpu.roll(u, TM - 1, 0))
    cw = cw_ref[...]
    conv = u_prev * cw[0:1] + u * cw[1:2] + u_next * cw[2:3]
    yc_ref[...] = (ab_ref[...].astype(F32) * conv).astype(BF16)

    k_all = jnp.concatenate([k0_ref[...], k1_ref[...], k2_ref[...], k3_ref[...], kc_ref[...]], axis=0)
    v_all = jnp.concatenate([v0_ref[...], v1_ref[...], v2_ref[...], v3_ref[...], vc_ref[...]], axis=0)
    nk = k_all.shape[0]
    nwin = 4 * WIN
    qi = lax.broadcasted_iota(jnp.int32, (TM, nk), 0)
    kj = lax.broadcasted_iota(jnp.int32, (TM, nk), 1)
    kpos = i * TM - WIN + kj
    valid_win = (jnp.abs(kj - WIN - qi) <= WIN) & (kpos >= 0) & (kpos < n_lat) & is_lat
    bias = jnp.where((kj >= nwin) | valid_win, 0.0, NEG_INF).astype(F32)
    lane = lax.broadcasted_iota(jnp.int32, (TM, LANE), 1)
    outs = []
    for pair in range(N_Q_HEADS // 2):
        kvh = (2 * pair) // (N_Q_HEADS // N_KV_HEADS)
        o_pair = []
        for h in (2 * pair, 2 * pair + 1):
            q = q_ref[:, h * LANE:(h + 1) * LANE]
            s = lax.dot_general(q, k_all, (((1,), (1,)), ((), ())), preferred_element_type=F32) + bias
            sink = sink_ref[h]
            m = jnp.maximum(jnp.max(s, axis=1, keepdims=True), sink)
            p = jnp.exp(s - m)
            l = jnp.sum(p, axis=1, keepdims=True) + jnp.exp(sink - m)
            o = jnp.dot(p.astype(BF16), v_all, preferred_element_type=F32) / l
            if kvh == 1:
                o = pltpu.roll(o, HEAD_DIM, 1)
            o_pair.append(o)
        outs.append(jnp.where(lane < HEAD_DIM, o_pair[0], pltpu.roll(o_pair[1], HEAD_DIM, 1)))
    ya_ref[...] = jnp.concatenate(outs, axis=1).astype(BF16)


def _even_mix(u, ab, q, k, v, conv_w, sink, n_lat):
    b, s, _ = u.shape
    nt = s // TM
    nl = n_lat // TM
    nkb = s // WIN
    rpt = TM // WIN
    qw = q.shape[-1]

    def kblk(off):
        return pl.BlockSpec((None, WIN, LANE),
                            lambda bb, i: (bb, jnp.clip(i * rpt + off, 0, nkb - 1), 0))

    ctx_spec = pl.BlockSpec((None, s - n_lat, LANE), lambda bb, i: (bb, n_lat // (s - n_lat), 0))
    halo = TM // 16
    u16 = u.reshape(b, s // 16, 16, CONV_DIM)
    kern = functools.partial(_even_mix_kernel, nl=nl, n_lat=n_lat)
    tok = lambda w: pl.BlockSpec((None, TM, w), lambda bb, i: (bb, i, 0))
    return pl.pallas_call(
        kern,
        grid=(b, nt),
        in_specs=[pl.BlockSpec(memory_space=pltpu.SMEM),
                  tok(qw),
                  kblk(-1), kblk(0), kblk(1), kblk(2), ctx_spec,
                  kblk(-1), kblk(0), kblk(1), kblk(2), ctx_spec,
                  tok(CONV_DIM),
                  pl.BlockSpec((None, None, 16, CONV_DIM),
                               lambda bb, i: (bb, jnp.maximum(i * halo - 1, 0), 0, 0)),
                  pl.BlockSpec((None, None, 16, CONV_DIM),
                               lambda bb, i: (bb, jnp.minimum((i + 1) * halo, s // 16 - 1), 0, 0)),
                  tok(CONV_DIM),
                  pl.BlockSpec((3, CONV_DIM), lambda bb, i: (0, 0))],
        out_specs=[tok(CONV_DIM), tok(N_Q_HEADS * HEAD_DIM)],
        out_shape=[jax.ShapeDtypeStruct((b, s, CONV_DIM), BF16),
                   jax.ShapeDtypeStruct((b, s, N_Q_HEADS * HEAD_DIM), BF16)],
        compiler_params=_cparams(("arbitrary", "arbitrary")),
        name="even_conv_window_attn",
    )(sink, q, k, k, k, k, k, v, v, v, v, v, u, u16, u16, ab, conv_w)


def _pre_odd_kernel(x_ref, mod_ref, g_ref, w_ref, gq_ref, wuq_ref, gkv_ref, wuk_ref, wuv_ref,
                    wf_ref, cos_ref, sa_ref, sb_ref, q_ref, k_ref, v_ref, y_ref):
    mod = mod_ref[...]
    h = _norm_mod(x_ref[...], g_ref[...], mod[0:1], mod[1:2])
    p = jnp.dot(h.astype(BF16), w_ref[...], preferred_element_type=F32)
    cq = p[:, 0:Q_LORA]
    ckv = p[:, Q_LORA:Q_LORA + KV_LORA]
    kr = p[:, Q_LORA + KV_LORA:Q_LORA + KV_LORA + LANE]
    f = p[:, Q_LORA + KV_LORA + LANE:]
    cos, sa, sb = cos_ref[...], sa_ref[...], sb_ref[...]
    cos8 = jnp.concatenate([cos] * MLA_HEADS, axis=1)
    sa8 = jnp.concatenate([sa] * MLA_HEADS, axis=1)
    sb8 = jnp.concatenate([sb] * MLA_HEADS, axis=1)

    cqn = cq * lax.rsqrt(jnp.mean(cq * cq, axis=-1, keepdims=True) + EPS) * gq_ref[...]
    q = jnp.dot(cqn.astype(BF16), wuq_ref[...], preferred_element_type=F32)
    q = _rope(q, cos8, sa8, sb8, 8) * ((QK_NOPE + QK_ROPE) ** -0.5)
    q_ref[...] = q.astype(BF16)

    ckvn = (ckv * lax.rsqrt(jnp.mean(ckv * ckv, axis=-1, keepdims=True) + EPS) * gkv_ref[...]).astype(BF16)
    kn = jnp.dot(ckvn, wuk_ref[...], preferred_element_type=F32)
    krr = _rope(kr, cos, sa, sb, 8)
    k_ref[...] = (kn + jnp.concatenate([krr] * MLA_HEADS, axis=1)).astype(BF16)
    v_ref[...] = jnp.dot(ckvn, wuv_ref[...], preferred_element_type=F32).astype(BF16)
    y_ref[...] = jnp.dot(f.astype(BF16), wf_ref[...], preferred_element_type=F32).astype(BF16)


def _pre_odd(xa, mods, g_mix, w_in, gq, wuq, gkv, wuk, wuv, wf, tabs, li):
    b, s, d = xa.shape
    nt = s // TM
    nl = nt - 1
    hw = MLA_HEADS * LANE
    vw = MLA_HEADS * V_HEAD
    tab_spec = pl.BlockSpec((TM, LANE), lambda i, bb: (i, 0))
    tok = lambda w: pl.BlockSpec((None, TM, w), lambda i, bb: (bb, i, 0))
    full = lambda a: pl.BlockSpec(a.shape, lambda i, bb: (0,) * a.ndim)
    return pl.pallas_call(
        _pre_odd_kernel,
        grid=(nt, b),
        in_specs=[tok(d),
                  pl.BlockSpec((None, None, None, 6, d), lambda i, bb: (li, bb, i // nl, 0, 0)),
                  pl.BlockSpec((None, 1, d), lambda i, bb: (li, 0, 0)),
                  full(w_in), full(gq), full(wuq), full(gkv), full(wuk), full(wuv), full(wf),
                  tab_spec, tab_spec, tab_spec],
        out_specs=[tok(hw), tok(hw), tok(vw), tok(2 * FOURIER_DIM)],
        out_shape=[jax.ShapeDtypeStruct((b, s, hw), BF16),
                   jax.ShapeDtypeStruct((b, s, hw), BF16),
                   jax.ShapeDtypeStruct((b, s, vw), BF16),
                   jax.ShapeDtypeStruct((b, s, 2 * FOURIER_DIM), BF16)],
        compiler_params=_cparams(("arbitrary", "arbitrary")),
        name="odd_in_proj",
    )(xa, mods, g_mix, w_in, gq, wuq, gkv, wuk, wuv, wf, *tabs)


def _mla_kernel(q_ref, k_ref, v_ref, o_ref, *, nl, n_lat):
    i = pl.program_id(1)
    s_all = k_ref.shape[0]
    lane = lax.broadcasted_iota(jnp.int32, (TM, LANE), 1)

    def attend(k0, nk):
        outs = []
        for pair in range(MLA_HEADS // 2):
            vp = v_ref[k0:k0 + nk, pair * LANE:(pair + 1) * LANE]
            o_pair = []
            for h in (2 * pair, 2 * pair + 1):
                q = q_ref[:, h * LANE:(h + 1) * LANE]
                k = k_ref[k0:k0 + nk, h * LANE:(h + 1) * LANE]
                s = lax.dot_general(q, k, (((1,), (1,)), ((), ())), preferred_element_type=F32)
                m = jnp.max(s, axis=1, keepdims=True)
                p = jnp.exp(s - m)
                l = jnp.sum(p, axis=1, keepdims=True)
                o_pair.append(jnp.dot(p.astype(BF16), vp, preferred_element_type=F32) / l)
            outs.append(jnp.where(lane < V_HEAD, o_pair[0], o_pair[1]))
        o_ref[...] = jnp.concatenate(outs, axis=1).astype(BF16)

    @pl.when(i < nl)
    def _():
        attend(0, s_all)

    @pl.when(i >= nl)
    def _():
        attend(n_lat, s_all - n_lat)


def _mla(q, k, v, n_lat):
    b, s, hw = q.shape
    vw = v.shape[-1]
    nt = s // TM
    kern = functools.partial(_mla_kernel, nl=n_lat // TM, n_lat=n_lat)
    return pl.pallas_call(
        kern,
        grid=(b, nt),
        in_specs=[pl.BlockSpec((None, TM, hw), lambda bb, i: (bb, i, 0)),
                  pl.BlockSpec((None, s, hw), lambda bb, i: (bb, 0, 0)),
                  pl.BlockSpec((None, s, vw), lambda bb, i: (bb, 0, 0))],
        out_specs=pl.BlockSpec((None, TM, vw), lambda bb, i: (bb, i, 0)),
        out_shape=jax.ShapeDtypeStruct((b, s, vw), BF16),
        compiler_params=_cparams(("arbitrary", "arbitrary")),
        name="mla_attention",
    )(q, k, v)


def _dft_kernel(ac_ref, as_ref, cc_ref, cs_ref, yc_ref, ys_ref, ycc_ref, ysc_ref, o_ref, *, nl):
    i = pl.program_id(1)

    @pl.when(i < nl)
    def _():
        o_ref[...] = (jnp.dot(ac_ref[...], yc_ref[...], preferred_element_type=F32)
                      - jnp.dot(as_ref[...], ys_ref[...], preferred_element_type=F32)).astype(BF16)

    @pl.when(i >= nl)
    def _():
        o_ref[...] = (jnp.dot(cc_ref[...], ycc_ref[...], preferred_element_type=F32)
                      - jnp.dot(cs_ref[...], ysc_ref[...], preferred_element_type=F32)).astype(BF16)


def _fourier(ycs, dft_lat, dft_ctx, n_lat):
    b, s, _ = ycs.shape
    fd = FOURIER_DIM
    n_ctx = s - n_lat
    nl = n_lat // TM
    ac, as_ = dft_lat
    cc, cs = dft_ctx
    rows = pl.BlockSpec((TM, n_lat), lambda bb, i: (jnp.minimum(i, nl - 1), 0))
    small = pl.BlockSpec((n_ctx, n_ctx), lambda bb, i: (0, 0))
    return pl.pallas_call(
        functools.partial(_dft_kernel, nl=nl),
        grid=(b, s // TM),
        in_specs=[rows, rows, small, small,
                  pl.BlockSpec((None, n_lat, fd), lambda bb, i: (bb, 0, 0)),
                  pl.BlockSpec((None, n_lat, fd), lambda bb, i: (bb, 0, 1)),
                  pl.BlockSpec((None, n_ctx, fd), lambda bb, i: (bb, n_lat // n_ctx, 0)),
                  pl.BlockSpec((None, n_ctx, fd), lambda bb, i: (bb, n_lat // n_ctx, 1))],
        out_specs=pl.BlockSpec((None, TM, fd), lambda bb, i: (bb, i, 0)),
        out_shape=jax.ShapeDtypeStruct((b, s, fd), BF16),
        compiler_params=_cparams(("arbitrary", "arbitrary")),
        name="fourier_mix",
    )(ac, as_, cc, cs, ycs, ycs, ycs, ycs)


def _post_kernel(x_ref, ya_ref, yb_ref, mod_ref, wa_ref, wb_ref, g_ref, rw_ref, rb_ref,
                 xo_ref, hx_ref, rt_ref, cnt_ref, carry_ref):
    @pl.when((pl.program_id(0) == 0) & (pl.program_id(1) == 0))
    def _():
        carry_ref[...] = jnp.zeros_like(carry_ref)

    mod = mod_ref[...]
    y = (jnp.dot(ya_ref[...], wa_ref[...], preferred_element_type=F32)
         + jnp.dot(yb_ref[...], wb_ref[...], preferred_element_type=F32))
    x = x_ref[...] + mod[2:3] * y
    xo_ref[...] = x
    h2f = _norm_mod(x, g_ref[...], mod[3:4], mod[4:5])
    h2 = h2f.astype(BF16)
    logits = lax.dot_general(rw_ref[...], h2, (((1,), (1,)), ((), ())), preferred_element_type=F32)
    scores = jax.nn.sigmoid(logits)
    biased = scores + rb_ref[...]
    sc = [scores[e:e + 1, :] for e in range(N_EXPERTS)]
    bs = [biased[e:e + 1, :] for e in range(N_EXPERTS)]
    sel, gval = [], []
    for g in range(N_GROUPS):
        ids = range(g * EXPERTS_PER_GROUP, (g + 1) * EXPERTS_PER_GROUP)
        gsum = None
        for e in ids:
            rank = sum(((bs[j] > bs[e]) if j > e else (bs[j] >= bs[e])).astype(jnp.int32)
                       for j in ids if j != e)
            s_e = rank < 2
            sel.append(s_e)
            term = jnp.where(s_e, bs[e], 0.0)
            gsum = term if gsum is None else gsum + term
        gval.append(gsum)
    gsel = []
    for g in range(N_GROUPS):
        ok = None
        for j in range(N_GROUPS):
            if j == g:
                continue
            c = (gval[g] > gval[j]) if j < g else (gval[g] >= gval[j])
            ok = c if ok is None else (ok & c)
        gsel.append(ok)
    chosen = [sel[e] & gsel[e // EXPERTS_PER_GROUP] for e in range(N_EXPERTS)]
    wsum = sum(jnp.where(chosen[e], sc[e], 0.0) for e in range(N_EXPERTS))
    gates = [jnp.where(chosen[e], sc[e], 0.0) / wsum for e in range(N_EXPERTS)]
    bucket = jnp.zeros_like(wsum)
    w_lo = jnp.zeros_like(wsum)
    w_hi = jnp.zeros_like(wsum)
    for g in range(N_GROUPS):
        for pi, (ea, eb) in enumerate(PAIRS):
            ea, eb = g * EXPERTS_PER_GROUP + ea, g * EXPERTS_PER_GROUP + eb
            cond = chosen[ea] & chosen[eb]
            bucket = jnp.where(cond, float(g * len(PAIRS) + pi), bucket)
            w_lo = jnp.where(cond, gates[ea], w_lo)
            w_hi = jnp.where(cond, gates[eb], w_hi)
    kio = lax.broadcasted_iota(jnp.int32, (NB_PAD, TM), 0).astype(F32)
    onehot = (kio == bucket).astype(F32)
    tri = (lax.broadcasted_iota(jnp.int32, (TM, TM), 0)
           <= lax.broadcasted_iota(jnp.int32, (TM, TM), 1)).astype(BF16)
    prefix = jnp.dot(onehot.astype(BF16), tri, preferred_element_type=F32)
    carry = carry_ref[...]
    rank = jnp.sum(onehot * (prefix + carry[:, 0:1]), axis=0, keepdims=True) - 1.0
    carry = carry + jnp.sum(onehot, axis=1, keepdims=True)
    carry_ref[...] = carry
    cnt_ref[...] = carry
    zrow = jnp.zeros_like(wsum)
    rt_ref[...] = jnp.concatenate([bucket, rank] + [zrow] * 6, axis=0)
    ext_t = jnp.concatenate([w_lo, w_hi] + [zrow] * (LANE - 2), axis=0)
    hx_ref[...] = jnp.concatenate([h2f, ext_t.T], axis=1)


def _post(xa, ya, yb, mods, w_out_a, w_out_b, g_ffn, rw_t, rb, li):
    b, s, d = xa.shape
    nt = s // TM
    nl = nt - 1
    half = ya.shape[-1]
    tok = lambda w: pl.BlockSpec((None, TM, w), lambda bb, i: (bb, i, 0))
    return pl.pallas_call(
        _post_kernel,
        grid=(b, nt),
        in_specs=[tok(d), tok(half), tok(half),
                  pl.BlockSpec((None, None, None, 6, d), lambda bb, i: (li, bb, i // nl, 0, 0)),
                  pl.BlockSpec((half, d), lambda bb, i: (0, 0)),
                  pl.BlockSpec((half, d), lambda bb, i: (0, 0)),
                  pl.BlockSpec((None, 1, d), lambda bb, i: (li, 0, 0)),
                  pl.BlockSpec((N_EXPERTS, d), lambda bb, i: (0, 0)),
                  pl.BlockSpec((N_EXPERTS, 1), lambda bb, i: (0, 0))],
        out_specs=[tok(d), tok(d + LANE),
                   pl.BlockSpec((None, 8, TM), lambda bb, i: (bb, 0, i)),
                   pl.BlockSpec((NB_PAD, LANE), lambda bb, i: (0, 0))],
        out_shape=[jax.ShapeDtypeStruct((b, s, d), F32),
                   jax.ShapeDtypeStruct((b, s, d + LANE), F32),
                   jax.ShapeDtypeStruct((b, 8, s), F32),
                   jax.ShapeDtypeStruct((NB_PAD, LANE), F32)],
        scratch_shapes=[pltpu.VMEM((NB_PAD, LANE), F32)],
        input_output_aliases={0: 0},
        compiler_params=_cparams(("arbitrary", "arbitrary")),
        name="out_proj_router",
    )(xa, ya, yb, mods, w_out_a, w_out_b, g_ffn, rw_t, rb)


def _scatter_kernel(pos_ref, h_ref, init_ref, o_ref, sem):
    del init_ref

    def body(r, c):
        p = pos_ref[0, r]
        pltpu.make_async_copy(h_ref.at[pl.ds(r, 1), :], o_ref.at[pl.ds(p, 1), :], sem).start()
        return c

    lax.fori_loop(0, TM, body, 0, unroll=8)
    pltpu.make_async_copy(h_ref, o_ref.at[pl.ds(0, TM), :], sem).wait()


def _scatter_rows(hx, pos3, t_pad):
    t, w = hx.shape
    return pl.pallas_call(
        _scatter_kernel,
        grid=(t // TM,),
        in_specs=[pl.BlockSpec((None, 1, TM), lambda i: (i, 0, 0), memory_space=pltpu.SMEM),
                  pl.BlockSpec((TM, w), lambda i: (i, 0)),
                  pl.BlockSpec(memory_space=pl.ANY)],
        out_specs=pl.BlockSpec(memory_space=pl.ANY),
        out_shape=jax.ShapeDtypeStruct((t_pad, w), F32),
        scratch_shapes=[pltpu.SemaphoreType.DMA(())],
        input_output_aliases={2: 0},
        compiler_params=_cparams(("arbitrary",)),
        name="moe_scatter_rows",
    )(pos3, hx, jnp.zeros((t_pad, w), F32))


def _gmm_kernel(elo_ref, ehi_ref, valid_ref, hs_ref, wg0, wu0, wd0, wg1, wu1, wd1, o_ref):
    del elo_ref, ehi_ref
    j = pl.program_id(0)
    d = o_ref.shape[-1]

    @pl.when(valid_ref[j] == 1)
    def _():
        hx = hs_ref[...]
        h = hx[:, :d].astype(BF16)

        def ffn(wg, wu, wd):
            a = jnp.dot(h, wg[...], preferred_element_type=F32)
            u = jnp.dot(h, wu[...], preferred_element_type=F32)
            act = (jax.nn.silu(a) * u).astype(BF16)
            return jnp.dot(act, wd[...], preferred_element_type=F32)

        o_ref[...] = hx[:, d:d + 1] * ffn(wg0, wu0, wd0) + hx[:, d + 1:d + 2] * ffn(wg1, wu1, wd1)

    @pl.when(valid_ref[j] == 0)
    def _():
        o_ref[...] = jnp.zeros_like(o_ref)


def _gmm(hs, e_lo, e_hi, valid, wg, wu, wd, li):
    t_pad, w = hs.shape
    d = w - LANE
    dff = wg.shape[-1]
    up = lambda sel: pl.BlockSpec((None, None, d, dff), lambda j, lo, hi, v: (li, sel(lo, hi)[j], 0, 0))
    dn = lambda sel: pl.BlockSpec((None, None, dff, d), lambda j, lo, hi, v: (li, sel(lo, hi)[j], 0, 0))
    first = lambda lo, hi: lo
    second = lambda lo, hi: hi
    return pl.pallas_call(
        _gmm_kernel,
        grid_spec=pltpu.PrefetchScalarGridSpec(
            num_scalar_prefetch=3,
            grid=(t_pad // TME,),
            in_specs=[pl.BlockSpec((TME, w), lambda j, lo, hi, v: (j, 0)),
                      up(first), up(first), dn(first), up(second), up(second), dn(second)],
            out_specs=pl.BlockSpec((TME, d), lambda j, lo, hi, v: (j, 0))),
        out_shape=jax.ShapeDtypeStruct((t_pad, d), F32),
        compiler_params=_cparams(("arbitrary",)),
        name="moe_grouped_ffn",
    )(e_lo, e_hi, valid, hs, wg, wu, wd, wg, wu, wd)


def _route_tables(rt, cnt, t_pad):
    b, _, s = rt.shape
    nb = N_GROUPS * len(PAIRS)
    bucket = rt[:, 0, :].astype(jnp.int32)
    rank = rt[:, 1, :].astype(jnp.int32)
    counts = cnt[:nb, 0].astype(jnp.int32)
    padded = ((counts + TME - 1) // TME) * TME
    ends = jnp.cumsum(padded)
    starts = ends - padded
    pos = (jnp.take(starts, bucket) + rank).reshape(b * s // TM, 1, TM)
    n_tiles = t_pad // TME
    n_used = ends[-1] // TME
    tile = jnp.arange(n_tiles, dtype=jnp.int32)
    valid = (tile < n_used).astype(jnp.int32)
    tile_c = jnp.minimum(tile, n_used - 1)
    tb = jnp.minimum(jnp.searchsorted(ends, tile_c * TME, side="right"), nb - 1).astype(jnp.int32)
    grp, pi = tb // len(PAIRS), tb % len(PAIRS)
    pa = jnp.asarray([p[0] for p in PAIRS], jnp.int32)
    pb = jnp.asarray([p[1] for p in PAIRS], jnp.int32)
    e_lo = grp * EXPERTS_PER_GROUP + jnp.take(pa, pi)
    e_hi = grp * EXPERTS_PER_GROUP + jnp.take(pb, pi)
    return pos, e_lo, e_hi, valid


def _issue_gather(pos_ref, y_ref, buf_ref, sem, slot):
    def body(r, c):
        p = pos_ref[0, r]
        pltpu.make_async_copy(y_ref.at[pl.ds(p, 1), :], buf_ref.at[slot, pl.ds(r, 1), :], sem.at[slot]).start()
        return c

    lax.fori_loop(0, TM, body, 0, unroll=8)


def _gather_tile(pos_ref, posn_ref, y_ref, buf_ref, sem):
    ni = pl.num_programs(1)
    lin = pl.program_id(0) * ni + pl.program_id(1)
    slot = lin % 2

    @pl.when(lin == 0)
    def _():
        _issue_gather(pos_ref, y_ref, buf_ref, sem, slot)

    @pl.when(lin + 1 < pl.num_programs(0) * ni)
    def _():
        _issue_gather(posn_ref, y_ref, buf_ref, sem, 1 - slot)

    pltpu.make_async_copy(y_ref.at[pl.ds(0, TM), :], buf_ref.at[slot], sem.at[slot]).wait()
    return buf_ref[slot]


def _combine_kernel(pos_ref, posn_ref, x_ref, mod_ref, y_ref, xo_ref, buf_ref, sem):
    f = _gather_tile(pos_ref, posn_ref, y_ref, buf_ref, sem)
    mod = mod_ref[...]
    xo_ref[...] = x_ref[...] + mod[5:6] * f


def _final_kernel(pos_ref, posn_ref, x_ref, mod_ref, g_ref, y_ref, o_ref, buf_ref, sem):
    f = _gather_tile(pos_ref, posn_ref, y_ref, buf_ref, sem)
    mod = mod_ref[...]
    x = x_ref[...] + mod[5:6] * f
    var = jnp.mean(x * x, axis=-1, keepdims=True)
    o_ref[...] = (x * lax.rsqrt(var + EPS)) * g_ref[...]


def _pos_specs(nt, ni, nb):
    def cur(bb, i):
        return (bb * nt + i, 0, 0)

    def nxt(bb, i):
        wrap = i + 1 == ni
        b2 = jnp.minimum(jnp.where(wrap, bb + 1, bb), nb - 1)
        i2 = jnp.where(wrap, jnp.where(bb + 1 == nb, i, 0), i + 1)
        return (b2 * nt + i2, 0, 0)

    return [pl.BlockSpec((None, 1, TM), cur, memory_space=pltpu.SMEM),
            pl.BlockSpec((None, 1, TM), nxt, memory_space=pltpu.SMEM)]


def _combine(xa, ys, pos3, mods, li):
    b, s, d = xa.shape
    nt = s // TM
    nl = nt - 1
    tok = pl.BlockSpec((None, TM, d), lambda bb, i: (bb, i, 0))
    return pl.pallas_call(
        _combine_kernel,
        grid=(b, nt),
        in_specs=_pos_specs(nt, nt, b) + [
            tok,
            pl.BlockSpec((None, None, None, 6, d), lambda bb, i: (li, bb, i // nl, 0, 0)),
            pl.BlockSpec(memory_space=pl.ANY)],
        out_specs=tok,
        out_shape=jax.ShapeDtypeStruct((b, s, d), F32),
        scratch_shapes=[pltpu.VMEM((2, TM, d), F32), pltpu.SemaphoreType.DMA((2,))],
        input_output_aliases={2: 0},
        compiler_params=_cparams(("arbitrary", "arbitrary")),
        name="moe_gather_residual",
    )(pos3, pos3, xa, mods, ys)


def _final(xa, ys, pos3, mods, g_final, li, n_lat):
    b, s, d = xa.shape
    nt = s // TM
    nl = n_lat // TM
    tok = pl.BlockSpec((None, TM, d), lambda bb, i: (bb, i, 0))
    return pl.pallas_call(
        _final_kernel,
        grid=(b, nl),
        in_specs=_pos_specs(nt, nl, b) + [
            tok,
            pl.BlockSpec((None, None, None, 6, d), lambda bb, i: (li, bb, 0, 0, 0)),
            pl.BlockSpec((1, d), lambda bb, i: (0, 0)),
            pl.BlockSpec(memory_space=pl.ANY)],
        out_specs=tok,
        out_shape=jax.ShapeDtypeStruct((b, n_lat, d), F32),
        scratch_shapes=[pltpu.VMEM((2, TM, d), F32), pltpu.SemaphoreType.DMA((2,))],
        compiler_params=_cparams(("arbitrary", "arbitrary")),
        name="moe_gather_residual_final_norm",
    )(pos3, pos3, xa, mods, g_final, ys)


def _rope_tables(n_lat, n_ctx, rot_dim, lane_off):
    rows = n_lat // GRID_W
    r = jnp.repeat(jnp.arange(rows, dtype=F32), GRID_W)[:n_lat]
    c = jnp.tile(jnp.arange(GRID_W, dtype=F32), rows)[:n_lat]
    axis_dim = rot_dim // 2
    inv_freq = ROPE_BASE ** (-jnp.arange(0, axis_dim, 2, dtype=F32) / axis_dim)
    ang_r = r[:, None] * inv_freq
    ang_c = c[:, None] * inv_freq
    ang = jnp.concatenate([ang_r, ang_r, ang_c, ang_c], axis=-1)
    cos, sin = jnp.cos(ang), jnp.sin(ang)
    grp = (np.arange(rot_dim) // (rot_dim // 4)) % 2
    sin_a = jnp.where(grp == 0, -sin, 0.0)
    sin_b = jnp.where(grp == 1, sin, 0.0)

    def place(t, fill):
        reps = 2 if rot_dim == 64 else 1
        blk = jnp.full((n_lat, LANE), fill, F32)
        for k in range(reps):
            blk = blk.at[:, lane_off + k * rot_dim:lane_off + (k + 1) * rot_dim].set(t)
        ctx = jnp.full((n_ctx, LANE), fill, F32)
        return jnp.concatenate([blk, ctx], axis=0)

    return place(cos, 1.0), place(sin_a, 0.0), place(sin_b, 0.0)


def _dft_mats(n, scale):
    k = jnp.arange(n, dtype=jnp.int32)
    m = (k[:, None] * k[None, :]) % n
    ang = m.astype(F32) * (2.0 * np.pi / n)
    return (jnp.cos(ang) * scale).astype(BF16), (jnp.sin(ang) * scale).astype(BF16)


def _even_w_in(w):
    d = w.shape[0]
    c3 = 3 * CONV_DIM
    qcols = []
    per_kv = N_Q_HEADS // N_KV_HEADS
    for h in range(N_Q_HEADS):
        wq = w[:, c3 + h * HEAD_DIM:c3 + (h + 1) * HEAD_DIM]
        z = jnp.zeros((d, HEAD_DIM), w.dtype)
        qcols.append(jnp.concatenate([wq, z] if h // per_kv == 0 else [z, wq], axis=1))
    return jnp.concatenate([w[:, :c3]] + qcols + [w[:, c3 + N_Q_HEADS * HEAD_DIM:]], axis=1).astype(BF16)


def _odd_w_in(w):
    d = w.shape[0]
    o = Q_LORA + KV_LORA
    kr = jnp.concatenate([jnp.zeros((d, QK_NOPE), w.dtype), w[:, o:o + QK_ROPE],
                          jnp.zeros((d, LANE - QK_NOPE - QK_ROPE), w.dtype)], axis=1)
    return jnp.concatenate([w[:, :o], kr, w[:, o + QK_ROPE:]], axis=1).astype(BF16)


def _pad_heads(w, per_head, keep_from, keep, heads):
    rows = w.shape[0]
    w3 = w.reshape(rows, heads, per_head)[:, :, keep_from:keep_from + keep]
    w3 = jnp.pad(w3, ((0, 0), (0, 0), (0, LANE - keep)))
    return w3.reshape(rows, heads * LANE).astype(BF16)


def _channel_dft():
    gd = FOURIER_DIM // FOURIER_GROUPS
    k = np.arange(gd)
    ang = 2.0 * np.pi * ((k[:, None] * k[None, :]) % gd) / gd
    eye = np.eye(FOURIER_GROUPS)
    wc = np.kron(eye, np.cos(ang))
    ws = np.kron(eye, np.sin(ang))
    return jnp.asarray(np.concatenate([wc, ws], axis=1), BF16)


def kernel(x, c, ctx, c_ctx, ada_w, ada_b, norm_mix_g, norm_ffn_g, ev_w_in, ev_conv_w, ev_sink, ev_w_out,
           od_w_in, od_q_norm_g, od_w_uq, od_kv_norm_g, od_w_ukv, od_w_out, router_w, router_bias,
           ex_w_gate, ex_w_up, ex_w_down, final_norm_g):
    b, n, d = x.shape
    l = ctx.shape[1]
    s = n + l
    depth = ada_w.shape[0]
    assert l == TM and n % 512 == 0

    rows = 8 * ((b + 1 + 7) // 8)
    cvec = jnp.concatenate([c, c_ctx[None, :], jnp.zeros((rows - b - 1, d), F32)], axis=0)
    mod_all = _modulation(cvec, ada_w, ada_b).reshape(depth, rows, 6, d)
    mods = jnp.stack([mod_all[:, :b], jnp.broadcast_to(mod_all[:, b:b + 1], (depth, b, 6, d))], axis=2)

    tabs_even = _rope_tables(n, l, HEAD_DIM, 0)
    tabs_odd = _rope_tables(n, l, QK_ROPE, QK_NOPE)
    gd = FOURIER_DIM // FOURIER_GROUPS
    dft_lat = _dft_mats(n, (n * gd) ** -0.5)
    dft_ctx = _dft_mats(l, (l * gd) ** -0.5)
    wf = _channel_dft()

    g_mix = norm_mix_g.reshape(depth, 1, d)
    g_ffn = norm_ffn_g.reshape(depth, 1, d)
    rw_t = router_w.T.astype(BF16)
    rb = router_bias.reshape(N_EXPERTS, 1).astype(F32)
    wg, wu, wd = ex_w_gate.astype(BF16), ex_w_up.astype(BF16), ex_w_down.astype(BF16)

    xa = jnp.concatenate([x, ctx], axis=1)
    t_pad = (-(-(b * s) // TME) + N_GROUPS * len(PAIRS)) * TME
    out = None
    for li in range(depth):
        j = li // 2
        if li % 2 == 0:
            u, ab, q, k, v = _pre_even(xa, mods, g_mix, _even_w_in(ev_w_in[j]), tabs_even, li)
            ya, yb = _even_mix(u, ab, q, k, v, ev_conv_w[j], ev_sink[j], n)
            w_out = ev_w_out[j].astype(BF16)
        else:
            q, k, v, ycs = _pre_odd(
                xa, mods, g_mix, _odd_w_in(od_w_in[j]),
                od_q_norm_g[j].reshape(1, Q_LORA),
                _pad_heads(od_w_uq[j], QK_NOPE + QK_ROPE, 0, QK_NOPE + QK_ROPE, MLA_HEADS),
                od_kv_norm_g[j].reshape(1, KV_LORA),
                _pad_heads(od_w_ukv[j], QK_NOPE + V_HEAD, 0, QK_NOPE, MLA_HEADS),
                od_w_ukv[j].reshape(KV_LORA, MLA_HEADS, QK_NOPE + V_HEAD)[:, :, QK_NOPE:]
                .reshape(KV_LORA, MLA_HEADS * V_HEAD).astype(BF16),
                wf, tabs_odd, li)
            ya = _mla(q, k, v, n)
            yb = _fourier(ycs, dft_lat, dft_ctx, n)
            w_out = od_w_out[j].astype(BF16)
        half = w_out.shape[0] // 2
        xa, hx, rt, cnt = _post(xa, ya, yb, mods, w_out[:half], w_out[half:], g_ffn, rw_t, rb, li)
        pos3, e_lo, e_hi, valid = _route_tables(rt, cnt, t_pad)
        hs = _scatter_rows(hx.reshape(b * s, d + LANE), pos3, t_pad)
        ys = _gmm(hs, e_lo, e_hi, valid, wg, wu, wd, li)
        if li == depth - 1:
            out = _final(xa, ys, pos3, mods, final_norm_g.reshape(1, d), li, n)
        else:
            xa = _combine(xa, ys, pos3, mods, li)
    return out
```

```python
import functools

import jax
import jax.numpy as jnp
import numpy as np
from jax import lax
from jax.experimental import pallas as pl
from jax.experimental.pallas import tpu as pltpu

F32 = jnp.float32
BF16 = jnp.bfloat16

EPS = 1e-6
ROPE_BASE = 10000.0
GRID_W = 64
NEG_INF = -1e30
LOG2E = 1.4426950408889634
LANE = 128
TM = 256
WIN = 128
N_Q_HEADS = 8
N_KV_HEADS = 2
HEAD_DIM = 64
CONV_DIM = 512
MLA_HEADS = 8
Q_LORA = 256
KV_LORA = 128
QK_NOPE = 64
QK_ROPE = 32
V_HEAD = 64
FOURIER_DIM = 512
FOURIER_GROUPS = 4
N_EXPERTS = 16
N_GROUPS = 4
EXPERTS_PER_GROUP = 4
PAIRS = ((0, 1), (0, 2), (0, 3), (1, 2), (1, 3), (2, 3))
NB_PAD = 32
TME = 256
VMEM_LIMIT = 56 * 1024 * 1024


def _cparams(sem):
    return pltpu.CompilerParams(dimension_semantics=sem, vmem_limit_bytes=VMEM_LIMIT)


def _norm_mod(x, g, shift, scale):
    var = jnp.mean(x * x, axis=-1, keepdims=True)
    return (x * lax.rsqrt(var + EPS)) * g * (1.0 + scale) + shift


def _rope(x, cos, sin_a, sin_b, shift):
    w = x.shape[-1]
    return x * cos + pltpu.roll(x, w - shift, 1) * sin_a + pltpu.roll(x, shift, 1) * sin_b


def _mod_kernel(c_ref, w_ref, b_ref, o_ref):
    c = c_ref[...]
    o_ref[...] = jnp.dot(jax.nn.silu(c), w_ref[...], preferred_element_type=F32,
                         precision=lax.Precision.HIGHEST) + b_ref[...]


def _modulation(cvec, ada_w, ada_b):
    depth, d, d6 = ada_w.shape
    tn = d6 // 4
    rows = cvec.shape[0]
    return pl.pallas_call(
        _mod_kernel,
        grid=(depth, d6 // tn),
        in_specs=[pl.BlockSpec((rows, d), lambda l, j: (0, 0)),
                  pl.BlockSpec((None, d, tn), lambda l, j: (l, 0, j)),
                  pl.BlockSpec((None, 1, tn), lambda l, j: (l, 0, j))],
        out_specs=pl.BlockSpec((None, rows, tn), lambda l, j: (l, 0, j)),
        out_shape=jax.ShapeDtypeStruct((depth, rows, d6), F32),
        compiler_params=_cparams(("arbitrary", "arbitrary")),
        name="adaln_modulation",
    )(cvec, ada_w, ada_b.reshape(depth, 1, d6))


def _pre_even_kernel(x_ref, mod_ref, g_ref, w_ref, cos_ref, sa_ref, sb_ref,
                     u_ref, ab_ref, q_ref, k_ref, v_ref):
    mod = mod_ref[...]
    h = _norm_mod(x_ref[...], g_ref[...], mod[0:1], mod[1:2])
    p = jnp.dot(h.astype(BF16), w_ref[...], preferred_element_type=F32)
    c = CONV_DIM
    ax, ab, ac = p[:, 0:c], p[:, c:2 * c], p[:, 2 * c:3 * c]
    qw = N_Q_HEADS * LANE
    q = p[:, 3 * c:3 * c + qw]
    k = p[:, 3 * c + qw:3 * c + qw + LANE]
    v = p[:, 3 * c + qw + LANE:3 * c + qw + 2 * LANE]
    cos, sa, sb = cos_ref[...], sa_ref[...], sb_ref[...]
    cos8 = jnp.concatenate([cos] * N_Q_HEADS, axis=1)
    sa8 = jnp.concatenate([sa] * N_Q_HEADS, axis=1)
    sb8 = jnp.concatenate([sb] * N_Q_HEADS, axis=1)
    u_ref[...] = (ac * ax).astype(BF16)
    ab_ref[...] = ab.astype(BF16)
    q_ref[...] = (_rope(q, cos8, sa8, sb8, 16) * (HEAD_DIM ** -0.5 * LOG2E)).astype(BF16)
    k_ref[...] = _rope(k, cos, sa, sb, 16).astype(BF16)
    v_ref[...] = v.astype(BF16)


def _pre_even(xa, mods, g_mix, w_in, tabs, li):
    b, s, d = xa.shape
    nt = s // TM
    nl = nt - 1
    wcols = w_in.shape[1]
    qw = N_Q_HEADS * LANE
    tab_spec = pl.BlockSpec((TM, LANE), lambda i, bb: (i, 0))
    tok = lambda w: pl.BlockSpec((None, TM, w), lambda i, bb: (bb, i, 0))
    return pl.pallas_call(
        _pre_even_kernel,
        grid=(nt, b),
        in_specs=[tok(d),
                  pl.BlockSpec((None, None, None, 6, d), lambda i, bb: (li, bb, i // nl, 0, 0)),
                  pl.BlockSpec((None, 1, d), lambda i, bb: (li, 0, 0)),
                  pl.BlockSpec((d, wcols), lambda i, bb: (0, 0)),
                  tab_spec, tab_spec, tab_spec],
        out_specs=[tok(CONV_DIM), tok(CONV_DIM), tok(qw), tok(LANE), tok(LANE)],
        out_shape=[jax.ShapeDtypeStruct((b, s, CONV_DIM), BF16),
                   jax.ShapeDtypeStruct((b, s, CONV_DIM), BF16),
                   jax.ShapeDtypeStruct((b, s, qw), BF16),
                   jax.ShapeDtypeStruct((b, s, LANE), BF16),
                   jax.ShapeDtypeStruct((b, s, LANE), BF16)],
        compiler_params=_cparams(("arbitrary", "arbitrary")),
        name="even_in_proj",
    )(xa, mods, g_mix, w_in, *tabs)


def _even_mix_kernel(sink_ref, q_ref, k0_ref, k1_ref, k2_ref, k3_ref, kc_ref,
                     v0_ref, v1_ref, v2_ref, v3_ref, vc_ref,
                     u_ref, up_ref, un_ref, ab_ref, cw_ref, yc_ref, ya_ref, *, nl, n_lat):
    i = pl.program_id(1)
    nt = pl.num_programs(1)
    is_lat = i < nl
    u = u_ref[...].astype(F32)
    row = lax.broadcasted_iota(jnp.int32, u.shape, 0)
    first = jnp.logical_or(i == 0, i == nl)
    last = jnp.logical_or(i == nl - 1, i == nt - 1)
    prev_row = jnp.where(first, 0.0, up_ref[15:16, :].astype(F32))
    next_row = jnp.where(last, 0.0, un_ref[0:1, :].astype(F32))
    u_prev = jnp.where(row == 0, prev_row, pltpu.roll(u, 1, 0))
    u_next = jnp.where(row == TM - 1, next_row, pltpu.roll(u, TM - 1, 0))
    cw = cw_ref[...]
    conv = u_prev * cw[0:1] + u * cw[1:2] + u_next * cw[2:3]
    yc_ref[...] = (ab_ref[...].astype(F32) * conv).astype(BF16)

    k_all = jnp.concatenate([k0_ref[...], k1_ref[...], k2_ref[...], k3_ref[...], kc_ref[...]], axis=0)
    v_all = jnp.concatenate([v0_ref[...], v1_ref[...], v2_ref[...], v3_ref[...], vc_ref[...]], axis=0)
    nk = k_all.shape[0]
    nwin = 4 * WIN
    qi = lax.broadcasted_iota(jnp.int32, (TM, nk), 0)
    kj = lax.broadcasted_iota(jnp.int32, (TM, nk), 1)
    kpos = i * TM - WIN + kj
    valid_win = (jnp.abs(kj - WIN - qi) <= WIN) & (kpos >= 0) & (kpos < n_lat) & is_lat
    bias = jnp.where((kj >= nwin) | valid_win, 0.0, NEG_INF).astype(F32)
    lane = lax.broadcasted_iota(jnp.int32, (TM, LANE), 1)
    outs = []
    for pair in range(N_Q_HEADS // 2):
        kvh = (2 * pair) // (N_Q_HEADS // N_KV_HEADS)
        o_pair = []
        for h in (2 * pair, 2 * pair + 1):
            q = q_ref[:, h * LANE:(h + 1) * LANE]
            s = lax.dot_general(q, k_all, (((1,), (1,)), ((), ())), preferred_element_type=F32) + bias
            sink = sink_ref[h] * LOG2E
            m = jnp.maximum(jnp.max(s, axis=1, keepdims=True), sink)
            p = jnp.exp2(s - m)
            l = jnp.sum(p, axis=1, keepdims=True) + jnp.exp2(sink - m)
            o = jnp.dot(p.astype(BF16), v_all, preferred_element_type=F32) / l
            if kvh == 1:
                o = pltpu.roll(o, HEAD_DIM, 1)
            o_pair.append(o)
        outs.append(jnp.where(lane < HEAD_DIM, o_pair[0], pltpu.roll(o_pair[1], HEAD_DIM, 1)))
    ya_ref[...] = jnp.concatenate(outs, axis=1).astype(BF16)


def _even_mix(u, ab, q, k, v, conv_w, sink, n_lat):
    b, s, _ = u.shape
    nt = s // TM
    nl = n_lat // TM
    nkb = s // WIN
    rpt = TM // WIN
    qw = q.shape[-1]

    def kblk(off):
        return pl.BlockSpec((None, WIN, LANE),
                            lambda bb, i: (bb, jnp.clip(i * rpt + off, 0, nkb - 1), 0))

    ctx_spec = pl.BlockSpec((None, s - n_lat, LANE), lambda bb, i: (bb, n_lat // (s - n_lat), 0))
    halo = TM // 16
    u16 = u.reshape(b, s // 16, 16, CONV_DIM)
    kern = functools.partial(_even_mix_kernel, nl=nl, n_lat=n_lat)
    tok = lambda w: pl.BlockSpec((None, TM, w), lambda bb, i: (bb, i, 0))
    return pl.pallas_call(
        kern,
        grid=(b, nt),
        in_specs=[pl.BlockSpec(memory_space=pltpu.SMEM),
                  tok(qw),
                  kblk(-1), kblk(0), kblk(1), kblk(2), ctx_spec,
                  kblk(-1), kblk(0), kblk(1), kblk(2), ctx_spec,
                  tok(CONV_DIM),
                  pl.BlockSpec((None, None, 16, CONV_DIM),
                               lambda bb, i: (bb, jnp.maximum(i * halo - 1, 0), 0, 0)),
                  pl.BlockSpec((None, None, 16, CONV_DIM),
                               lambda bb, i: (bb, jnp.minimum((i + 1) * halo, s // 16 - 1), 0, 0)),
                  tok(CONV_DIM),
                  pl.BlockSpec((3, CONV_DIM), lambda bb, i: (0, 0))],
        out_specs=[tok(CONV_DIM), tok(N_Q_HEADS * HEAD_DIM)],
        out_shape=[jax.ShapeDtypeStruct((b, s, CONV_DIM), BF16),
                   jax.ShapeDtypeStruct((b, s, N_Q_HEADS * HEAD_DIM), BF16)],
        compiler_params=_cparams(("arbitrary", "arbitrary")),
        name="even_conv_window_attn",
    )(sink, q, k, k, k, k, k, v, v, v, v, v, u, u16, u16, ab, conv_w)


def _pre_odd_kernel(x_ref, mod_ref, g_ref, w_ref, gq_ref, wuq_ref, gkv_ref, wuk_ref, wuv_ref,
                    wf_ref, cos_ref, sa_ref, sb_ref, q_ref, k_ref, v_ref, y_ref):
    mod = mod_ref[...]
    h = _norm_mod(x_ref[...], g_ref[...], mod[0:1], mod[1:2])
    p = jnp.dot(h.astype(BF16), w_ref[...], preferred_element_type=F32)
    cq = p[:, 0:Q_LORA]
    ckv = p[:, Q_LORA:Q_LORA + KV_LORA]
    kr = p[:, Q_LORA + KV_LORA:Q_LORA + KV_LORA + LANE]
    f = p[:, Q_LORA + KV_LORA + LANE:]
    cos, sa, sb = cos_ref[...], sa_ref[...], sb_ref[...]
    cos8 = jnp.concatenate([cos] * MLA_HEADS, axis=1)
    sa8 = jnp.concatenate([sa] * MLA_HEADS, axis=1)
    sb8 = jnp.concatenate([sb] * MLA_HEADS, axis=1)

    cqn = cq * lax.rsqrt(jnp.mean(cq * cq, axis=-1, keepdims=True) + EPS) * gq_ref[...]
    q = jnp.dot(cqn.astype(BF16), wuq_ref[...], preferred_element_type=F32)
    q = _rope(q, cos8, sa8, sb8, 8) * ((QK_NOPE + QK_ROPE) ** -0.5 * LOG2E)
    q_ref[...] = q.astype(BF16)

    ckvn = (ckv * lax.rsqrt(jnp.mean(ckv * ckv, axis=-1, keepdims=True) + EPS) * gkv_ref[...]).astype(BF16)
    kn = jnp.dot(ckvn, wuk_ref[...], preferred_element_type=F32)
    krr = _rope(kr, cos, sa, sb, 8)
    k_ref[...] = (kn + jnp.concatenate([krr] * MLA_HEADS, axis=1)).astype(BF16)
    v = jnp.dot(ckvn, wuv_ref[...], preferred_element_type=F32)
    one_lane = lax.broadcasted_iota(jnp.int32, v.shape, 1) % LANE == V_HEAD
    v_ref[...] = jnp.where(one_lane, 1.0, v).astype(BF16)
    y_ref[...] = jnp.dot(f.astype(BF16), wf_ref[...], preferred_element_type=F32).astype(BF16)


def _pre_odd(xa, mods, g_mix, w_in, gq, wuq, gkv, wuk, wuv, wf, tabs, li):
    b, s, d = xa.shape
    nt = s // TM
    nl = nt - 1
    hw = MLA_HEADS * LANE
    vw = hw
    tab_spec = pl.BlockSpec((TM, LANE), lambda i, bb: (i, 0))
    tok = lambda w: pl.BlockSpec((None, TM, w), lambda i, bb: (bb, i, 0))
    full = lambda a: pl.BlockSpec(a.shape, lambda i, bb: (0,) * a.ndim)
    return pl.pallas_call(
        _pre_odd_kernel,
        grid=(nt, b),
        in_specs=[tok(d),
                  pl.BlockSpec((None, None, None, 6, d), lambda i, bb: (li, bb, i // nl, 0, 0)),
                  pl.BlockSpec((None, 1, d), lambda i, bb: (li, 0, 0)),
                  full(w_in), full(gq), full(wuq), full(gkv), full(wuk), full(wuv), full(wf),
                  tab_spec, tab_spec, tab_spec],
        out_specs=[tok(hw), tok(hw), tok(vw), tok(2 * FOURIER_DIM)],
        out_shape=[jax.ShapeDtypeStruct((b, s, hw), BF16),
                   jax.ShapeDtypeStruct((b, s, hw), BF16),
                   jax.ShapeDtypeStruct((b, s, vw), BF16),
                   jax.ShapeDtypeStruct((b, s, 2 * FOURIER_DIM), BF16)],
        compiler_params=_cparams(("arbitrary", "arbitrary")),
        name="odd_in_proj",
    )(xa, mods, g_mix, w_in, gq, wuq, gkv, wuk, wuv, wf, *tabs)


def _mla_kernel(q_ref, k_ref, v_ref, o_ref, *, nl, n_lat):
    i = pl.program_id(1)
    s_all = k_ref.shape[0]
    lane = lax.broadcasted_iota(jnp.int32, (TM, LANE), 1)

    def attend(k0, nk):
        outs = []
        for pair in range(MLA_HEADS // 2):
            o_pair = []
            for h in (2 * pair, 2 * pair + 1):
                q = q_ref[:, h * LANE:(h + 1) * LANE]
                k = k_ref[k0:k0 + nk, h * LANE:(h + 1) * LANE]
                v = v_ref[k0:k0 + nk, h * LANE:(h + 1) * LANE]
                s = lax.dot_general(q, k, (((1,), (1,)), ((), ())), preferred_element_type=F32)
                m = jnp.max(s, axis=1, keepdims=True)
                p = jnp.exp2(s - m)
                o = jnp.dot(p.astype(BF16), v, preferred_element_type=F32)
                o_pair.append(o / o[:, V_HEAD:V_HEAD + 1])
            outs.append(jnp.where(lane < V_HEAD, o_pair[0], pltpu.roll(o_pair[1], V_HEAD, 1)))
        o_ref[...] = jnp.concatenate(outs, axis=1).astype(BF16)

    @pl.when(i < nl)
    def _():
        attend(0, s_all)

    @pl.when(i >= nl)
    def _():
        attend(n_lat, s_all - n_lat)


def _mla(q, k, v, n_lat):
    b, s, hw = q.shape
    vw = MLA_HEADS * V_HEAD
    nt = s // TM
    kern = functools.partial(_mla_kernel, nl=n_lat // TM, n_lat=n_lat)
    return pl.pallas_call(
        kern,
        grid=(b, nt),
        in_specs=[pl.BlockSpec((None, TM, hw), lambda bb, i: (bb, i, 0)),
                  pl.BlockSpec((None, s, hw), lambda bb, i: (bb, 0, 0)),
                  pl.BlockSpec((None, s, hw), lambda bb, i: (bb, 0, 0))],
        out_specs=pl.BlockSpec((None, TM, vw), lambda bb, i: (bb, i, 0)),
        out_shape=jax.ShapeDtypeStruct((b, s, vw), BF16),
        compiler_params=_cparams(("arbitrary", "arbitrary")),
        name="mla_attention",
    )(q, k, v)


def _dft_kernel(ac_ref, as_ref, cc_ref, cs_ref, yc_ref, ys_ref, ycc_ref, ysc_ref, o_ref, *, nl):
    i = pl.program_id(1)

    @pl.when(i < nl)
    def _():
        o_ref[...] = (jnp.dot(ac_ref[...], yc_ref[...], preferred_element_type=F32)
                      - jnp.dot(as_ref[...], ys_ref[...], preferred_element_type=F32)).astype(BF16)

    @pl.when(i >= nl)
    def _():
        o_ref[...] = (jnp.dot(cc_ref[...], ycc_ref[...], preferred_element_type=F32)
                      - jnp.dot(cs_ref[...], ysc_ref[...], preferred_element_type=F32)).astype(BF16)


def _fourier(ycs, dft_lat, dft_ctx, n_lat):
    b, s, _ = ycs.shape
    fd = FOURIER_DIM
    n_ctx = s - n_lat
    nl = n_lat // TM
    ac, as_ = dft_lat
    cc, cs = dft_ctx
    rows = pl.BlockSpec((TM, n_lat), lambda bb, i: (jnp.minimum(i, nl - 1), 0))
    small = pl.BlockSpec((n_ctx, n_ctx), lambda bb, i: (0, 0))
    return pl.pallas_call(
        functools.partial(_dft_kernel, nl=nl),
        grid=(b, s // TM),
        in_specs=[rows, rows, small, small,
                  pl.BlockSpec((None, n_lat, fd), lambda bb, i: (bb, 0, 0)),
                  pl.BlockSpec((None, n_lat, fd), lambda bb, i: (bb, 0, 1)),
                  pl.BlockSpec((None, n_ctx, fd), lambda bb, i: (bb, n_lat // n_ctx, 0)),
                  pl.BlockSpec((None, n_ctx, fd), lambda bb, i: (bb, n_lat // n_ctx, 1))],
        out_specs=pl.BlockSpec((None, TM, fd), lambda bb, i: (bb, i, 0)),
        out_shape=jax.ShapeDtypeStruct((b, s, fd), BF16),
        compiler_params=_cparams(("arbitrary", "arbitrary")),
        name="fourier_mix",
    )(ac, as_, cc, cs, ycs, ycs, ycs, ycs)


def _post_kernel(x_ref, ya_ref, yb_ref, mod_ref, wa_ref, wb_ref, g_ref, rw_ref, rb_ref,
                 xo_ref, hx_ref, rt_ref, cnt_ref, carry_ref):
    @pl.when((pl.program_id(0) == 0) & (pl.program_id(1) == 0))
    def _():
        carry_ref[...] = jnp.zeros_like(carry_ref)

    mod = mod_ref[...]
    y = (jnp.dot(ya_ref[...], wa_ref[...], preferred_element_type=F32)
         + jnp.dot(yb_ref[...], wb_ref[...], preferred_element_type=F32))
    x = x_ref[...] + mod[2:3] * y
    xo_ref[...] = x
    h2f = _norm_mod(x, g_ref[...], mod[3:4], mod[4:5])
    h2 = h2f.astype(BF16)
    logits = lax.dot_general(rw_ref[...], h2, (((1,), (1,)), ((), ())), preferred_element_type=F32)
    scores = jax.nn.sigmoid(logits)
    biased = scores + rb_ref[...]
    sc = [scores[e:e + 1, :] for e in range(N_EXPERTS)]
    bs = [biased[e:e + 1, :] for e in range(N_EXPERTS)]
    sel, gval = [], []
    for g in range(N_GROUPS):
        ids = range(g * EXPERTS_PER_GROUP, (g + 1) * EXPERTS_PER_GROUP)
        gsum = None
        for e in ids:
            rank = sum(((bs[j] > bs[e]) if j > e else (bs[j] >= bs[e])).astype(jnp.int32)
                       for j in ids if j != e)
            s_e = rank < 2
            sel.append(s_e)
            term = jnp.where(s_e, bs[e], 0.0)
            gsum = term if gsum is None else gsum + term
        gval.append(gsum)
    gsel = []
    for g in range(N_GROUPS):
        ok = None
        for j in range(N_GROUPS):
            if j == g:
                continue
            c = (gval[g] > gval[j]) if j < g else (gval[g] >= gval[j])
            ok = c if ok is None else (ok & c)
        gsel.append(ok)
    chosen = [sel[e] & gsel[e // EXPERTS_PER_GROUP] for e in range(N_EXPERTS)]
    wsum = sum(jnp.where(chosen[e], sc[e], 0.0) for e in range(N_EXPERTS))
    gates = [jnp.where(chosen[e], sc[e], 0.0) / wsum for e in range(N_EXPERTS)]
    bucket = jnp.zeros_like(wsum)
    w_lo = jnp.zeros_like(wsum)
    w_hi = jnp.zeros_like(wsum)
    for g in range(N_GROUPS):
        for pi, (ea, eb) in enumerate(PAIRS):
            ea, eb = g * EXPERTS_PER_GROUP + ea, g * EXPERTS_PER_GROUP + eb
            cond = chosen[ea] & chosen[eb]
            bucket = jnp.where(cond, float(g * len(PAIRS) + pi), bucket)
            w_lo = jnp.where(cond, gates[ea], w_lo)
            w_hi = jnp.where(cond, gates[eb], w_hi)
    kio = lax.broadcasted_iota(jnp.int32, (NB_PAD, TM), 0).astype(F32)
    onehot = (kio == bucket).astype(F32)
    tri = (lax.broadcasted_iota(jnp.int32, (TM, TM), 0)
           <= lax.broadcasted_iota(jnp.int32, (TM, TM), 1)).astype(BF16)
    prefix = jnp.dot(onehot.astype(BF16), tri, preferred_element_type=F32)
    carry = carry_ref[...]
    rank = jnp.sum(onehot * (prefix + carry[:, 0:1]), axis=0, keepdims=True) - 1.0
    carry = carry + jnp.sum(onehot, axis=1, keepdims=True)
    carry_ref[...] = carry
    cnt_ref[...] = carry
    zrow = jnp.zeros_like(wsum)
    rt_ref[...] = jnp.concatenate([bucket, rank] + [zrow] * 6, axis=0)
    ext_t = jnp.concatenate([w_lo, w_hi] + [zrow] * (LANE - 2), axis=0)
    hx_ref[...] = jnp.concatenate([h2f, ext_t.T], axis=1)


def _post(xa, ya, yb, mods, w_out_a, w_out_b, g_ffn, rw_t, rb, li):
    b, s, d = xa.shape
    nt = s // TM
    nl = nt - 1
    half = ya.shape[-1]
    tok = lambda w: pl.BlockSpec((None, TM, w), lambda bb, i: (bb, i, 0))
    return pl.pallas_call(
        _post_kernel,
        grid=(b, nt),
        in_specs=[tok(d), tok(half), tok(half),
                  pl.BlockSpec((None, None, None, 6, d), lambda bb, i: (li, bb, i // nl, 0, 0)),
                  pl.BlockSpec((half, d), lambda bb, i: (0, 0)),
                  pl.BlockSpec((half, d), lambda bb, i: (0, 0)),
                  pl.BlockSpec((None, 1, d), lambda bb, i: (li, 0, 0)),
                  pl.BlockSpec((N_EXPERTS, d), lambda bb, i: (0, 0)),
                  pl.BlockSpec((N_EXPERTS, 1), lambda bb, i: (0, 0))],
        out_specs=[tok(d), tok(d + LANE),
                   pl.BlockSpec((None, 8, TM), lambda bb, i: (bb, 0, i)),
                   pl.BlockSpec((NB_PAD, LANE), lambda bb, i: (0, 0))],
        out_shape=[jax.ShapeDtypeStruct((b, s, d), F32),
                   jax.ShapeDtypeStruct((b, s, d + LANE), F32),
                   jax.ShapeDtypeStruct((b, 8, s), F32),
                   jax.ShapeDtypeStruct((NB_PAD, LANE), F32)],
        scratch_shapes=[pltpu.VMEM((NB_PAD, LANE), F32)],
        input_output_aliases={0: 0},
        compiler_params=_cparams(("arbitrary", "arbitrary")),
        name="out_proj_router",
    )(xa, ya, yb, mods, w_out_a, w_out_b, g_ffn, rw_t, rb)


def _scatter_kernel(pos_ref, h_ref, init_ref, o_ref, sem):
    del init_ref

    def body(r, c):
        p = pos_ref[0, r]
        pltpu.make_async_copy(h_ref.at[pl.ds(r, 1), :], o_ref.at[pl.ds(p, 1), :], sem).start()
        return c

    lax.fori_loop(0, TM, body, 0, unroll=8)
    pltpu.make_async_copy(h_ref, o_ref.at[pl.ds(0, TM), :], sem).wait()


def _scatter_rows(hx, pos3, init):
    t, w = hx.shape
    t_pad = init.shape[0]
    return pl.pallas_call(
        _scatter_kernel,
        grid=(t // TM,),
        in_specs=[pl.BlockSpec((None, 1, TM), lambda i: (i, 0, 0), memory_space=pltpu.SMEM),
                  pl.BlockSpec((TM, w), lambda i: (i, 0)),
                  pl.BlockSpec(memory_space=pl.ANY)],
        out_specs=pl.BlockSpec(memory_space=pl.ANY),
        out_shape=jax.ShapeDtypeStruct((t_pad, w), F32),
        scratch_shapes=[pltpu.SemaphoreType.DMA(())],
        input_output_aliases={2: 0},
        compiler_params=_cparams(("arbitrary",)),
        name="moe_scatter_rows",
    )(pos3, hx, init)


def _gmm_kernel(elo_ref, ehi_ref, valid_ref, hs_ref, wg0, wu0, wd0, wg1, wu1, wd1, o_ref):
    del elo_ref, ehi_ref
    j = pl.program_id(0)
    d = o_ref.shape[-1]

    @pl.when(valid_ref[j] == 1)
    def _():
        hx = hs_ref[...]
        h = hx[:, :d].astype(BF16)

        def ffn(wg, wu, wd):
            a = jnp.dot(h, wg[...], preferred_element_type=F32)
            u = jnp.dot(h, wu[...], preferred_element_type=F32)
            act = (jax.nn.silu(a) * u).astype(BF16)
            return jnp.dot(act, wd[...], preferred_element_type=F32)

        o_ref[...] = hx[:, d:d + 1] * ffn(wg0, wu0, wd0) + hx[:, d + 1:d + 2] * ffn(wg1, wu1, wd1)

    @pl.when(valid_ref[j] == 0)
    def _():
        o_ref[...] = jnp.zeros_like(o_ref)


def _gmm(hs, e_lo, e_hi, valid, wg, wu, wd, li):
    t_pad, w = hs.shape
    d = w - LANE
    dff = wg.shape[-1]
    up = lambda sel: pl.BlockSpec((None, None, d, dff), lambda j, lo, hi, v: (li, sel(lo, hi)[j], 0, 0))
    dn = lambda sel: pl.BlockSpec((None, None, dff, d), lambda j, lo, hi, v: (li, sel(lo, hi)[j], 0, 0))
    first = lambda lo, hi: lo
    second = lambda lo, hi: hi
    return pl.pallas_call(
        _gmm_kernel,
        grid_spec=pltpu.PrefetchScalarGridSpec(
            num_scalar_prefetch=3,
            grid=(t_pad // TME,),
            in_specs=[pl.BlockSpec((TME, w), lambda j, lo, hi, v: (j, 0)),
                      up(first), up(first), dn(first), up(second), up(second), dn(second)],
            out_specs=pl.BlockSpec((TME, d), lambda j, lo, hi, v: (j, 0))),
        out_shape=jax.ShapeDtypeStruct((t_pad, d), F32),
        compiler_params=_cparams(("arbitrary",)),
        name="moe_grouped_ffn",
    )(e_lo, e_hi, valid, hs, wg, wu, wd, wg, wu, wd)


def _route_tables(rt, cnt, t_pad):
    b, _, s = rt.shape
    nb = N_GROUPS * len(PAIRS)
    bucket = rt[:, 0, :].astype(jnp.int32)
    rank = rt[:, 1, :].astype(jnp.int32)
    counts = cnt[:nb, 0].astype(jnp.int32)
    padded = ((counts + TME - 1) // TME) * TME
    ids = jnp.arange(nb, dtype=jnp.int32)
    ends = jnp.sum(jnp.where(ids[None, :] <= ids[:, None], padded[None, :], 0), axis=1)
    starts = ends - padded
    start_of = jnp.sum(jnp.where(bucket[..., None] == ids, starts, 0), axis=-1)
    pos = (start_of + rank).reshape(b * s // TM, 1, TM)
    n_tiles = t_pad // TME
    n_used = ends[-1] // TME
    tile = jnp.arange(n_tiles, dtype=jnp.int32)
    valid = (tile < n_used).astype(jnp.int32)
    tile_c = jnp.minimum(tile, n_used - 1)
    tb = jnp.minimum(jnp.sum((ends[None, :] <= tile_c[:, None] * TME).astype(jnp.int32), axis=1), nb - 1)
    e_lo_of = jnp.asarray([g * EXPERTS_PER_GROUP + p[0] for g in range(N_GROUPS) for p in PAIRS], jnp.int32)
    e_hi_of = jnp.asarray([g * EXPERTS_PER_GROUP + p[1] for g in range(N_GROUPS) for p in PAIRS], jnp.int32)
    hit = tb[:, None] == ids
    e_lo = jnp.sum(jnp.where(hit, e_lo_of, 0), axis=1)
    e_hi = jnp.sum(jnp.where(hit, e_hi_of, 0), axis=1)
    return pos, e_lo, e_hi, valid


def _issue_gather(pos_ref, y_ref, buf_ref, sem, slot):
    def body(r, c):
        p = pos_ref[0, r]
        pltpu.make_async_copy(y_ref.at[pl.ds(p, 1), :], buf_ref.at[slot, pl.ds(r, 1), :], sem.at[slot]).start()
        return c

    lax.fori_loop(0, TM, body, 0, unroll=8)


def _gather_tile(pos_ref, posn_ref, y_ref, buf_ref, sem):
    ni = pl.num_programs(1)
    lin = pl.program_id(0) * ni + pl.program_id(1)
    slot = lin % 2

    @pl.when(lin == 0)
    def _():
        _issue_gather(pos_ref, y_ref, buf_ref, sem, slot)

    @pl.when(lin + 1 < pl.num_programs(0) * ni)
    def _():
        _issue_gather(posn_ref, y_ref, buf_ref, sem, 1 - slot)

    pltpu.make_async_copy(y_ref.at[pl.ds(0, TM), :], buf_ref.at[slot], sem.at[slot]).wait()
    return buf_ref[slot]


def _combine_kernel(pos_ref, posn_ref, x_ref, mod_ref, y_ref, xo_ref, buf_ref, sem):
    f = _gather_tile(pos_ref, posn_ref, y_ref, buf_ref, sem)
    mod = mod_ref[...]
    xo_ref[...] = x_ref[...] + mod[5:6] * f


def _final_kernel(pos_ref, posn_ref, x_ref, mod_ref, g_ref, y_ref, o_ref, buf_ref, sem):
    f = _gather_tile(pos_ref, posn_ref, y_ref, buf_ref, sem)
    mod = mod_ref[...]
    x = x_ref[...] + mod[5:6] * f
    var = jnp.mean(x * x, axis=-1, keepdims=True)
    o_ref[...] = (x * lax.rsqrt(var + EPS)) * g_ref[...]


def _pos_specs(nt, ni, nb):
    def cur(bb, i):
        return (bb * nt + i, 0, 0)

    def nxt(bb, i):
        wrap = i + 1 == ni
        b2 = jnp.minimum(jnp.where(wrap, bb + 1, bb), nb - 1)
        i2 = jnp.where(wrap, jnp.where(bb + 1 == nb, i, 0), i + 1)
        return (b2 * nt + i2, 0, 0)

    return [pl.BlockSpec((None, 1, TM), cur, memory_space=pltpu.SMEM),
            pl.BlockSpec((None, 1, TM), nxt, memory_space=pltpu.SMEM)]


def _combine(xa, ys, pos3, mods, li):
    b, s, d = xa.shape
    nt = s // TM
    nl = nt - 1
    tok = pl.BlockSpec((None, TM, d), lambda bb, i: (bb, i, 0))
    return pl.pallas_call(
        _combine_kernel,
        grid=(b, nt),
        in_specs=_pos_specs(nt, nt, b) + [
            tok,
            pl.BlockSpec((None, None, None, 6, d), lambda bb, i: (li, bb, i // nl, 0, 0)),
            pl.BlockSpec(memory_space=pl.ANY)],
        out_specs=tok,
        out_shape=jax.ShapeDtypeStruct((b, s, d), F32),
        scratch_shapes=[pltpu.VMEM((2, TM, d), F32), pltpu.SemaphoreType.DMA((2,))],
        input_output_aliases={2: 0},
        compiler_params=_cparams(("arbitrary", "arbitrary")),
        name="moe_gather_residual",
    )(pos3, pos3, xa, mods, ys)


def _final(xa, ys, pos3, mods, g_final, li, n_lat):
    b, s, d = xa.shape
    nt = s // TM
    nl = n_lat // TM
    tok = pl.BlockSpec((None, TM, d), lambda bb, i: (bb, i, 0))
    return pl.pallas_call(
        _final_kernel,
        grid=(b, nl),
        in_specs=_pos_specs(nt, nl, b) + [
            tok,
            pl.BlockSpec((None, None, None, 6, d), lambda bb, i: (li, bb, 0, 0, 0)),
            pl.BlockSpec((1, d), lambda bb, i: (0, 0)),
            pl.BlockSpec(memory_space=pl.ANY)],
        out_specs=tok,
        out_shape=jax.ShapeDtypeStruct((b, n_lat, d), F32),
        scratch_shapes=[pltpu.VMEM((2, TM, d), F32), pltpu.SemaphoreType.DMA((2,))],
        compiler_params=_cparams(("arbitrary", "arbitrary")),
        name="moe_gather_residual_final_norm",
    )(pos3, pos3, xa, mods, g_final, ys)


def _rope_tables(n_lat, n_ctx, rot_dim, lane_off):
    rows = n_lat // GRID_W
    r = jnp.repeat(jnp.arange(rows, dtype=F32), GRID_W)[:n_lat]
    c = jnp.tile(jnp.arange(GRID_W, dtype=F32), rows)[:n_lat]
    axis_dim = rot_dim // 2
    inv_freq = ROPE_BASE ** (-jnp.arange(0, axis_dim, 2, dtype=F32) / axis_dim)
    ang_r = r[:, None] * inv_freq
    ang_c = c[:, None] * inv_freq
    ang = jnp.concatenate([ang_r, ang_r, ang_c, ang_c], axis=-1)
    cos, sin = jnp.cos(ang), jnp.sin(ang)
    grp = (np.arange(rot_dim) // (rot_dim // 4)) % 2
    sin_a = jnp.where(grp == 0, -sin, 0.0)
    sin_b = jnp.where(grp == 1, sin, 0.0)

    def place(t, fill):
        reps = 2 if rot_dim == 64 else 1
        blk = jnp.full((n_lat, LANE), fill, F32)
        for k in range(reps):
            blk = blk.at[:, lane_off + k * rot_dim:lane_off + (k + 1) * rot_dim].set(t)
        ctx = jnp.full((n_ctx, LANE), fill, F32)
        return jnp.concatenate([blk, ctx], axis=0)

    return place(cos, 1.0), place(sin_a, 0.0), place(sin_b, 0.0)


def _dft_mats(n, scale):
    r = 1
    while r * r < n:
        r *= 2
    k = jnp.arange(n, dtype=jnp.int32)[:, None]
    a = jnp.arange(n // r, dtype=jnp.int32)[None, :]
    c = jnp.arange(r, dtype=jnp.int32)[None, :]
    ang_a = ((k * a * r) % n).astype(F32) * (2.0 * np.pi / n)
    ang_c = ((k * c) % n).astype(F32) * (2.0 * np.pi / n)
    ca, sa = (jnp.cos(ang_a) * scale)[:, :, None], (jnp.sin(ang_a) * scale)[:, :, None]
    cc, sc = jnp.cos(ang_c)[:, None, :], jnp.sin(ang_c)[:, None, :]
    cos = (ca * cc - sa * sc).reshape(n, n)
    sin = (sa * cc + ca * sc).reshape(n, n)
    return cos.astype(BF16), sin.astype(BF16)


def _even_w_in(w):
    d = w.shape[0]
    c3 = 3 * CONV_DIM
    qcols = []
    per_kv = N_Q_HEADS // N_KV_HEADS
    for h in range(N_Q_HEADS):
        wq = w[:, c3 + h * HEAD_DIM:c3 + (h + 1) * HEAD_DIM]
        z = jnp.zeros((d, HEAD_DIM), w.dtype)
        qcols.append(jnp.concatenate([wq, z] if h // per_kv == 0 else [z, wq], axis=1))
    return jnp.concatenate([w[:, :c3]] + qcols + [w[:, c3 + N_Q_HEADS * HEAD_DIM:]], axis=1).astype(BF16)


def _odd_w_in(w):
    d = w.shape[0]
    o = Q_LORA + KV_LORA
    kr = jnp.concatenate([jnp.zeros((d, QK_NOPE), w.dtype), w[:, o:o + QK_ROPE],
                          jnp.zeros((d, LANE - QK_NOPE - QK_ROPE), w.dtype)], axis=1)
    return jnp.concatenate([w[:, :o], kr, w[:, o + QK_ROPE:]], axis=1).astype(BF16)


def _pad_heads(w, per_head, keep_from, keep, heads):
    rows = w.shape[0]
    w3 = w.reshape(rows, heads, per_head)[:, :, keep_from:keep_from + keep]
    w3 = jnp.pad(w3, ((0, 0), (0, 0), (0, LANE - keep)))
    return w3.reshape(rows, heads * LANE).astype(BF16)


def _channel_dft():
    gd = FOURIER_DIM // FOURIER_GROUPS
    k = np.arange(gd)
    ang = 2.0 * np.pi * ((k[:, None] * k[None, :]) % gd) / gd
    eye = np.eye(FOURIER_GROUPS)
    wc = np.kron(eye, np.cos(ang))
    ws = np.kron(eye, np.sin(ang))
    return jnp.asarray(np.concatenate([wc, ws], axis=1), BF16)


def kernel(x, c, ctx, c_ctx, ada_w, ada_b, norm_mix_g, norm_ffn_g, ev_w_in, ev_conv_w, ev_sink, ev_w_out,
           od_w_in, od_q_norm_g, od_w_uq, od_kv_norm_g, od_w_ukv, od_w_out, router_w, router_bias,
           ex_w_gate, ex_w_up, ex_w_down, final_norm_g):
    b, n, d = x.shape
    l = ctx.shape[1]
    s = n + l
    depth = ada_w.shape[0]
    assert l == TM and n % 512 == 0

    rows = 8 * ((b + 1 + 7) // 8)
    cvec = jnp.concatenate([c, c_ctx[None, :], jnp.zeros((rows - b - 1, d), F32)], axis=0)
    mod_all = _modulation(cvec, ada_w, ada_b).reshape(depth, rows, 6, d)
    mods = jnp.stack([mod_all[:, :b], jnp.broadcast_to(mod_all[:, b:b + 1], (depth, b, 6, d))], axis=2)

    tabs_even = _rope_tables(n, l, HEAD_DIM, 0)
    tabs_odd = _rope_tables(n, l, QK_ROPE, QK_NOPE)
    gd = FOURIER_DIM // FOURIER_GROUPS
    dft_lat = _dft_mats(n, (n * gd) ** -0.5)
    dft_ctx = _dft_mats(l, (l * gd) ** -0.5)
    wf = _channel_dft()

    g_mix = norm_mix_g.reshape(depth, 1, d)
    g_ffn = norm_ffn_g.reshape(depth, 1, d)
    rw_t = router_w.T.astype(BF16)
    rb = router_bias.reshape(N_EXPERTS, 1).astype(F32)
    wg, wu, wd = ex_w_gate.astype(BF16), ex_w_up.astype(BF16), ex_w_down.astype(BF16)

    xa = jnp.concatenate([x, ctx], axis=1)
    t_pad = (-(-(b * s) // TME) + N_GROUPS * len(PAIRS)) * TME
    hs = jnp.zeros((t_pad, d + LANE), F32)
    out = None
    for li in range(depth):
        j = li // 2
        if li % 2 == 0:
            u, ab, q, k, v = _pre_even(xa, mods, g_mix, _even_w_in(ev_w_in[j]), tabs_even, li)
            ya, yb = _even_mix(u, ab, q, k, v, ev_conv_w[j], ev_sink[j], n)
            w_out = ev_w_out[j].astype(BF16)
        else:
            q, k, v, ycs = _pre_odd(
                xa, mods, g_mix, _odd_w_in(od_w_in[j]),
                od_q_norm_g[j].reshape(1, Q_LORA),
                _pad_heads(od_w_uq[j], QK_NOPE + QK_ROPE, 0, QK_NOPE + QK_ROPE, MLA_HEADS),
                od_kv_norm_g[j].reshape(1, KV_LORA),
                _pad_heads(od_w_ukv[j], QK_NOPE + V_HEAD, 0, QK_NOPE, MLA_HEADS),
                _pad_heads(od_w_ukv[j], QK_NOPE + V_HEAD, QK_NOPE, V_HEAD, MLA_HEADS),
                wf, tabs_odd, li)
            ya = _mla(q, k, v, n)
            yb = _fourier(ycs, dft_lat, dft_ctx, n)
            w_out = od_w_out[j].astype(BF16)
        half = w_out.shape[0] // 2
        xa, hx, rt, cnt = _post(xa, ya, yb, mods, w_out[:half], w_out[half:], g_ffn, rw_t, rb, li)
        pos3, e_lo, e_hi, valid = _route_tables(rt, cnt, t_pad)
        hs = _scatter_rows(hx.reshape(b * s, d + LANE), pos3, hs)
        ys = _gmm(hs, e_lo, e_hi, valid, wg, wu, wd, li)
        if li == depth - 1:
            out = _final(xa, ys, pos3, mods, final_norm_g.reshape(1, d), li, n)
        else:
            xa = _combine(xa, ys, pos3, mods, li)
    return out
```

```python
import functools

import jax
import jax.numpy as jnp
import numpy as np
from jax import lax
from jax.experimental import pallas as pl
from jax.experimental.pallas import tpu as pltpu

F32 = jnp.float32
BF16 = jnp.bfloat16

EPS = 1e-6
ROPE_BASE = 10000.0
GRID_W = 64
NEG_INF = -1e30
LOG2E = 1.4426950408889634
LANE = 128
TM = 256
WIN = 128
N_Q_HEADS = 8
N_KV_HEADS = 2
HEAD_DIM = 64
CONV_DIM = 512
MLA_HEADS = 8
Q_LORA = 256
KV_LORA = 128
QK_NOPE = 64
QK_ROPE = 32
V_HEAD = 64
FOURIER_DIM = 512
FOURIER_GROUPS = 4
N_EXPERTS = 16
N_GROUPS = 4
EXPERTS_PER_GROUP = 4
PAIRS = ((0, 1), (0, 2), (0, 3), (1, 2), (1, 3), (2, 3))
NB_PAD = 32
TME = 256
DMA_QUEUES = 2
TQ_MLA = 512
VMEM_LIMIT = 56 * 1024 * 1024


def _cparams(sem):
    return pltpu.CompilerParams(dimension_semantics=sem, vmem_limit_bytes=VMEM_LIMIT)


def _norm_mod(x, g, shift, scale):
    var = jnp.mean(x * x, axis=-1, keepdims=True)
    return (x * lax.rsqrt(var + EPS)) * g * (1.0 + scale) + shift


def _rope(x, cos, sin_a, sin_b, shift):
    w = x.shape[-1]
    return x * cos + pltpu.roll(x, w - shift, 1) * sin_a + pltpu.roll(x, shift, 1) * sin_b


def _mod_kernel(c_ref, w_ref, b_ref, o_ref):
    c = c_ref[...]
    o_ref[...] = jnp.dot(jax.nn.silu(c), w_ref[...], preferred_element_type=F32,
                         precision=lax.Precision.HIGHEST) + b_ref[...]


def _modulation(cvec, ada_w, ada_b):
    depth, d, d6 = ada_w.shape
    tn = d6 // 4
    rows = cvec.shape[0]
    return pl.pallas_call(
        _mod_kernel,
        grid=(depth, d6 // tn),
        in_specs=[pl.BlockSpec((rows, d), lambda l, j: (0, 0)),
                  pl.BlockSpec((None, d, tn), lambda l, j: (l, 0, j)),
                  pl.BlockSpec((None, 1, tn), lambda l, j: (l, 0, j))],
        out_specs=pl.BlockSpec((None, rows, tn), lambda l, j: (l, 0, j)),
        out_shape=jax.ShapeDtypeStruct((depth, rows, d6), F32),
        compiler_params=_cparams(("arbitrary", "arbitrary")),
        name="adaln_modulation",
    )(cvec, ada_w, ada_b.reshape(depth, 1, d6))


def _pre_even_kernel(x_ref, mod_ref, g_ref, w_ref, cos_ref, sa_ref, sb_ref,
                     u_ref, ab_ref, q_ref, k_ref, v_ref):
    mod = mod_ref[...]
    h = _norm_mod(x_ref[...], g_ref[...], mod[0:1], mod[1:2])
    p = jnp.dot(h.astype(BF16), w_ref[...], preferred_element_type=F32)
    c = CONV_DIM
    ax, ab, ac = p[:, 0:c], p[:, c:2 * c], p[:, 2 * c:3 * c]
    qw = N_Q_HEADS * LANE
    q = p[:, 3 * c:3 * c + qw]
    k = p[:, 3 * c + qw:3 * c + qw + LANE]
    v = p[:, 3 * c + qw + LANE:3 * c + qw + 2 * LANE]
    cos, sa, sb = cos_ref[...], sa_ref[...], sb_ref[...]
    cos8 = jnp.concatenate([cos] * N_Q_HEADS, axis=1)
    sa8 = jnp.concatenate([sa] * N_Q_HEADS, axis=1)
    sb8 = jnp.concatenate([sb] * N_Q_HEADS, axis=1)
    u_ref[...] = (ac * ax).astype(BF16)
    ab_ref[...] = ab.astype(BF16)
    q_ref[...] = (_rope(q, cos8, sa8, sb8, 16) * (HEAD_DIM ** -0.5 * LOG2E)).astype(BF16)
    k_ref[...] = _rope(k, cos, sa, sb, 16).astype(BF16)
    v_ref[...] = v.astype(BF16)


def _pre_even(xa, mods, g_mix, w_in, tabs, li):
    b, s, d = xa.shape
    nt = s // TM
    nl = nt - 1
    wcols = w_in.shape[1]
    qw = N_Q_HEADS * LANE
    tab_spec = pl.BlockSpec((TM, LANE), lambda i, bb: (i, 0))
    tok = lambda w: pl.BlockSpec((None, TM, w), lambda i, bb: (bb, i, 0))
    return pl.pallas_call(
        _pre_even_kernel,
        grid=(nt, b),
        in_specs=[tok(d),
                  pl.BlockSpec((None, None, None, 6, d), lambda i, bb: (li, bb, i // nl, 0, 0)),
                  pl.BlockSpec((None, 1, d), lambda i, bb: (li, 0, 0)),
                  pl.BlockSpec((d, wcols), lambda i, bb: (0, 0)),
                  tab_spec, tab_spec, tab_spec],
        out_specs=[tok(CONV_DIM), tok(CONV_DIM), tok(qw), tok(LANE), tok(LANE)],
        out_shape=[jax.ShapeDtypeStruct((b, s, CONV_DIM), BF16),
                   jax.ShapeDtypeStruct((b, s, CONV_DIM), BF16),
                   jax.ShapeDtypeStruct((b, s, qw), BF16),
                   jax.ShapeDtypeStruct((b, s, LANE), BF16),
                   jax.ShapeDtypeStruct((b, s, LANE), BF16)],
        compiler_params=_cparams(("arbitrary", "arbitrary")),
        name="even_in_proj",
    )(xa, mods, g_mix, w_in, *tabs)


def _even_mix_kernel(sink_ref, q_ref, k0_ref, k1_ref, k2_ref, k3_ref, kc_ref,
                     v0_ref, v1_ref, v2_ref, v3_ref, vc_ref,
                     u_ref, up_ref, un_ref, ab_ref, cw_ref, yc_ref, ya_ref, *, nl, n_lat):
    i = pl.program_id(1)
    nt = pl.num_programs(1)
    is_lat = i < nl
    u = u_ref[...].astype(F32)
    row = lax.broadcasted_iota(jnp.int32, u.shape, 0)
    first = jnp.logical_or(i == 0, i == nl)
    last = jnp.logical_or(i == nl - 1, i == nt - 1)
    prev_row = jnp.where(first, 0.0, up_ref[15:16, :].astype(F32))
    next_row = jnp.where(last, 0.0, un_ref[0:1, :].astype(F32))
    u_prev = jnp.where(row == 0, prev_row, pltpu.roll(u, 1, 0))
    u_next = jnp.where(row == TM - 1, next_row, pltpu.roll(u, TM - 1, 0))
    cw = cw_ref[...]
    conv = u_prev * cw[0:1] + u * cw[1:2] + u_next * cw[2:3]
    yc_ref[...] = (ab_ref[...].astype(F32) * conv).astype(BF16)

    k_all = jnp.concatenate([k0_ref[...], k1_ref[...], k2_ref[...], k3_ref[...], kc_ref[...]], axis=0)
    v_all = jnp.concatenate([v0_ref[...], v1_ref[...], v2_ref[...], v3_ref[...], vc_ref[...]], axis=0)
    nk = k_all.shape[0]
    nwin = 4 * WIN
    qi = lax.broadcasted_iota(jnp.int32, (TM, nk), 0)
    kj = lax.broadcasted_iota(jnp.int32, (TM, nk), 1)
    kpos = i * TM - WIN + kj
    valid_win = (jnp.abs(kj - WIN - qi) <= WIN) & (kpos >= 0) & (kpos < n_lat) & is_lat
    bias = jnp.where((kj >= nwin) | valid_win, 0.0, NEG_INF).astype(F32)
    lane = lax.broadcasted_iota(jnp.int32, (TM, LANE), 1)
    outs = []
    per_kv = N_Q_HEADS // N_KV_HEADS
    for kvh in range(N_KV_HEADS):
        heads = range(kvh * per_kv, (kvh + 1) * per_kv)
        qs = jnp.concatenate([q_ref[:, h * LANE:(h + 1) * LANE] for h in heads], axis=0)
        s = lax.dot_general(qs, k_all, (((1,), (1,)), ((), ())), preferred_element_type=F32)
        s = (s.reshape(per_kv, TM, nk) + bias[None]).reshape(per_kv * TM, nk)
        sink = jnp.concatenate([jnp.full((TM, 1), sink_ref[h] * LOG2E, F32) for h in heads], axis=0)
        m = jnp.maximum(jnp.max(s, axis=1, keepdims=True), sink)
        p = jnp.exp2(s - m)
        l = jnp.sum(p, axis=1, keepdims=True) + jnp.exp2(sink - m)
        o = jnp.dot(p.astype(BF16), v_all, preferred_element_type=F32) / l
        if kvh == 1:
            o = pltpu.roll(o, HEAD_DIM, 1)
        for pr in range(per_kv // 2):
            o_a = o[(2 * pr) * TM:(2 * pr + 1) * TM]
            o_b = o[(2 * pr + 1) * TM:(2 * pr + 2) * TM]
            outs.append(jnp.where(lane < HEAD_DIM, o_a, pltpu.roll(o_b, HEAD_DIM, 1)))
    ya_ref[...] = jnp.concatenate(outs, axis=1).astype(BF16)


def _even_mix(u, ab, q, k, v, conv_w, sink, n_lat):
    b, s, _ = u.shape
    nt = s // TM
    nl = n_lat // TM
    nkb = s // WIN
    rpt = TM // WIN
    qw = q.shape[-1]

    def kblk(off):
        return pl.BlockSpec((None, WIN, LANE),
                            lambda bb, i: (bb, jnp.clip(i * rpt + off, 0, nkb - 1), 0))

    ctx_spec = pl.BlockSpec((None, s - n_lat, LANE), lambda bb, i: (bb, n_lat // (s - n_lat), 0))
    halo = TM // 16
    u16 = u.reshape(b, s // 16, 16, CONV_DIM)
    kern = functools.partial(_even_mix_kernel, nl=nl, n_lat=n_lat)
    tok = lambda w: pl.BlockSpec((None, TM, w), lambda bb, i: (bb, i, 0))
    return pl.pallas_call(
        kern,
        grid=(b, nt),
        in_specs=[pl.BlockSpec(memory_space=pltpu.SMEM),
                  tok(qw),
                  kblk(-1), kblk(0), kblk(1), kblk(2), ctx_spec,
                  kblk(-1), kblk(0), kblk(1), kblk(2), ctx_spec,
                  tok(CONV_DIM),
                  pl.BlockSpec((None, None, 16, CONV_DIM),
                               lambda bb, i: (bb, jnp.maximum(i * halo - 1, 0), 0, 0)),
                  pl.BlockSpec((None, None, 16, CONV_DIM),
                               lambda bb, i: (bb, jnp.minimum((i + 1) * halo, s // 16 - 1), 0, 0)),
                  tok(CONV_DIM),
                  pl.BlockSpec((3, CONV_DIM), lambda bb, i: (0, 0))],
        out_specs=[tok(CONV_DIM), tok(N_Q_HEADS * HEAD_DIM)],
        out_shape=[jax.ShapeDtypeStruct((b, s, CONV_DIM), BF16),
                   jax.ShapeDtypeStruct((b, s, N_Q_HEADS * HEAD_DIM), BF16)],
        compiler_params=_cparams(("arbitrary", "arbitrary")),
        name="even_conv_window_attn",
    )(sink, q, k, k, k, k, k, v, v, v, v, v, u, u16, u16, ab, conv_w)


def _pre_odd_kernel(x_ref, mod_ref, g_ref, w_ref, gq_ref, wuq_ref, gkv_ref, wuk_ref, wuv_ref,
                    wf_ref, cos_ref, sa_ref, sb_ref, q_ref, k_ref, v_ref, y_ref):
    mod = mod_ref[...]
    h = _norm_mod(x_ref[...], g_ref[...], mod[0:1], mod[1:2])
    p = jnp.dot(h.astype(BF16), w_ref[...], preferred_element_type=F32)
    cq = p[:, 0:Q_LORA]
    ckv = p[:, Q_LORA:Q_LORA + KV_LORA]
    kr = p[:, Q_LORA + KV_LORA:Q_LORA + KV_LORA + LANE]
    f = p[:, Q_LORA + KV_LORA + LANE:]
    cos, sa, sb = cos_ref[...], sa_ref[...], sb_ref[...]
    cos8 = jnp.concatenate([cos] * MLA_HEADS, axis=1)
    sa8 = jnp.concatenate([sa] * MLA_HEADS, axis=1)
    sb8 = jnp.concatenate([sb] * MLA_HEADS, axis=1)

    cqn = cq * lax.rsqrt(jnp.mean(cq * cq, axis=-1, keepdims=True) + EPS) * gq_ref[...]
    q = jnp.dot(cqn.astype(BF16), wuq_ref[...], preferred_element_type=F32)
    q = _rope(q, cos8, sa8, sb8, 8) * ((QK_NOPE + QK_ROPE) ** -0.5 * LOG2E)
    q_ref[...] = q.astype(BF16)

    ckvn = (ckv * lax.rsqrt(jnp.mean(ckv * ckv, axis=-1, keepdims=True) + EPS) * gkv_ref[...]).astype(BF16)
    kn = jnp.dot(ckvn, wuk_ref[...], preferred_element_type=F32)
    krr = _rope(kr, cos, sa, sb, 8)
    k_ref[...] = (kn + jnp.concatenate([krr] * MLA_HEADS, axis=1)).astype(BF16)
    v = jnp.dot(ckvn, wuv_ref[...], preferred_element_type=F32)
    one_lane = lax.broadcasted_iota(jnp.int32, v.shape, 1) % LANE == V_HEAD
    v_ref[...] = jnp.where(one_lane, 1.0, v).astype(BF16)
    y_ref[...] = jnp.dot(f.astype(BF16), wf_ref[...], preferred_element_type=F32).astype(BF16)


def _pre_odd(xa, mods, g_mix, w_in, gq, wuq, gkv, wuk, wuv, wf, tabs, li):
    b, s, d = xa.shape
    nt = s // TM
    nl = nt - 1
    hw = MLA_HEADS * LANE
    vw = hw
    tab_spec = pl.BlockSpec((TM, LANE), lambda i, bb: (i, 0))
    tok = lambda w: pl.BlockSpec((None, TM, w), lambda i, bb: (bb, i, 0))
    full = lambda a: pl.BlockSpec(a.shape, lambda i, bb: (0,) * a.ndim)
    return pl.pallas_call(
        _pre_odd_kernel,
        grid=(nt, b),
        in_specs=[tok(d),
                  pl.BlockSpec((None, None, None, 6, d), lambda i, bb: (li, bb, i // nl, 0, 0)),
                  pl.BlockSpec((None, 1, d), lambda i, bb: (li, 0, 0)),
                  full(w_in), full(gq), full(wuq), full(gkv), full(wuk), full(wuv), full(wf),
                  tab_spec, tab_spec, tab_spec],
        out_specs=[tok(hw), tok(hw), tok(vw), tok(2 * FOURIER_DIM)],
        out_shape=[jax.ShapeDtypeStruct((b, s, hw), BF16),
                   jax.ShapeDtypeStruct((b, s, hw), BF16),
                   jax.ShapeDtypeStruct((b, s, vw), BF16),
                   jax.ShapeDtypeStruct((b, s, 2 * FOURIER_DIM), BF16)],
        compiler_params=_cparams(("arbitrary", "arbitrary")),
        name="odd_in_proj",
    )(xa, mods, g_mix, w_in, gq, wuq, gkv, wuk, wuv, wf, *tabs)


def _mla_kernel(q_ref, k_ref, v_ref, o_ref):
    lane = lax.broadcasted_iota(jnp.int32, (q_ref.shape[0], LANE), 1)
    outs = []
    for pair in range(MLA_HEADS // 2):
        o_pair = []
        for h in (2 * pair, 2 * pair + 1):
            q = q_ref[:, h * LANE:(h + 1) * LANE]
            k = k_ref[:, h * LANE:(h + 1) * LANE]
            v = v_ref[:, h * LANE:(h + 1) * LANE]
            s = lax.dot_general(q, k, (((1,), (1,)), ((), ())), preferred_element_type=F32)
            m = jnp.max(s, axis=1, keepdims=True)
            p = jnp.exp2(s - m)
            o = jnp.dot(p.astype(BF16), v, preferred_element_type=F32)
            o_pair.append(o / o[:, V_HEAD:V_HEAD + 1])
        outs.append(jnp.where(lane < V_HEAD, o_pair[0], pltpu.roll(o_pair[1], V_HEAD, 1)))
    o_ref[...] = jnp.concatenate(outs, axis=1).astype(BF16)


def _mla(q, k, v, n_lat):
    b, s, hw = q.shape
    vw = MLA_HEADS * V_HEAD
    n_ctx = s - n_lat
    tq = TQ_MLA if n_lat % TQ_MLA == 0 else TM
    once = pl.Buffered(1)
    lat = pl.pallas_call(
        _mla_kernel,
        grid=(b, n_lat // tq),
        in_specs=[pl.BlockSpec((None, tq, hw), lambda bb, i: (bb, i, 0)),
                  pl.BlockSpec((None, s, hw), lambda bb, i: (bb, 0, 0), pipeline_mode=once),
                  pl.BlockSpec((None, s, hw), lambda bb, i: (bb, 0, 0), pipeline_mode=once)],
        out_specs=pl.BlockSpec((None, tq, vw), lambda bb, i: (bb, i, 0)),
        out_shape=jax.ShapeDtypeStruct((b, n_lat, vw), BF16),
        compiler_params=_cparams(("arbitrary", "arbitrary")),
        name="mla_attention",
    )(q, k, v)
    cblk = n_lat // n_ctx
    ctx_spec = pl.BlockSpec((None, n_ctx, hw), lambda bb: (bb, cblk, 0))
    ctx = pl.pallas_call(
        _mla_kernel,
        grid=(b,),
        in_specs=[ctx_spec, ctx_spec, ctx_spec],
        out_specs=pl.BlockSpec((None, n_ctx, vw), lambda bb: (bb, 0, 0)),
        out_shape=jax.ShapeDtypeStruct((b, n_ctx, vw), BF16),
        compiler_params=_cparams(("arbitrary",)),
        name="mla_attention_context",
    )(q, k, v)
    return jnp.concatenate([lat, ctx], axis=1)


def _dft_kernel(ac_ref, as_ref, cc_ref, cs_ref, yc_ref, ys_ref, ycc_ref, ysc_ref, o_ref, *, nl):
    i = pl.program_id(1)

    @pl.when(i < nl)
    def _():
        o_ref[...] = (jnp.dot(ac_ref[...], yc_ref[...], preferred_element_type=F32)
                      - jnp.dot(as_ref[...], ys_ref[...], preferred_element_type=F32)).astype(BF16)

    @pl.when(i >= nl)
    def _():
        o_ref[...] = (jnp.dot(cc_ref[...], ycc_ref[...], preferred_element_type=F32)
                      - jnp.dot(cs_ref[...], ysc_ref[...], preferred_element_type=F32)).astype(BF16)


def _fourier(ycs, dft_lat, dft_ctx, n_lat):
    b, s, _ = ycs.shape
    fd = FOURIER_DIM
    n_ctx = s - n_lat
    nl = n_lat // TM
    ac, as_ = dft_lat
    cc, cs = dft_ctx
    rows = pl.BlockSpec((TM, n_lat), lambda bb, i: (jnp.minimum(i, nl - 1), 0))
    small = pl.BlockSpec((n_ctx, n_ctx), lambda bb, i: (0, 0))
    return pl.pallas_call(
        functools.partial(_dft_kernel, nl=nl),
        grid=(b, s // TM),
        in_specs=[rows, rows, small, small,
                  pl.BlockSpec((None, n_lat, fd), lambda bb, i: (bb, 0, 0)),
                  pl.BlockSpec((None, n_lat, fd), lambda bb, i: (bb, 0, 1)),
                  pl.BlockSpec((None, n_ctx, fd), lambda bb, i: (bb, n_lat // n_ctx, 0)),
                  pl.BlockSpec((None, n_ctx, fd), lambda bb, i: (bb, n_lat // n_ctx, 1))],
        out_specs=pl.BlockSpec((None, TM, fd), lambda bb, i: (bb, i, 0)),
        out_shape=jax.ShapeDtypeStruct((b, s, fd), BF16),
        compiler_params=_cparams(("arbitrary", "arbitrary")),
        name="fourier_mix",
    )(ac, as_, cc, cs, ycs, ycs, ycs, ycs)


def _post_kernel(x_ref, ya_ref, yb_ref, mod_ref, wa_ref, wb_ref, g_ref, rw_ref, rb_ref,
                 xo_ref, hx_ref, rt_ref, cnt_ref, carry_ref):
    @pl.when((pl.program_id(0) == 0) & (pl.program_id(1) == 0))
    def _():
        carry_ref[...] = jnp.zeros_like(carry_ref)

    mod = mod_ref[...]
    y = (jnp.dot(ya_ref[...], wa_ref[...], preferred_element_type=F32)
         + jnp.dot(yb_ref[...], wb_ref[...], preferred_element_type=F32))
    x = x_ref[...] + mod[2:3] * y
    xo_ref[...] = x
    h2f = _norm_mod(x, g_ref[...], mod[3:4], mod[4:5])
    h2 = h2f.astype(BF16)
    logits = lax.dot_general(rw_ref[...], h2, (((1,), (1,)), ((), ())), preferred_element_type=F32)
    scores = jax.nn.sigmoid(logits)
    biased = scores + rb_ref[...]
    sc = [scores[e:e + 1, :] for e in range(N_EXPERTS)]
    bs = [biased[e:e + 1, :] for e in range(N_EXPERTS)]
    sel, gval = [], []
    for g in range(N_GROUPS):
        ids = range(g * EXPERTS_PER_GROUP, (g + 1) * EXPERTS_PER_GROUP)
        gsum = None
        for e in ids:
            rank = sum(((bs[j] > bs[e]) if j > e else (bs[j] >= bs[e])).astype(jnp.int32)
                       for j in ids if j != e)
            s_e = rank < 2
            sel.append(s_e)
            term = jnp.where(s_e, bs[e], 0.0)
            gsum = term if gsum is None else gsum + term
        gval.append(gsum)
    gsel = []
    for g in range(N_GROUPS):
        ok = None
        for j in range(N_GROUPS):
            if j == g:
                continue
            c = (gval[g] > gval[j]) if j < g else (gval[g] >= gval[j])
            ok = c if ok is None else (ok & c)
        gsel.append(ok)
    chosen = [sel[e] & gsel[e // EXPERTS_PER_GROUP] for e in range(N_EXPERTS)]
    wsum = sum(jnp.where(chosen[e], sc[e], 0.0) for e in range(N_EXPERTS))
    gates = [jnp.where(chosen[e], sc[e], 0.0) / wsum for e in range(N_EXPERTS)]
    bucket = jnp.zeros_like(wsum)
    w_lo = jnp.zeros_like(wsum)
    w_hi = jnp.zeros_like(wsum)
    for g in range(N_GROUPS):
        for pi, (ea, eb) in enumerate(PAIRS):
            ea, eb = g * EXPERTS_PER_GROUP + ea, g * EXPERTS_PER_GROUP + eb
            cond = chosen[ea] & chosen[eb]
            bucket = jnp.where(cond, float(g * len(PAIRS) + pi), bucket)
            w_lo = jnp.where(cond, gates[ea], w_lo)
            w_hi = jnp.where(cond, gates[eb], w_hi)
    kio = lax.broadcasted_iota(jnp.int32, (NB_PAD, TM), 0).astype(F32)
    onehot = (kio == bucket).astype(F32)
    tri = (lax.broadcasted_iota(jnp.int32, (TM, TM), 0)
           <= lax.broadcasted_iota(jnp.int32, (TM, TM), 1)).astype(BF16)
    prefix = jnp.dot(onehot.astype(BF16), tri, preferred_element_type=F32)
    carry = carry_ref[...]
    rank = jnp.sum(onehot * (prefix + carry[:, 0:1]), axis=0, keepdims=True) - 1.0
    carry = carry + jnp.sum(onehot, axis=1, keepdims=True)
    carry_ref[...] = carry
    cnt_ref[...] = carry
    zrow = jnp.zeros_like(wsum)
    rt_ref[...] = jnp.concatenate([bucket, rank] + [zrow] * 6, axis=0)
    ext_t = jnp.concatenate([w_lo, w_hi] + [zrow] * (LANE - 2), axis=0)
    hx_ref[...] = jnp.concatenate([h2f, ext_t.T], axis=1)


def _post(xa, ya, yb, mods, w_out_a, w_out_b, g_ffn, rw_t, rb, li):
    b, s, d = xa.shape
    nt = s // TM
    nl = nt - 1
    half = ya.shape[-1]
    tok = lambda w: pl.BlockSpec((None, TM, w), lambda bb, i: (bb, i, 0))
    return pl.pallas_call(
        _post_kernel,
        grid=(b, nt),
        in_specs=[tok(d), tok(half), tok(half),
                  pl.BlockSpec((None, None, None, 6, d), lambda bb, i: (li, bb, i // nl, 0, 0)),
                  pl.BlockSpec((half, d), lambda bb, i: (0, 0)),
                  pl.BlockSpec((half, d), lambda bb, i: (0, 0)),
                  pl.BlockSpec((None, 1, d), lambda bb, i: (li, 0, 0)),
                  pl.BlockSpec((N_EXPERTS, d), lambda bb, i: (0, 0)),
                  pl.BlockSpec((N_EXPERTS, 1), lambda bb, i: (0, 0))],
        out_specs=[tok(d), tok(d + LANE),
                   pl.BlockSpec((None, 8, TM), lambda bb, i: (bb, 0, i)),
                   pl.BlockSpec((NB_PAD, LANE), lambda bb, i: (0, 0))],
        out_shape=[jax.ShapeDtypeStruct((b, s, d), F32),
                   jax.ShapeDtypeStruct((b, s, d + LANE), F32),
                   jax.ShapeDtypeStruct((b, 8, s), F32),
                   jax.ShapeDtypeStruct((NB_PAD, LANE), F32)],
        scratch_shapes=[pltpu.VMEM((NB_PAD, LANE), F32)],
        input_output_aliases={0: 0},
        compiler_params=_cparams(("arbitrary", "arbitrary")),
        name="out_proj_router",
    )(xa, ya, yb, mods, w_out_a, w_out_b, g_ffn, rw_t, rb)


def _scatter_kernel(pos_ref, h_ref, init_ref, o_ref, sem):
    del init_ref

    def body(r2, c):
        for prio in range(DMA_QUEUES):
            r = r2 * DMA_QUEUES + prio
            p = pos_ref[0, r]
            pltpu.make_async_copy(h_ref.at[pl.ds(r, 1), :], o_ref.at[pl.ds(p, 1), :], sem).start(priority=prio)
        return c

    lax.fori_loop(0, TM // DMA_QUEUES, body, 0, unroll=4)
    pltpu.make_async_copy(h_ref, o_ref.at[pl.ds(0, TM), :], sem).wait()


def _scatter_rows(hx, pos3, init):
    t, w = hx.shape
    t_pad = init.shape[0]
    return pl.pallas_call(
        _scatter_kernel,
        grid=(t // TM,),
        in_specs=[pl.BlockSpec((None, 1, TM), lambda i: (i, 0, 0), memory_space=pltpu.SMEM),
                  pl.BlockSpec((TM, w), lambda i: (i, 0)),
                  pl.BlockSpec(memory_space=pl.ANY)],
        out_specs=pl.BlockSpec(memory_space=pl.ANY),
        out_shape=jax.ShapeDtypeStruct((t_pad, w), F32),
        scratch_shapes=[pltpu.SemaphoreType.DMA(())],
        input_output_aliases={2: 0},
        compiler_params=_cparams(("arbitrary",)),
        name="moe_scatter_rows",
    )(pos3, hx, init)


def _gmm_kernel(elo_ref, ehi_ref, valid_ref, hs_ref, wg0, wu0, wd0, wg1, wu1, wd1, o_ref):
    del elo_ref, ehi_ref
    j = pl.program_id(0)
    d = o_ref.shape[-1]

    @pl.when(valid_ref[j] == 1)
    def _():
        hx = hs_ref[...]
        h = hx[:, :d].astype(BF16)

        def ffn(wg, wu, wd):
            a = jnp.dot(h, wg[...], preferred_element_type=F32)
            u = jnp.dot(h, wu[...], preferred_element_type=F32)
            act = (jax.nn.silu(a) * u).astype(BF16)
            return jnp.dot(act, wd[...], preferred_element_type=F32)

        o_ref[...] = hx[:, d:d + 1] * ffn(wg0, wu0, wd0) + hx[:, d + 1:d + 2] * ffn(wg1, wu1, wd1)

    @pl.when(valid_ref[j] == 0)
    def _():
        o_ref[...] = jnp.zeros_like(o_ref)


def _gmm(hs, e_lo, e_hi, valid, wg, wu, wd, li):
    t_pad, w = hs.shape
    d = w - LANE
    dff = wg.shape[-1]
    up = lambda sel: pl.BlockSpec((None, None, d, dff), lambda j, lo, hi, v: (li, sel(lo, hi)[j], 0, 0))
    dn = lambda sel: pl.BlockSpec((None, None, dff, d), lambda j, lo, hi, v: (li, sel(lo, hi)[j], 0, 0))
    first = lambda lo, hi: lo
    second = lambda lo, hi: hi
    return pl.pallas_call(
        _gmm_kernel,
        grid_spec=pltpu.PrefetchScalarGridSpec(
            num_scalar_prefetch=3,
            grid=(t_pad // TME,),
            in_specs=[pl.BlockSpec((TME, w), lambda j, lo, hi, v: (j, 0)),
                      up(first), up(first), dn(first), up(second), up(second), dn(second)],
            out_specs=pl.BlockSpec((TME, d), lambda j, lo, hi, v: (j, 0))),
        out_shape=jax.ShapeDtypeStruct((t_pad, d), F32),
        compiler_params=_cparams(("arbitrary",)),
        name="moe_grouped_ffn",
    )(e_lo, e_hi, valid, hs, wg, wu, wd, wg, wu, wd)


def _route_tables(rt, cnt, t_pad):
    b, _, s = rt.shape
    nb = N_GROUPS * len(PAIRS)
    bucket = rt[:, 0, :].astype(jnp.int32)
    rank = rt[:, 1, :].astype(jnp.int32)
    counts = cnt[:nb, 0].astype(jnp.int32)
    padded = ((counts + TME - 1) // TME) * TME
    ids = jnp.arange(nb, dtype=jnp.int32)
    ends = jnp.sum(jnp.where(ids[None, :] <= ids[:, None], padded[None, :], 0), axis=1)
    starts = ends - padded
    start_of = jnp.sum(jnp.where(bucket[..., None] == ids, starts, 0), axis=-1)
    pos = (start_of + rank).reshape(b * s // TM, 1, TM)
    n_tiles = t_pad // TME
    n_used = ends[-1] // TME
    tile = jnp.arange(n_tiles, dtype=jnp.int32)
    valid = (tile < n_used).astype(jnp.int32)
    tile_c = jnp.minimum(tile, n_used - 1)
    tb = jnp.minimum(jnp.sum((ends[None, :] <= tile_c[:, None] * TME).astype(jnp.int32), axis=1), nb - 1)
    e_lo_of = jnp.asarray([g * EXPERTS_PER_GROUP + p[0] for g in range(N_GROUPS) for p in PAIRS], jnp.int32)
    e_hi_of = jnp.asarray([g * EXPERTS_PER_GROUP + p[1] for g in range(N_GROUPS) for p in PAIRS], jnp.int32)
    hit = tb[:, None] == ids
    e_lo = jnp.sum(jnp.where(hit, e_lo_of, 0), axis=1)
    e_hi = jnp.sum(jnp.where(hit, e_hi_of, 0), axis=1)
    return pos, e_lo, e_hi, valid


def _issue_gather(pos_ref, y_ref, buf_ref, sem, slot):
    def body(r2, c):
        for prio in range(DMA_QUEUES):
            r = r2 * DMA_QUEUES + prio
            p = pos_ref[0, r]
            pltpu.make_async_copy(y_ref.at[pl.ds(p, 1), :], buf_ref.at[slot, pl.ds(r, 1), :],
                                  sem.at[slot]).start(priority=prio)
        return c

    lax.fori_loop(0, TM // DMA_QUEUES, body, 0, unroll=4)


def _gather_tile(pos_ref, posn_ref, y_ref, buf_ref, sem):
    ni = pl.num_programs(1)
    lin = pl.program_id(0) * ni + pl.program_id(1)
    slot = lin % 2

    @pl.when(lin == 0)
    def _():
        _issue_gather(pos_ref, y_ref, buf_ref, sem, slot)

    @pl.when(lin + 1 < pl.num_programs(0) * ni)
    def _():
        _issue_gather(posn_ref, y_ref, buf_ref, sem, 1 - slot)

    pltpu.make_async_copy(y_ref.at[pl.ds(0, TM), :], buf_ref.at[slot], sem.at[slot]).wait()
    return buf_ref[slot]


def _combine_kernel(pos_ref, posn_ref, x_ref, mod_ref, y_ref, xo_ref, buf_ref, sem):
    f = _gather_tile(pos_ref, posn_ref, y_ref, buf_ref, sem)
    mod = mod_ref[...]
    xo_ref[...] = x_ref[...] + mod[5:6] * f


def _final_kernel(pos_ref, posn_ref, x_ref, mod_ref, g_ref, y_ref, o_ref, buf_ref, sem):
    f = _gather_tile(pos_ref, posn_ref, y_ref, buf_ref, sem)
    mod = mod_ref[...]
    x = x_ref[...] + mod[5:6] * f
    var = jnp.mean(x * x, axis=-1, keepdims=True)
    o_ref[...] = (x * lax.rsqrt(var + EPS)) * g_ref[...]


def _pos_specs(nt, ni, nb):
    def cur(bb, i):
        return (bb * nt + i, 0, 0)

    def nxt(bb, i):
        wrap = i + 1 == ni
        b2 = jnp.minimum(jnp.where(wrap, bb + 1, bb), nb - 1)
        i2 = jnp.where(wrap, jnp.where(bb + 1 == nb, i, 0), i + 1)
        return (b2 * nt + i2, 0, 0)

    return [pl.BlockSpec((None, 1, TM), cur, memory_space=pltpu.SMEM),
            pl.BlockSpec((None, 1, TM), nxt, memory_space=pltpu.SMEM)]


def _combine(xa, ys, pos3, mods, li):
    b, s, d = xa.shape
    nt = s // TM
    nl = nt - 1
    tok = pl.BlockSpec((None, TM, d), lambda bb, i: (bb, i, 0))
    return pl.pallas_call(
        _combine_kernel,
        grid=(b, nt),
        in_specs=_pos_specs(nt, nt, b) + [
            tok,
            pl.BlockSpec((None, None, None, 6, d), lambda bb, i: (li, bb, i // nl, 0, 0)),
            pl.BlockSpec(memory_space=pl.ANY)],
        out_specs=tok,
        out_shape=jax.ShapeDtypeStruct((b, s, d), F32),
        scratch_shapes=[pltpu.VMEM((2, TM, d), F32), pltpu.SemaphoreType.DMA((2,))],
        input_output_aliases={2: 0},
        compiler_params=_cparams(("arbitrary", "arbitrary")),
        name="moe_gather_residual",
    )(pos3, pos3, xa, mods, ys)


def _final(xa, ys, pos3, mods, g_final, li, n_lat):
    b, s, d = xa.shape
    nt = s // TM
    nl = n_lat // TM
    tok = pl.BlockSpec((None, TM, d), lambda bb, i: (bb, i, 0))
    return pl.pallas_call(
        _final_kernel,
        grid=(b, nl),
        in_specs=_pos_specs(nt, nl, b) + [
            tok,
            pl.BlockSpec((None, None, None, 6, d), lambda bb, i: (li, bb, 0, 0, 0)),
            pl.BlockSpec((1, d), lambda bb, i: (0, 0)),
            pl.BlockSpec(memory_space=pl.ANY)],
        out_specs=tok,
        out_shape=jax.ShapeDtypeStruct((b, n_lat, d), F32),
        scratch_shapes=[pltpu.VMEM((2, TM, d), F32), pltpu.SemaphoreType.DMA((2,))],
        compiler_params=_cparams(("arbitrary", "arbitrary")),
        name="moe_gather_residual_final_norm",
    )(pos3, pos3, xa, mods, g_final, ys)


def _rope_tables(n_lat, n_ctx, rot_dim, lane_off):
    rows = n_lat // GRID_W
    r = jnp.repeat(jnp.arange(rows, dtype=F32), GRID_W)[:n_lat]
    c = jnp.tile(jnp.arange(GRID_W, dtype=F32), rows)[:n_lat]
    axis_dim = rot_dim // 2
    inv_freq = ROPE_BASE ** (-jnp.arange(0, axis_dim, 2, dtype=F32) / axis_dim)
    ang_r = r[:, None] * inv_freq
    ang_c = c[:, None] * inv_freq
    ang = jnp.concatenate([ang_r, ang_r, ang_c, ang_c], axis=-1)
    cos, sin = jnp.cos(ang), jnp.sin(ang)
    grp = (np.arange(rot_dim) // (rot_dim // 4)) % 2
    sin_a = jnp.where(grp == 0, -sin, 0.0)
    sin_b = jnp.where(grp == 1, sin, 0.0)

    def place(t, fill):
        reps = 2 if rot_dim == 64 else 1
        blk = jnp.full((n_lat, LANE), fill, F32)
        for k in range(reps):
            blk = blk.at[:, lane_off + k * rot_dim:lane_off + (k + 1) * rot_dim].set(t)
        ctx = jnp.full((n_ctx, LANE), fill, F32)
        return jnp.concatenate([blk, ctx], axis=0)

    return place(cos, 1.0), place(sin_a, 0.0), place(sin_b, 0.0)


def _dft_mats(n, scale):
    r = 1
    while r * r < n:
        r *= 2
    k = jnp.arange(n, dtype=jnp.int32)[:, None]
    a = jnp.arange(n // r, dtype=jnp.int32)[None, :]
    c = jnp.arange(r, dtype=jnp.int32)[None, :]
    ang_a = ((k * a * r) % n).astype(F32) * (2.0 * np.pi / n)
    ang_c = ((k * c) % n).astype(F32) * (2.0 * np.pi / n)
    ca, sa = (jnp.cos(ang_a) * scale)[:, :, None], (jnp.sin(ang_a) * scale)[:, :, None]
    cc, sc = jnp.cos(ang_c)[:, None, :], jnp.sin(ang_c)[:, None, :]
    cos = (ca * cc - sa * sc).reshape(n, n)
    sin = (sa * cc + ca * sc).reshape(n, n)
    return cos.astype(BF16), sin.astype(BF16)


def _even_w_in(w):
    d = w.shape[0]
    c3 = 3 * CONV_DIM
    qcols = []
    per_kv = N_Q_HEADS // N_KV_HEADS
    for h in range(N_Q_HEADS):
        wq = w[:, c3 + h * HEAD_DIM:c3 + (h + 1) * HEAD_DIM]
        z = jnp.zeros((d, HEAD_DIM), w.dtype)
        qcols.append(jnp.concatenate([wq, z] if h // per_kv == 0 else [z, wq], axis=1))
    return jnp.concatenate([w[:, :c3]] + qcols + [w[:, c3 + N_Q_HEADS * HEAD_DIM:]], axis=1).astype(BF16)


def _odd_w_in(w):
    d = w.shape[0]
    o = Q_LORA + KV_LORA
    kr = jnp.concatenate([jnp.zeros((d, QK_NOPE), w.dtype), w[:, o:o + QK_ROPE],
                          jnp.zeros((d, LANE - QK_NOPE - QK_ROPE), w.dtype)], axis=1)
    return jnp.concatenate([w[:, :o], kr, w[:, o + QK_ROPE:]], axis=1).astype(BF16)


def _pad_heads(w, per_head, keep_from, keep, heads):
    rows = w.shape[0]
    w3 = w.reshape(rows, heads, per_head)[:, :, keep_from:keep_from + keep]
    w3 = jnp.pad(w3, ((0, 0), (0, 0), (0, LANE - keep)))
    return w3.reshape(rows, heads * LANE).astype(BF16)


def _channel_dft():
    gd = FOURIER_DIM // FOURIER_GROUPS
    k = np.arange(gd)
    ang = 2.0 * np.pi * ((k[:, None] * k[None, :]) % gd) / gd
    eye = np.eye(FOURIER_GROUPS)
    wc = np.kron(eye, np.cos(ang))
    ws = np.kron(eye, np.sin(ang))
    return jnp.asarray(np.concatenate([wc, ws], axis=1), BF16)


def kernel(x, c, ctx, c_ctx, ada_w, ada_b, norm_mix_g, norm_ffn_g, ev_w_in, ev_conv_w, ev_sink, ev_w_out,
           od_w_in, od_q_norm_g, od_w_uq, od_kv_norm_g, od_w_ukv, od_w_out, router_w, router_bias,
           ex_w_gate, ex_w_up, ex_w_down, final_norm_g):
    b, n, d = x.shape
    l = ctx.shape[1]
    s = n + l
    depth = ada_w.shape[0]
    assert l == TM and n % 512 == 0

    rows = 8 * ((b + 1 + 7) // 8)
    cvec = jnp.concatenate([c, c_ctx[None, :], jnp.zeros((rows - b - 1, d), F32)], axis=0)
    mod_all = _modulation(cvec, ada_w, ada_b).reshape(depth, rows, 6, d)
    mods = jnp.stack([mod_all[:, :b], jnp.broadcast_to(mod_all[:, b:b + 1], (depth, b, 6, d))], axis=2)

    tabs_even = _rope_tables(n, l, HEAD_DIM, 0)
    tabs_odd = _rope_tables(n, l, QK_ROPE, QK_NOPE)
    gd = FOURIER_DIM // FOURIER_GROUPS
    dft_lat = _dft_mats(n, (n * gd) ** -0.5)
    dft_ctx = _dft_mats(l, (l * gd) ** -0.5)
    wf = _channel_dft()

    g_mix = norm_mix_g.reshape(depth, 1, d)
    g_ffn = norm_ffn_g.reshape(depth, 1, d)
    rw_t = router_w.T.astype(BF16)
    rb = router_bias.reshape(N_EXPERTS, 1).astype(F32)
    wg, wu, wd = ex_w_gate.astype(BF16), ex_w_up.astype(BF16), ex_w_down.astype(BF16)

    xa = jnp.concatenate([x, ctx], axis=1)
    t_pad = (-(-(b * s) // TME) + N_GROUPS * len(PAIRS)) * TME
    hs = jnp.zeros((t_pad, d + LANE), F32)
    out = None
    for li in range(depth):
        j = li // 2
        if li % 2 == 0:
            u, ab, q, k, v = _pre_even(xa, mods, g_mix, _even_w_in(ev_w_in[j]), tabs_even, li)
            ya, yb = _even_mix(u, ab, q, k, v, ev_conv_w[j], ev_sink[j], n)
            w_out = ev_w_out[j].astype(BF16)
        else:
            q, k, v, ycs = _pre_odd(
                xa, mods, g_mix, _odd_w_in(od_w_in[j]),
                od_q_norm_g[j].reshape(1, Q_LORA),
                _pad_heads(od_w_uq[j], QK_NOPE + QK_ROPE, 0, QK_NOPE + QK_ROPE, MLA_HEADS),
                od_kv_norm_g[j].reshape(1, KV_LORA),
                _pad_heads(od_w_ukv[j], QK_NOPE + V_HEAD, 0, QK_NOPE, MLA_HEADS),
                _pad_heads(od_w_ukv[j], QK_NOPE + V_HEAD, QK_NOPE, V_HEAD, MLA_HEADS),
                wf, tabs_odd, li)
            ya = _mla(q, k, v, n)
            yb = _fourier(ycs, dft_lat, dft_ctx, n)
            w_out = od_w_out[j].astype(BF16)
        half = w_out.shape[0] // 2
        xa, hx, rt, cnt = _post(xa, ya, yb, mods, w_out[:half], w_out[half:], g_ffn, rw_t, rb, li)
        pos3, e_lo, e_hi, valid = _route_tables(rt, cnt, t_pad)
        hs = _scatter_rows(hx.reshape(b * s, d + LANE), pos3, hs)
        ys = _gmm(hs, e_lo, e_hi, valid, wg, wu, wd, li)
        if li == depth - 1:
            out = _final(xa, ys, pos3, mods, final_norm_g.reshape(1, d), li, n)
        else:
            xa = _combine(xa, ys, pos3, mods, li)
    return out
```

```python
import functools

import jax
import jax.numpy as jnp
import numpy as np
from jax import lax
from jax.experimental import pallas as pl
from jax.experimental.pallas import tpu as pltpu

F32 = jnp.float32
BF16 = jnp.bfloat16

EPS = 1e-6
ROPE_BASE = 10000.0
GRID_W = 64
NEG_INF = -1e30
LOG2E = 1.4426950408889634
LANE = 128
TM = 256
WIN = 128
N_Q_HEADS = 8
N_KV_HEADS = 2
HEAD_DIM = 64
CONV_DIM = 512
MLA_HEADS = 8
Q_LORA = 256
KV_LORA = 128
QK_NOPE = 64
QK_ROPE = 32
V_HEAD = 64
FOURIER_DIM = 512
FOURIER_GROUPS = 4
N_EXPERTS = 16
N_GROUPS = 4
EXPERTS_PER_GROUP = 4
PAIRS = ((0, 1), (0, 2), (0, 3), (1, 2), (1, 3), (2, 3))
NB_PAD = 32
TME = 512
DMA_QUEUES = 2
TQ_MLA = 1024
VMEM_LIMIT = 56 * 1024 * 1024


def _cparams(sem):
    return pltpu.CompilerParams(dimension_semantics=sem, vmem_limit_bytes=VMEM_LIMIT)


def _norm_mod(x, g, shift, scale):
    var = jnp.mean(x * x, axis=-1, keepdims=True)
    return (x * lax.rsqrt(var + EPS)) * g * (1.0 + scale) + shift


def _rope(x, cos, sin_a, sin_b, shift):
    w = x.shape[-1]
    return x * cos + pltpu.roll(x, w - shift, 1) * sin_a + pltpu.roll(x, shift, 1) * sin_b


def _mod_kernel(c_ref, w_ref, b_ref, o_ref):
    c = c_ref[...]
    o_ref[...] = jnp.dot(jax.nn.silu(c), w_ref[...], preferred_element_type=F32,
                         precision=lax.Precision.HIGHEST) + b_ref[...]


def _modulation(cvec, ada_w, ada_b):
    depth, d, d6 = ada_w.shape
    tn = d6 // 4
    rows = cvec.shape[0]
    return pl.pallas_call(
        _mod_kernel,
        grid=(depth, d6 // tn),
        in_specs=[pl.BlockSpec((rows, d), lambda l, j: (0, 0)),
                  pl.BlockSpec((None, d, tn), lambda l, j: (l, 0, j)),
                  pl.BlockSpec((None, 1, tn), lambda l, j: (l, 0, j))],
        out_specs=pl.BlockSpec((None, rows, tn), lambda l, j: (l, 0, j)),
        out_shape=jax.ShapeDtypeStruct((depth, rows, d6), F32),
        compiler_params=_cparams(("arbitrary", "arbitrary")),
        name="adaln_modulation",
    )(cvec, ada_w, ada_b.reshape(depth, 1, d6))


def _pre_even_kernel(x_ref, mod_ref, g_ref, w_ref, cos_ref, sa_ref, sb_ref,
                     u_ref, ab_ref, q_ref, k_ref, v_ref):
    mod = mod_ref[...]
    h = _norm_mod(x_ref[...], g_ref[...], mod[0:1], mod[1:2])
    p = jnp.dot(h.astype(BF16), w_ref[...], preferred_element_type=F32)
    c = CONV_DIM
    ax, ab, ac = p[:, 0:c], p[:, c:2 * c], p[:, 2 * c:3 * c]
    qw = N_Q_HEADS * LANE
    q = p[:, 3 * c:3 * c + qw]
    k = p[:, 3 * c + qw:3 * c + qw + LANE]
    v = p[:, 3 * c + qw + LANE:3 * c + qw + 2 * LANE]
    cos, sa, sb = cos_ref[...], sa_ref[...], sb_ref[...]
    cos8 = jnp.concatenate([cos] * N_Q_HEADS, axis=1)
    sa8 = jnp.concatenate([sa] * N_Q_HEADS, axis=1)
    sb8 = jnp.concatenate([sb] * N_Q_HEADS, axis=1)
    u_ref[...] = (ac * ax).astype(BF16)
    ab_ref[...] = ab.astype(BF16)
    q_ref[...] = (_rope(q, cos8, sa8, sb8, 16) * (HEAD_DIM ** -0.5 * LOG2E)).astype(BF16)
    k_ref[...] = _rope(k, cos, sa, sb, 16).astype(BF16)
    v_ref[...] = v.astype(BF16)


def _pre_even(xa, mods, g_mix, w_in, tabs, li):
    b, s, d = xa.shape
    nt = s // TM
    nl = nt - 1
    wcols = w_in.shape[1]
    qw = N_Q_HEADS * LANE
    tab_spec = pl.BlockSpec((TM, LANE), lambda i, bb: (i, 0))
    tok = lambda w: pl.BlockSpec((None, TM, w), lambda i, bb: (bb, i, 0))
    return pl.pallas_call(
        _pre_even_kernel,
        grid=(nt, b),
        in_specs=[tok(d),
                  pl.BlockSpec((None, None, None, 6, d), lambda i, bb: (li, bb, i // nl, 0, 0)),
                  pl.BlockSpec((None, 1, d), lambda i, bb: (li, 0, 0)),
                  pl.BlockSpec((d, wcols), lambda i, bb: (0, 0)),
                  tab_spec, tab_spec, tab_spec],
        out_specs=[tok(CONV_DIM), tok(CONV_DIM), tok(qw), tok(LANE), tok(LANE)],
        out_shape=[jax.ShapeDtypeStruct((b, s, CONV_DIM), BF16),
                   jax.ShapeDtypeStruct((b, s, CONV_DIM), BF16),
                   jax.ShapeDtypeStruct((b, s, qw), BF16),
                   jax.ShapeDtypeStruct((b, s, LANE), BF16),
                   jax.ShapeDtypeStruct((b, s, LANE), BF16)],
        compiler_params=_cparams(("arbitrary", "arbitrary")),
        name="even_in_proj",
    )(xa, mods, g_mix, w_in, *tabs)


def _even_mix_kernel(sink_ref, q_ref, k_ref, v_ref,
                     u_ref, up_ref, un_ref, ab_ref, cw_ref, yc_ref, ya_ref, *, nl, n_lat):
    i = pl.program_id(1)
    nt = pl.num_programs(1)
    is_lat = i < nl
    u = u_ref[...].astype(F32)
    row = lax.broadcasted_iota(jnp.int32, u.shape, 0)
    first = jnp.logical_or(i == 0, i == nl)
    last = jnp.logical_or(i == nl - 1, i == nt - 1)
    prev_row = jnp.where(first, 0.0, up_ref[15:16, :].astype(F32))
    next_row = jnp.where(last, 0.0, un_ref[0:1, :].astype(F32))
    u_prev = jnp.where(row == 0, prev_row, pltpu.roll(u, 1, 0))
    u_next = jnp.where(row == TM - 1, next_row, pltpu.roll(u, TM - 1, 0))
    cw = cw_ref[...]
    conv = u_prev * cw[0:1] + u * cw[1:2] + u_next * cw[2:3]
    yc_ref[...] = (ab_ref[...].astype(F32) * conv).astype(BF16)

    s_all = k_ref.shape[0]
    nwin = TM + 2 * WIN
    w0 = pl.multiple_of(jnp.clip(i * TM - WIN, 0, s_all - nwin), WIN)
    k_all = jnp.concatenate([k_ref[pl.ds(w0, nwin), :], k_ref[n_lat:, :]], axis=0)
    v_all = jnp.concatenate([v_ref[pl.ds(w0, nwin), :], v_ref[n_lat:, :]], axis=0)
    nk = k_all.shape[0]
    qpos = i * TM + lax.broadcasted_iota(jnp.int32, (TM, nk), 0)
    kj = lax.broadcasted_iota(jnp.int32, (TM, nk), 1)
    kpos = w0 + kj
    valid_win = (jnp.abs(kpos - qpos) <= WIN) & (kpos < n_lat) & is_lat
    bias = jnp.where((kj >= nwin) | valid_win, 0.0, NEG_INF).astype(F32)
    lane = lax.broadcasted_iota(jnp.int32, (TM, LANE), 1)
    outs = []
    for pair in range(N_Q_HEADS // 2):
        kvh = (2 * pair) // (N_Q_HEADS // N_KV_HEADS)
        o_pair = []
        for h in (2 * pair, 2 * pair + 1):
            q = q_ref[:, h * LANE:(h + 1) * LANE]
            s = lax.dot_general(q, k_all, (((1,), (1,)), ((), ())), preferred_element_type=F32) + bias
            sink = sink_ref[h] * LOG2E
            m = jnp.maximum(jnp.max(s, axis=1, keepdims=True), sink)
            p = jnp.exp2(s - m)
            l = jnp.sum(p, axis=1, keepdims=True) + jnp.exp2(sink - m)
            o = jnp.dot(p.astype(BF16), v_all, preferred_element_type=F32) / l
            if kvh == 1:
                o = pltpu.roll(o, HEAD_DIM, 1)
            o_pair.append(o)
        outs.append(jnp.where(lane < HEAD_DIM, o_pair[0], pltpu.roll(o_pair[1], HEAD_DIM, 1)))
    ya_ref[...] = jnp.concatenate(outs, axis=1).astype(BF16)


def _even_mix(u, ab, q, k, v, conv_w, sink, n_lat):
    b, s, _ = u.shape
    nt = s // TM
    nl = n_lat // TM
    qw = q.shape[-1]
    whole = pl.BlockSpec((None, s, LANE), lambda bb, i: (bb, 0, 0))
    halo = TM // 16
    u16 = u.reshape(b, s // 16, 16, CONV_DIM)
    kern = functools.partial(_even_mix_kernel, nl=nl, n_lat=n_lat)
    tok = lambda w: pl.BlockSpec((None, TM, w), lambda bb, i: (bb, i, 0))
    return pl.pallas_call(
        kern,
        grid=(b, nt),
        in_specs=[pl.BlockSpec(memory_space=pltpu.SMEM),
                  tok(qw), whole, whole,
                  tok(CONV_DIM),
                  pl.BlockSpec((None, None, 16, CONV_DIM),
                               lambda bb, i: (bb, jnp.maximum(i * halo - 1, 0), 0, 0)),
                  pl.BlockSpec((None, None, 16, CONV_DIM),
                               lambda bb, i: (bb, jnp.minimum((i + 1) * halo, s // 16 - 1), 0, 0)),
                  tok(CONV_DIM),
                  pl.BlockSpec((3, CONV_DIM), lambda bb, i: (0, 0))],
        out_specs=[tok(CONV_DIM), tok(N_Q_HEADS * HEAD_DIM)],
        out_shape=[jax.ShapeDtypeStruct((b, s, CONV_DIM), BF16),
                   jax.ShapeDtypeStruct((b, s, N_Q_HEADS * HEAD_DIM), BF16)],
        compiler_params=_cparams(("arbitrary", "arbitrary")),
        name="even_conv_window_attn",
    )(sink, q, k, v, u, u16, u16, ab, conv_w)


def _pre_odd_kernel(x_ref, mod_ref, g_ref, w_ref, gq_ref, wuq_ref, gkv_ref, wuk_ref, wuv_ref,
                    wf_ref, cos_ref, sa_ref, sb_ref, q_ref, k_ref, v_ref, y_ref):
    mod = mod_ref[...]
    h = _norm_mod(x_ref[...], g_ref[...], mod[0:1], mod[1:2])
    p = jnp.dot(h.astype(BF16), w_ref[...], preferred_element_type=F32)
    cq = p[:, 0:Q_LORA]
    ckv = p[:, Q_LORA:Q_LORA + KV_LORA]
    kr = p[:, Q_LORA + KV_LORA:Q_LORA + KV_LORA + LANE]
    f = p[:, Q_LORA + KV_LORA + LANE:]
    cos, sa, sb = cos_ref[...], sa_ref[...], sb_ref[...]
    cos8 = jnp.concatenate([cos] * MLA_HEADS, axis=1)
    sa8 = jnp.concatenate([sa] * MLA_HEADS, axis=1)
    sb8 = jnp.concatenate([sb] * MLA_HEADS, axis=1)

    cqn = cq * lax.rsqrt(jnp.mean(cq * cq, axis=-1, keepdims=True) + EPS) * gq_ref[...]
    q = jnp.dot(cqn.astype(BF16), wuq_ref[...], preferred_element_type=F32)
    q = _rope(q, cos8, sa8, sb8, 8) * ((QK_NOPE + QK_ROPE) ** -0.5 * LOG2E)
    q_ref[...] = q.astype(BF16)

    ckvn = (ckv * lax.rsqrt(jnp.mean(ckv * ckv, axis=-1, keepdims=True) + EPS) * gkv_ref[...]).astype(BF16)
    kn = jnp.dot(ckvn, wuk_ref[...], preferred_element_type=F32)
    krr = _rope(kr, cos, sa, sb, 8)
    k_ref[...] = (kn + jnp.concatenate([krr] * MLA_HEADS, axis=1)).astype(BF16)
    v = jnp.dot(ckvn, wuv_ref[...], preferred_element_type=F32)
    one_lane = lax.broadcasted_iota(jnp.int32, v.shape, 1) % LANE == V_HEAD
    v_ref[...] = jnp.where(one_lane, 1.0, v).astype(BF16)
    y_ref[...] = jnp.dot(f.astype(BF16), wf_ref[...], preferred_element_type=F32).astype(BF16)


def _pre_odd(xa, mods, g_mix, w_in, gq, wuq, gkv, wuk, wuv, wf, tabs, li):
    b, s, d = xa.shape
    nt = s // TM
    nl = nt - 1
    hw = MLA_HEADS * LANE
    vw = hw
    tab_spec = pl.BlockSpec((TM, LANE), lambda i, bb: (i, 0))
    tok = lambda w: pl.BlockSpec((None, TM, w), lambda i, bb: (bb, i, 0))
    full = lambda a: pl.BlockSpec(a.shape, lambda i, bb: (0,) * a.ndim)
    return pl.pallas_call(
        _pre_odd_kernel,
        grid=(nt, b),
        in_specs=[tok(d),
                  pl.BlockSpec((None, None, None, 6, d), lambda i, bb: (li, bb, i // nl, 0, 0)),
                  pl.BlockSpec((None, 1, d), lambda i, bb: (li, 0, 0)),
                  full(w_in), full(gq), full(wuq), full(gkv), full(wuk), full(wuv), full(wf),
                  tab_spec, tab_spec, tab_spec],
        out_specs=[tok(hw), tok(hw), tok(vw), tok(2 * FOURIER_DIM)],
        out_shape=[jax.ShapeDtypeStruct((b, s, hw), BF16),
                   jax.ShapeDtypeStruct((b, s, hw), BF16),
                   jax.ShapeDtypeStruct((b, s, vw), BF16),
                   jax.ShapeDtypeStruct((b, s, 2 * FOURIER_DIM), BF16)],
        compiler_params=_cparams(("arbitrary", "arbitrary")),
        name="odd_in_proj",
    )(xa, mods, g_mix, w_in, gq, wuq, gkv, wuk, wuv, wf, *tabs)


def _mla_kernel(q_ref, k_ref, v_ref, o_ref):
    lane = lax.broadcasted_iota(jnp.int32, (q_ref.shape[0], LANE), 1)
    outs = []
    for pair in range(MLA_HEADS // 2):
        o_pair = []
        for h in (2 * pair, 2 * pair + 1):
            q = q_ref[:, h * LANE:(h + 1) * LANE]
            k = k_ref[:, h * LANE:(h + 1) * LANE]
            v = v_ref[:, h * LANE:(h + 1) * LANE]
            s = lax.dot_general(q, k, (((1,), (1,)), ((), ())), preferred_element_type=F32)
            m = jnp.max(s, axis=1, keepdims=True)
            p = jnp.exp2(s - m)
            o = jnp.dot(p.astype(BF16), v, preferred_element_type=F32)
            o_pair.append(o / o[:, V_HEAD:V_HEAD + 1])
        outs.append(jnp.where(lane < V_HEAD, o_pair[0], pltpu.roll(o_pair[1], V_HEAD, 1)))
    o_ref[...] = jnp.concatenate(outs, axis=1).astype(BF16)


def _mla(q, k, v, n_lat):
    b, s, hw = q.shape
    vw = MLA_HEADS * V_HEAD
    n_ctx = s - n_lat
    tq = TQ_MLA if n_lat % TQ_MLA == 0 else TM
    once = pl.Buffered(1)
    lat = pl.pallas_call(
        _mla_kernel,
        grid=(b, n_lat // tq),
        in_specs=[pl.BlockSpec((None, tq, hw), lambda bb, i: (bb, i, 0)),
                  pl.BlockSpec((None, s, hw), lambda bb, i: (bb, 0, 0), pipeline_mode=once),
                  pl.BlockSpec((None, s, hw), lambda bb, i: (bb, 0, 0), pipeline_mode=once)],
        out_specs=pl.BlockSpec((None, tq, vw), lambda bb, i: (bb, i, 0)),
        out_shape=jax.ShapeDtypeStruct((b, n_lat, vw), BF16),
        compiler_params=_cparams(("arbitrary", "arbitrary")),
        name="mla_attention",
    )(q, k, v)
    cblk = n_lat // n_ctx
    ctx_spec = pl.BlockSpec((None, n_ctx, hw), lambda bb: (bb, cblk, 0))
    ctx = pl.pallas_call(
        _mla_kernel,
        grid=(b,),
        in_specs=[ctx_spec, ctx_spec, ctx_spec],
        out_specs=pl.BlockSpec((None, n_ctx, vw), lambda bb: (bb, 0, 0)),
        out_shape=jax.ShapeDtypeStruct((b, n_ctx, vw), BF16),
        compiler_params=_cparams(("arbitrary",)),
        name="mla_attention_context",
    )(q, k, v)
    return jnp.concatenate([lat, ctx], axis=1)


def _dft_kernel(ac_ref, as_ref, cc_ref, cs_ref, yc_ref, ys_ref, ycc_ref, ysc_ref, o_ref, *, nl):
    i = pl.program_id(1)

    @pl.when(i < nl)
    def _():
        o_ref[...] = (jnp.dot(ac_ref[...], yc_ref[...], preferred_element_type=F32)
                      - jnp.dot(as_ref[...], ys_ref[...], preferred_element_type=F32)).astype(BF16)

    @pl.when(i >= nl)
    def _():
        o_ref[...] = (jnp.dot(cc_ref[...], ycc_ref[...], preferred_element_type=F32)
                      - jnp.dot(cs_ref[...], ysc_ref[...], preferred_element_type=F32)).astype(BF16)


def _fourier(ycs, dft_lat, dft_ctx, n_lat):
    b, s, _ = ycs.shape
    fd = FOURIER_DIM
    n_ctx = s - n_lat
    nl = n_lat // TM
    ac, as_ = dft_lat
    cc, cs = dft_ctx
    rows = pl.BlockSpec((TM, n_lat), lambda bb, i: (jnp.minimum(i, nl - 1), 0))
    small = pl.BlockSpec((n_ctx, n_ctx), lambda bb, i: (0, 0))
    return pl.pallas_call(
        functools.partial(_dft_kernel, nl=nl),
        grid=(b, s // TM),
        in_specs=[rows, rows, small, small,
                  pl.BlockSpec((None, n_lat, fd), lambda bb, i: (bb, 0, 0)),
                  pl.BlockSpec((None, n_lat, fd), lambda bb, i: (bb, 0, 1)),
                  pl.BlockSpec((None, n_ctx, fd), lambda bb, i: (bb, n_lat // n_ctx, 0)),
                  pl.BlockSpec((None, n_ctx, fd), lambda bb, i: (bb, n_lat // n_ctx, 1))],
        out_specs=pl.BlockSpec((None, TM, fd), lambda bb, i: (bb, i, 0)),
        out_shape=jax.ShapeDtypeStruct((b, s, fd), BF16),
        compiler_params=_cparams(("arbitrary", "arbitrary")),
        name="fourier_mix",
    )(ac, as_, cc, cs, ycs, ycs, ycs, ycs)


def _post_kernel(x_ref, ya_ref, yb_ref, mod_ref, wa_ref, wb_ref, g_ref, rw_ref, rb_ref,
                 xo_ref, hx_ref, rt_ref, cnt_ref, carry_ref):
    @pl.when((pl.program_id(0) == 0) & (pl.program_id(1) == 0))
    def _():
        carry_ref[...] = jnp.zeros_like(carry_ref)

    mod = mod_ref[...]
    y = (jnp.dot(ya_ref[...], wa_ref[...], preferred_element_type=F32)
         + jnp.dot(yb_ref[...], wb_ref[...], preferred_element_type=F32))
    x = x_ref[...] + mod[2:3] * y
    xo_ref[...] = x
    h2f = _norm_mod(x, g_ref[...], mod[3:4], mod[4:5])
    h2 = h2f.astype(BF16)
    logits = lax.dot_general(rw_ref[...], h2, (((1,), (1,)), ((), ())), preferred_element_type=F32)
    scores = jax.nn.sigmoid(logits)
    biased = scores + rb_ref[...]
    sc = [scores[e:e + 1, :] for e in range(N_EXPERTS)]
    bs = [biased[e:e + 1, :] for e in range(N_EXPERTS)]
    sel, gval = [], []
    for g in range(N_GROUPS):
        ids = range(g * EXPERTS_PER_GROUP, (g + 1) * EXPERTS_PER_GROUP)
        gsum = None
        for e in ids:
            rank = sum(((bs[j] > bs[e]) if j > e else (bs[j] >= bs[e])).astype(jnp.int32)
                       for j in ids if j != e)
            s_e = rank < 2
            sel.append(s_e)
            term = jnp.where(s_e, bs[e], 0.0)
            gsum = term if gsum is None else gsum + term
        gval.append(gsum)
    gsel = []
    for g in range(N_GROUPS):
        ok = None
        for j in range(N_GROUPS):
            if j == g:
                continue
            c = (gval[g] > gval[j]) if j < g else (gval[g] >= gval[j])
            ok = c if ok is None else (ok & c)
        gsel.append(ok)
    chosen = [sel[e] & gsel[e // EXPERTS_PER_GROUP] for e in range(N_EXPERTS)]
    wsum = sum(jnp.where(chosen[e], sc[e], 0.0) for e in range(N_EXPERTS))
    gates = [jnp.where(chosen[e], sc[e], 0.0) / wsum for e in range(N_EXPERTS)]
    bucket = jnp.zeros_like(wsum)
    w_lo = jnp.zeros_like(wsum)
    w_hi = jnp.zeros_like(wsum)
    for g in range(N_GROUPS):
        for pi, (ea, eb) in enumerate(PAIRS):
            ea, eb = g * EXPERTS_PER_GROUP + ea, g * EXPERTS_PER_GROUP + eb
            cond = chosen[ea] & chosen[eb]
            bucket = jnp.where(cond, float(g * len(PAIRS) + pi), bucket)
            w_lo = jnp.where(cond, gates[ea], w_lo)
            w_hi = jnp.where(cond, gates[eb], w_hi)
    kio = lax.broadcasted_iota(jnp.int32, (NB_PAD, TM), 0).astype(F32)
    onehot = (kio == bucket).astype(F32)
    tri = (lax.broadcasted_iota(jnp.int32, (TM, TM), 0)
           <= lax.broadcasted_iota(jnp.int32, (TM, TM), 1)).astype(BF16)
    prefix = jnp.dot(onehot.astype(BF16), tri, preferred_element_type=F32)
    carry = carry_ref[...]
    rank = jnp.sum(onehot * (prefix + carry[:, 0:1]), axis=0, keepdims=True) - 1.0
    carry = carry + jnp.sum(onehot, axis=1, keepdims=True)
    carry_ref[...] = carry
    cnt_ref[...] = carry
    zrow = jnp.zeros_like(wsum)
    rt_ref[...] = jnp.concatenate([bucket, rank] + [zrow] * 6, axis=0)
    ext_t = jnp.concatenate([w_lo, w_hi] + [zrow] * (LANE - 2), axis=0)
    hx_ref[...] = jnp.concatenate([h2f, ext_t.T], axis=1)


def _post(xa, ya, yb, mods, w_out_a, w_out_b, g_ffn, rw_t, rb, li):
    b, s, d = xa.shape
    nt = s // TM
    nl = nt - 1
    half = ya.shape[-1]
    tok = lambda w: pl.BlockSpec((None, TM, w), lambda bb, i: (bb, i, 0))
    return pl.pallas_call(
        _post_kernel,
        grid=(b, nt),
        in_specs=[tok(d), tok(half), tok(half),
                  pl.BlockSpec((None, None, None, 6, d), lambda bb, i: (li, bb, i // nl, 0, 0)),
                  pl.BlockSpec((half, d), lambda bb, i: (0, 0)),
                  pl.BlockSpec((half, d), lambda bb, i: (0, 0)),
                  pl.BlockSpec((None, 1, d), lambda bb, i: (li, 0, 0)),
                  pl.BlockSpec((N_EXPERTS, d), lambda bb, i: (0, 0)),
                  pl.BlockSpec((N_EXPERTS, 1), lambda bb, i: (0, 0))],
        out_specs=[tok(d), tok(d + LANE),
                   pl.BlockSpec((None, 8, TM), lambda bb, i: (bb, 0, i)),
                   pl.BlockSpec((NB_PAD, LANE), lambda bb, i: (0, 0))],
        out_shape=[jax.ShapeDtypeStruct((b, s, d), F32),
                   jax.ShapeDtypeStruct((b, s, d + LANE), F32),
                   jax.ShapeDtypeStruct((b, 8, s), F32),
                   jax.ShapeDtypeStruct((NB_PAD, LANE), F32)],
        scratch_shapes=[pltpu.VMEM((NB_PAD, LANE), F32)],
        input_output_aliases={0: 0},
        compiler_params=_cparams(("arbitrary", "arbitrary")),
        name="out_proj_router",
    )(xa, ya, yb, mods, w_out_a, w_out_b, g_ffn, rw_t, rb)


def _scatter_kernel(pos_ref, h_ref, init_ref, o_ref, sem):
    del init_ref

    def body(r2, c):
        for prio in range(DMA_QUEUES):
            r = r2 * DMA_QUEUES + prio
            p = pos_ref[0, r]
            pltpu.make_async_copy(h_ref.at[pl.ds(r, 1), :], o_ref.at[pl.ds(p, 1), :], sem).start(priority=prio)
        return c

    lax.fori_loop(0, TM // DMA_QUEUES, body, 0, unroll=4)
    pltpu.make_async_copy(h_ref, o_ref.at[pl.ds(0, TM), :], sem).wait()


def _scatter_rows(hx, pos3, init):
    t, w = hx.shape
    t_pad = init.shape[0]
    return pl.pallas_call(
        _scatter_kernel,
        grid=(t // TM,),
        in_specs=[pl.BlockSpec((None, 1, TM), lambda i: (i, 0, 0), memory_space=pltpu.SMEM),
                  pl.BlockSpec((TM, w), lambda i: (i, 0)),
                  pl.BlockSpec(memory_space=pl.ANY)],
        out_specs=pl.BlockSpec(memory_space=pl.ANY),
        out_shape=jax.ShapeDtypeStruct((t_pad, w), F32),
        scratch_shapes=[pltpu.SemaphoreType.DMA(())],
        input_output_aliases={2: 0},
        compiler_params=_cparams(("arbitrary",)),
        name="moe_scatter_rows",
    )(pos3, hx, init)


def _gmm_kernel(elo_ref, ehi_ref, valid_ref, hs_ref, wg0, wu0, wd0, wg1, wu1, wd1, o_ref):
    del elo_ref, ehi_ref
    j = pl.program_id(0)
    d = o_ref.shape[-1]

    @pl.when(valid_ref[j] == 1)
    def _():
        hx = hs_ref[...]
        h = hx[:, :d].astype(BF16)

        def ffn(wg, wu, wd):
            a = jnp.dot(h, wg[...], preferred_element_type=F32)
            u = jnp.dot(h, wu[...], preferred_element_type=F32)
            act = (jax.nn.silu(a) * u).astype(BF16)
            return jnp.dot(act, wd[...], preferred_element_type=F32)

        o_ref[...] = hx[:, d:d + 1] * ffn(wg0, wu0, wd0) + hx[:, d + 1:d + 2] * ffn(wg1, wu1, wd1)

    @pl.when(valid_ref[j] == 0)
    def _():
        o_ref[...] = jnp.zeros_like(o_ref)


def _gmm(hs, e_lo, e_hi, valid, wg, wu, wd, li):
    t_pad, w = hs.shape
    d = w - LANE
    dff = wg.shape[-1]
    up = lambda sel: pl.BlockSpec((None, None, d, dff), lambda j, lo, hi, v: (li, sel(lo, hi)[j], 0, 0))
    dn = lambda sel: pl.BlockSpec((None, None, dff, d), lambda j, lo, hi, v: (li, sel(lo, hi)[j], 0, 0))
    first = lambda lo, hi: lo
    second = lambda lo, hi: hi
    return pl.pallas_call(
        _gmm_kernel,
        grid_spec=pltpu.PrefetchScalarGridSpec(
            num_scalar_prefetch=3,
            grid=(t_pad // TME,),
            in_specs=[pl.BlockSpec((TME, w), lambda j, lo, hi, v: (j, 0)),
                      up(first), up(first), dn(first), up(second), up(second), dn(second)],
            out_specs=pl.BlockSpec((TME, d), lambda j, lo, hi, v: (j, 0))),
        out_shape=jax.ShapeDtypeStruct((t_pad, d), F32),
        compiler_params=_cparams(("arbitrary",)),
        name="moe_grouped_ffn",
    )(e_lo, e_hi, valid, hs, wg, wu, wd, wg, wu, wd)


def _route_tables(rt, cnt, t_pad):
    b, _, s = rt.shape
    nb = N_GROUPS * len(PAIRS)
    bucket = rt[:, 0, :].astype(jnp.int32)
    rank = rt[:, 1, :].astype(jnp.int32)
    counts = cnt[:nb, 0].astype(jnp.int32)
    padded = ((counts + TME - 1) // TME) * TME
    ids = jnp.arange(nb, dtype=jnp.int32)
    ends = jnp.sum(jnp.where(ids[None, :] <= ids[:, None], padded[None, :], 0), axis=1)
    starts = ends - padded
    start_of = jnp.sum(jnp.where(bucket[..., None] == ids, starts, 0), axis=-1)
    pos = (start_of + rank).reshape(b * s // TM, 1, TM)
    n_tiles = t_pad // TME
    n_used = ends[-1] // TME
    tile = jnp.arange(n_tiles, dtype=jnp.int32)
    valid = (tile < n_used).astype(jnp.int32)
    tile_c = jnp.minimum(tile, n_used - 1)
    tb = jnp.minimum(jnp.sum((ends[None, :] <= tile_c[:, None] * TME).astype(jnp.int32), axis=1), nb - 1)
    e_lo_of = jnp.asarray([g * EXPERTS_PER_GROUP + p[0] for g in range(N_GROUPS) for p in PAIRS], jnp.int32)
    e_hi_of = jnp.asarray([g * EXPERTS_PER_GROUP + p[1] for g in range(N_GROUPS) for p in PAIRS], jnp.int32)
    hit = tb[:, None] == ids
    e_lo = jnp.sum(jnp.where(hit, e_lo_of, 0), axis=1)
    e_hi = jnp.sum(jnp.where(hit, e_hi_of, 0), axis=1)
    return pos, e_lo, e_hi, valid


def _issue_gather(pos_ref, y_ref, buf_ref, sem, slot):
    def body(r2, c):
        for prio in range(DMA_QUEUES):
            r = r2 * DMA_QUEUES + prio
            p = pos_ref[0, r]
            pltpu.make_async_copy(y_ref.at[pl.ds(p, 1), :], buf_ref.at[slot, pl.ds(r, 1), :],
                                  sem.at[slot]).start(priority=prio)
        return c

    lax.fori_loop(0, TM // DMA_QUEUES, body, 0, unroll=4)


def _gather_tile(pos_ref, posn_ref, y_ref, buf_ref, sem):
    ni = pl.num_programs(1)
    lin = pl.program_id(0) * ni + pl.program_id(1)
    slot = lin % 2

    @pl.when(lin == 0)
    def _():
        _issue_gather(pos_ref, y_ref, buf_ref, sem, slot)

    @pl.when(lin + 1 < pl.num_programs(0) * ni)
    def _():
        _issue_gather(posn_ref, y_ref, buf_ref, sem, 1 - slot)

    pltpu.make_async_copy(y_ref.at[pl.ds(0, TM), :], buf_ref.at[slot], sem.at[slot]).wait()
    return buf_ref[slot]


def _combine_kernel(pos_ref, posn_ref, x_ref, mod_ref, y_ref, xo_ref, buf_ref, sem):
    f = _gather_tile(pos_ref, posn_ref, y_ref, buf_ref, sem)
    mod = mod_ref[...]
    xo_ref[...] = x_ref[...] + mod[5:6] * f


def _final_kernel(pos_ref, posn_ref, x_ref, mod_ref, g_ref, y_ref, o_ref, buf_ref, sem):
    f = _gather_tile(pos_ref, posn_ref, y_ref, buf_ref, sem)
    mod = mod_ref[...]
    x = x_ref[...] + mod[5:6] * f
    var = jnp.mean(x * x, axis=-1, keepdims=True)
    o_ref[...] = (x * lax.rsqrt(var + EPS)) * g_ref[...]


def _pos_specs(nt, ni, nb):
    def cur(bb, i):
        return (bb * nt + i, 0, 0)

    def nxt(bb, i):
        wrap = i + 1 == ni
        b2 = jnp.minimum(jnp.where(wrap, bb + 1, bb), nb - 1)
        i2 = jnp.where(wrap, jnp.where(bb + 1 == nb, i, 0), i + 1)
        return (b2 * nt + i2, 0, 0)

    return [pl.BlockSpec((None, 1, TM), cur, memory_space=pltpu.SMEM),
            pl.BlockSpec((None, 1, TM), nxt, memory_space=pltpu.SMEM)]


def _combine(xa, ys, pos3, mods, li):
    b, s, d = xa.shape
    nt = s // TM
    nl = nt - 1
    tok = pl.BlockSpec((None, TM, d), lambda bb, i: (bb, i, 0))
    return pl.pallas_call(
        _combine_kernel,
        grid=(b, nt),
        in_specs=_pos_specs(nt, nt, b) + [
            tok,
            pl.BlockSpec((None, None, None, 6, d), lambda bb, i: (li, bb, i // nl, 0, 0)),
            pl.BlockSpec(memory_space=pl.ANY)],
        out_specs=tok,
        out_shape=jax.ShapeDtypeStruct((b, s, d), F32),
        scratch_shapes=[pltpu.VMEM((2, TM, d), F32), pltpu.SemaphoreType.DMA((2,))],
        input_output_aliases={2: 0},
        compiler_params=_cparams(("arbitrary", "arbitrary")),
        name="moe_gather_residual",
    )(pos3, pos3, xa, mods, ys)


def _final(xa, ys, pos3, mods, g_final, li, n_lat):
    b, s, d = xa.shape
    nt = s // TM
    nl = n_lat // TM
    tok = pl.BlockSpec((None, TM, d), lambda bb, i: (bb, i, 0))
    return pl.pallas_call(
        _final_kernel,
        grid=(b, nl),
        in_specs=_pos_specs(nt, nl, b) + [
            tok,
            pl.BlockSpec((None, None, None, 6, d), lambda bb, i: (li, bb, 0, 0, 0)),
            pl.BlockSpec((1, d), lambda bb, i: (0, 0)),
            pl.BlockSpec(memory_space=pl.ANY)],
        out_specs=tok,
        out_shape=jax.ShapeDtypeStruct((b, n_lat, d), F32),
        scratch_shapes=[pltpu.VMEM((2, TM, d), F32), pltpu.SemaphoreType.DMA((2,))],
        compiler_params=_cparams(("arbitrary", "arbitrary")),
        name="moe_gather_residual_final_norm",
    )(pos3, pos3, xa, mods, g_final, ys)


def _rope_tables(n_lat, n_ctx, rot_dim, lane_off):
    rows = n_lat // GRID_W
    r = jnp.repeat(jnp.arange(rows, dtype=F32), GRID_W)[:n_lat]
    c = jnp.tile(jnp.arange(GRID_W, dtype=F32), rows)[:n_lat]
    axis_dim = rot_dim // 2
    inv_freq = ROPE_BASE ** (-jnp.arange(0, axis_dim, 2, dtype=F32) / axis_dim)
    ang_r = r[:, None] * inv_freq
    ang_c = c[:, None] * inv_freq
    ang = jnp.concatenate([ang_r, ang_r, ang_c, ang_c], axis=-1)
    cos, sin = jnp.cos(ang), jnp.sin(ang)
    grp = (np.arange(rot_dim) // (rot_dim // 4)) % 2
    sin_a = jnp.where(grp == 0, -sin, 0.0)
    sin_b = jnp.where(grp == 1, sin, 0.0)

    def place(t, fill):
        reps = 2 if rot_dim == 64 else 1
        blk = jnp.full((n_lat, LANE), fill, F32)
        for k in range(reps):
            blk = blk.at[:, lane_off + k * rot_dim:lane_off + (k + 1) * rot_dim].set(t)
        ctx = jnp.full((n_ctx, LANE), fill, F32)
        return jnp.concatenate([blk, ctx], axis=0)

    return place(cos, 1.0), place(sin_a, 0.0), place(sin_b, 0.0)


def _dft_mats(n, scale):
    r = 1
    while r * r < n:
        r *= 2
    k = jnp.arange(n, dtype=jnp.int32)[:, None]
    a = jnp.arange(n // r, dtype=jnp.int32)[None, :]
    c = jnp.arange(r, dtype=jnp.int32)[None, :]
    ang_a = ((k * a * r) % n).astype(F32) * (2.0 * np.pi / n)
    ang_c = ((k * c) % n).astype(F32) * (2.0 * np.pi / n)
    ca, sa = (jnp.cos(ang_a) * scale)[:, :, None], (jnp.sin(ang_a) * scale)[:, :, None]
    cc, sc = jnp.cos(ang_c)[:, None, :], jnp.sin(ang_c)[:, None, :]
    cos = (ca * cc - sa * sc).reshape(n, n)
    sin = (sa * cc + ca * sc).reshape(n, n)
    return cos.astype(BF16), sin.astype(BF16)


def _even_w_in(w):
    d = w.shape[0]
    c3 = 3 * CONV_DIM
    qcols = []
    per_kv = N_Q_HEADS // N_KV_HEADS
    for h in range(N_Q_HEADS):
        wq = w[:, c3 + h * HEAD_DIM:c3 + (h + 1) * HEAD_DIM]
        z = jnp.zeros((d, HEAD_DIM), w.dtype)
        qcols.append(jnp.concatenate([wq, z] if h // per_kv == 0 else [z, wq], axis=1))
    return jnp.concatenate([w[:, :c3]] + qcols + [w[:, c3 + N_Q_HEADS * HEAD_DIM:]], axis=1).astype(BF16)


def _odd_w_in(w):
    d = w.shape[0]
    o = Q_LORA + KV_LORA
    kr = jnp.concatenate([jnp.zeros((d, QK_NOPE), w.dtype), w[:, o:o + QK_ROPE],
                          jnp.zeros((d, LANE - QK_NOPE - QK_ROPE), w.dtype)], axis=1)
    return jnp.concatenate([w[:, :o], kr, w[:, o + QK_ROPE:]], axis=1).astype(BF16)


def _pad_heads(w, per_head, keep_from, keep, heads):
    rows = w.shape[0]
    w3 = w.reshape(rows, heads, per_head)[:, :, keep_from:keep_from + keep]
    w3 = jnp.pad(w3, ((0, 0), (0, 0), (0, LANE - keep)))
    return w3.reshape(rows, heads * LANE).astype(BF16)


def _channel_dft():
    gd = FOURIER_DIM // FOURIER_GROUPS
    k = np.arange(gd)
    ang = 2.0 * np.pi * ((k[:, None] * k[None, :]) % gd) / gd
    eye = np.eye(FOURIER_GROUPS)
    wc = np.kron(eye, np.cos(ang))
    ws = np.kron(eye, np.sin(ang))
    return jnp.asarray(np.concatenate([wc, ws], axis=1), BF16)


def kernel(x, c, ctx, c_ctx, ada_w, ada_b, norm_mix_g, norm_ffn_g, ev_w_in, ev_conv_w, ev_sink, ev_w_out,
           od_w_in, od_q_norm_g, od_w_uq, od_kv_norm_g, od_w_ukv, od_w_out, router_w, router_bias,
           ex_w_gate, ex_w_up, ex_w_down, final_norm_g):
    b, n, d = x.shape
    l = ctx.shape[1]
    s = n + l
    depth = ada_w.shape[0]
    assert l == TM and n % 512 == 0

    rows = 8 * ((b + 1 + 7) // 8)
    cvec = jnp.concatenate([c, c_ctx[None, :], jnp.zeros((rows - b - 1, d), F32)], axis=0)
    mod_all = _modulation(cvec, ada_w, ada_b).reshape(depth, rows, 6, d)
    mods = jnp.stack([mod_all[:, :b], jnp.broadcast_to(mod_all[:, b:b + 1], (depth, b, 6, d))], axis=2)

    tabs_even = _rope_tables(n, l, HEAD_DIM, 0)
    tabs_odd = _rope_tables(n, l, QK_ROPE, QK_NOPE)
    gd = FOURIER_DIM // FOURIER_GROUPS
    dft_lat = _dft_mats(n, (n * gd) ** -0.5)
    dft_ctx = _dft_mats(l, (l * gd) ** -0.5)
    wf = _channel_dft()

    g_mix = norm_mix_g.reshape(depth, 1, d)
    g_ffn = norm_ffn_g.reshape(depth, 1, d)
    rw_t = router_w.T.astype(BF16)
    rb = router_bias.reshape(N_EXPERTS, 1).astype(F32)
    wg, wu, wd = ex_w_gate.astype(BF16), ex_w_up.astype(BF16), ex_w_down.astype(BF16)

    xa = jnp.concatenate([x, ctx], axis=1)
    t_pad = (-(-(b * s) // TME) + N_GROUPS * len(PAIRS)) * TME
    hs = jnp.zeros((t_pad, d + LANE), F32)
    out = None
    for li in range(depth):
        j = li // 2
        if li % 2 == 0:
            u, ab, q, k, v = _pre_even(xa, mods, g_mix, _even_w_in(ev_w_in[j]), tabs_even, li)
            ya, yb = _even_mix(u, ab, q, k, v, ev_conv_w[j], ev_sink[j], n)
            w_out = ev_w_out[j].astype(BF16)
        else:
            q, k, v, ycs = _pre_odd(
                xa, mods, g_mix, _odd_w_in(od_w_in[j]),
                od_q_norm_g[j].reshape(1, Q_LORA),
                _pad_heads(od_w_uq[j], QK_NOPE + QK_ROPE, 0, QK_NOPE + QK_ROPE, MLA_HEADS),
                od_kv_norm_g[j].reshape(1, KV_LORA),
                _pad_heads(od_w_ukv[j], QK_NOPE + V_HEAD, 0, QK_NOPE, MLA_HEADS),
                _pad_heads(od_w_ukv[j], QK_NOPE + V_HEAD, QK_NOPE, V_HEAD, MLA_HEADS),
                wf, tabs_odd, li)
            ya = _mla(q, k, v, n)
            yb = _fourier(ycs, dft_lat, dft_ctx, n)
            w_out = od_w_out[j].astype(BF16)
        half = w_out.shape[0] // 2
        xa, hx, rt, cnt = _post(xa, ya, yb, mods, w_out[:half], w_out[half:], g_ffn, rw_t, rb, li)
        pos3, e_lo, e_hi, valid = _route_tables(rt, cnt, t_pad)
        hs = _scatter_rows(hx.reshape(b * s, d + LANE), pos3, hs)
        ys = _gmm(hs, e_lo, e_hi, valid, wg, wu, wd, li)
        if li == depth - 1:
            out = _final(xa, ys, pos3, mods, final_norm_g.reshape(1, d), li, n)
        else:
            xa = _combine(xa, ys, pos3, mods, li)
    return out
```

```python
import functools

import jax
import jax.numpy as jnp
import numpy as np
from jax import lax
from jax.experimental import pallas as pl
from jax.experimental.pallas import tpu as pltpu

F32 = jnp.float32
BF16 = jnp.bfloat16

EPS = 1e-6
ROPE_BASE = 10000.0
GRID_W = 64
NEG_INF = -1e30
LOG2E = 1.4426950408889634
LANE = 128
TM = 256
WIN = 128
N_Q_HEADS = 8
N_KV_HEADS = 2
HEAD_DIM = 64
CONV_DIM = 512
MLA_HEADS = 8
Q_LORA = 256
KV_LORA = 128
QK_NOPE = 64
QK_ROPE = 32
V_HEAD = 64
FOURIER_DIM = 512
FOURIER_GROUPS = 4
N_EXPERTS = 16
N_GROUPS = 4
EXPERTS_PER_GROUP = 4
PAIRS = ((0, 1), (0, 2), (0, 3), (1, 2), (1, 3), (2, 3))
NB_PAD = 32
TME = 512
DMA_QUEUES = 2
TQ_MLA = 512
VMEM_LIMIT = 56 * 1024 * 1024


def _cparams(sem):
    return pltpu.CompilerParams(dimension_semantics=sem, vmem_limit_bytes=VMEM_LIMIT)


def _norm_mod(x, g, shift, scale):
    var = jnp.mean(x * x, axis=-1, keepdims=True)
    return (x * lax.rsqrt(var + EPS)) * g * (1.0 + scale) + shift


def _rope(x, cos, sin_a, sin_b, shift):
    w = x.shape[-1]
    return x * cos + pltpu.roll(x, w - shift, 1) * sin_a + pltpu.roll(x, shift, 1) * sin_b


def _mod_kernel(c_ref, w_ref, b_ref, o_ref):
    c = c_ref[...]
    o_ref[...] = jnp.dot(jax.nn.silu(c), w_ref[...], preferred_element_type=F32,
                         precision=lax.Precision.HIGHEST) + b_ref[...]


def _modulation(cvec, ada_w, ada_b):
    depth, d, d6 = ada_w.shape
    tn = d6 // 4
    rows = cvec.shape[0]
    return pl.pallas_call(
        _mod_kernel,
        grid=(depth, d6 // tn),
        in_specs=[pl.BlockSpec((rows, d), lambda l, j: (0, 0)),
                  pl.BlockSpec((None, d, tn), lambda l, j: (l, 0, j)),
                  pl.BlockSpec((None, 1, tn), lambda l, j: (l, 0, j))],
        out_specs=pl.BlockSpec((None, rows, tn), lambda l, j: (l, 0, j)),
        out_shape=jax.ShapeDtypeStruct((depth, rows, d6), F32),
        compiler_params=_cparams(("arbitrary", "arbitrary")),
        name="adaln_modulation",
    )(cvec, ada_w, ada_b.reshape(depth, 1, d6))


def _pre_even_kernel(x_ref, mod_ref, g_ref, w_ref, cos_ref, sa_ref, sb_ref,
                     u_ref, ab_ref, q_ref, k_ref, v_ref):
    mod = mod_ref[...]
    h = _norm_mod(x_ref[...], g_ref[...], mod[0:1], mod[1:2])
    p = jnp.dot(h.astype(BF16), w_ref[...], preferred_element_type=F32)
    c = CONV_DIM
    ax, ab, ac = p[:, 0:c], p[:, c:2 * c], p[:, 2 * c:3 * c]
    qw = N_Q_HEADS * LANE
    q = p[:, 3 * c:3 * c + qw]
    k = p[:, 3 * c + qw:3 * c + qw + LANE]
    v = p[:, 3 * c + qw + LANE:3 * c + qw + 2 * LANE]
    cos, sa, sb = cos_ref[...], sa_ref[...], sb_ref[...]
    cos8 = jnp.concatenate([cos] * N_Q_HEADS, axis=1)
    sa8 = jnp.concatenate([sa] * N_Q_HEADS, axis=1)
    sb8 = jnp.concatenate([sb] * N_Q_HEADS, axis=1)
    u_ref[...] = (ac * ax).astype(BF16)
    ab_ref[...] = ab.astype(BF16)
    q_ref[...] = (_rope(q, cos8, sa8, sb8, 16) * (HEAD_DIM ** -0.5 * LOG2E)).astype(BF16)
    k_ref[...] = _rope(k, cos, sa, sb, 16).astype(BF16)
    v_ref[...] = v.astype(BF16)


def _pre_even(xa, mods, g_mix, w_in, tabs, li):
    b, s, d = xa.shape
    nt = s // TM
    nl = nt - 1
    wcols = w_in.shape[1]
    qw = N_Q_HEADS * LANE
    tab_spec = pl.BlockSpec((TM, LANE), lambda i, bb: (i, 0))
    tok = lambda w: pl.BlockSpec((None, TM, w), lambda i, bb: (bb, i, 0))
    return pl.pallas_call(
        _pre_even_kernel,
        grid=(nt, b),
        in_specs=[tok(d),
                  pl.BlockSpec((None, None, None, 6, d), lambda i, bb: (li, bb, i // nl, 0, 0)),
                  pl.BlockSpec((None, 1, d), lambda i, bb: (li, 0, 0)),
                  pl.BlockSpec((d, wcols), lambda i, bb: (0, 0)),
                  tab_spec, tab_spec, tab_spec],
        out_specs=[tok(CONV_DIM), tok(CONV_DIM), tok(qw), tok(LANE), tok(LANE)],
        out_shape=[jax.ShapeDtypeStruct((b, s, CONV_DIM), BF16),
                   jax.ShapeDtypeStruct((b, s, CONV_DIM), BF16),
                   jax.ShapeDtypeStruct((b, s, qw), BF16),
                   jax.ShapeDtypeStruct((b, s, LANE), BF16),
                   jax.ShapeDtypeStruct((b, s, LANE), BF16)],
        compiler_params=_cparams(("arbitrary", "arbitrary")),
        name="even_in_proj",
    )(xa, mods, g_mix, w_in, *tabs)


def _even_mix_kernel(sink_ref, q_ref, k_ref, v_ref,
                     u_ref, up_ref, un_ref, ab_ref, cw_ref, yc_ref, ya_ref, *, nl, n_lat):
    i = pl.program_id(1)
    nt = pl.num_programs(1)
    is_lat = i < nl
    u = u_ref[...].astype(F32)
    row = lax.broadcasted_iota(jnp.int32, u.shape, 0)
    first = jnp.logical_or(i == 0, i == nl)
    last = jnp.logical_or(i == nl - 1, i == nt - 1)
    prev_row = jnp.where(first, 0.0, up_ref[15:16, :].astype(F32))
    next_row = jnp.where(last, 0.0, un_ref[0:1, :].astype(F32))
    u_prev = jnp.where(row == 0, prev_row, pltpu.roll(u, 1, 0))
    u_next = jnp.where(row == TM - 1, next_row, pltpu.roll(u, TM - 1, 0))
    cw = cw_ref[...]
    conv = u_prev * cw[0:1] + u * cw[1:2] + u_next * cw[2:3]
    yc_ref[...] = (ab_ref[...].astype(F32) * conv).astype(BF16)

    s_all = k_ref.shape[0]
    nwin = TM + 2 * WIN
    w0 = pl.multiple_of(jnp.clip(i * TM - WIN, 0, s_all - nwin), WIN)
    k_all = jnp.concatenate([k_ref[pl.ds(w0, nwin), :], k_ref[n_lat:, :]], axis=0)
    v_all = jnp.concatenate([v_ref[pl.ds(w0, nwin), :], v_ref[n_lat:, :]], axis=0)
    nk = k_all.shape[0]
    qpos = i * TM + lax.broadcasted_iota(jnp.int32, (TM, nk), 0)
    kj = lax.broadcasted_iota(jnp.int32, (TM, nk), 1)
    kpos = w0 + kj
    valid_win = (jnp.abs(kpos - qpos) <= WIN) & (kpos < n_lat) & is_lat
    bias = jnp.where((kj >= nwin) | valid_win, 0.0, NEG_INF).astype(F32)
    lane = lax.broadcasted_iota(jnp.int32, (TM, LANE), 1)
    outs = []
    for pair in range(N_Q_HEADS // 2):
        kvh = (2 * pair) // (N_Q_HEADS // N_KV_HEADS)
        o_pair = []
        for h in (2 * pair, 2 * pair + 1):
            q = q_ref[:, h * LANE:(h + 1) * LANE]
            s = lax.dot_general(q, k_all, (((1,), (1,)), ((), ())), preferred_element_type=F32) + bias
            sink = sink_ref[h] * LOG2E
            m = jnp.maximum(jnp.max(s, axis=1, keepdims=True), sink)
            p = jnp.exp2(s - m)
            l = jnp.sum(p, axis=1, keepdims=True) + jnp.exp2(sink - m)
            o = jnp.dot(p.astype(BF16), v_all, preferred_element_type=F32) / l
            if kvh == 1:
                o = pltpu.roll(o, HEAD_DIM, 1)
            o_pair.append(o)
        outs.append(jnp.where(lane < HEAD_DIM, o_pair[0], pltpu.roll(o_pair[1], HEAD_DIM, 1)))
    ya_ref[...] = jnp.concatenate(outs, axis=1).astype(BF16)


def _even_mix(u, ab, q, k, v, conv_w, sink, n_lat):
    b, s, _ = u.shape
    nt = s // TM
    nl = n_lat // TM
    qw = q.shape[-1]
    whole = pl.BlockSpec((None, s, LANE), lambda bb, i: (bb, 0, 0))
    halo = TM // 16
    u16 = u.reshape(b, s // 16, 16, CONV_DIM)
    kern = functools.partial(_even_mix_kernel, nl=nl, n_lat=n_lat)
    tok = lambda w: pl.BlockSpec((None, TM, w), lambda bb, i: (bb, i, 0))
    return pl.pallas_call(
        kern,
        grid=(b, nt),
        in_specs=[pl.BlockSpec(memory_space=pltpu.SMEM),
                  tok(qw), whole, whole,
                  tok(CONV_DIM),
                  pl.BlockSpec((None, None, 16, CONV_DIM),
                               lambda bb, i: (bb, jnp.maximum(i * halo - 1, 0), 0, 0)),
                  pl.BlockSpec((None, None, 16, CONV_DIM),
                               lambda bb, i: (bb, jnp.minimum((i + 1) * halo, s // 16 - 1), 0, 0)),
                  tok(CONV_DIM),
                  pl.BlockSpec((3, CONV_DIM), lambda bb, i: (0, 0))],
        out_specs=[tok(CONV_DIM), tok(N_Q_HEADS * HEAD_DIM)],
        out_shape=[jax.ShapeDtypeStruct((b, s, CONV_DIM), BF16),
                   jax.ShapeDtypeStruct((b, s, N_Q_HEADS * HEAD_DIM), BF16)],
        compiler_params=_cparams(("arbitrary", "arbitrary")),
        name="even_conv_window_attn",
    )(sink, q, k, v, u, u16, u16, ab, conv_w)


def _pre_odd_kernel(x_ref, mod_ref, g_ref, w_ref, gq_ref, wuq_ref, gkv_ref, wuk_ref, wuv_ref,
                    wf_ref, cos_ref, sa_ref, sb_ref, q_ref, k_ref, v_ref, y_ref):
    mod = mod_ref[...]
    h = _norm_mod(x_ref[...], g_ref[...], mod[0:1], mod[1:2])
    p = jnp.dot(h.astype(BF16), w_ref[...], preferred_element_type=F32)
    cq = p[:, 0:Q_LORA]
    ckv = p[:, Q_LORA:Q_LORA + KV_LORA]
    kr = p[:, Q_LORA + KV_LORA:Q_LORA + KV_LORA + LANE]
    f = p[:, Q_LORA + KV_LORA + LANE:]
    cos, sa, sb = cos_ref[...], sa_ref[...], sb_ref[...]
    cos8 = jnp.concatenate([cos] * MLA_HEADS, axis=1)
    sa8 = jnp.concatenate([sa] * MLA_HEADS, axis=1)
    sb8 = jnp.concatenate([sb] * MLA_HEADS, axis=1)

    cqn = cq * lax.rsqrt(jnp.mean(cq * cq, axis=-1, keepdims=True) + EPS) * gq_ref[...]
    q = jnp.dot(cqn.astype(BF16), wuq_ref[...], preferred_element_type=F32)
    q = _rope(q, cos8, sa8, sb8, 8) * ((QK_NOPE + QK_ROPE) ** -0.5 * LOG2E)
    q_ref[...] = q.astype(BF16)

    ckvn = (ckv * lax.rsqrt(jnp.mean(ckv * ckv, axis=-1, keepdims=True) + EPS) * gkv_ref[...]).astype(BF16)
    kn = jnp.dot(ckvn, wuk_ref[...], preferred_element_type=F32)
    krr = _rope(kr, cos, sa, sb, 8)
    k_ref[...] = (kn + jnp.concatenate([krr] * MLA_HEADS, axis=1)).astype(BF16)
    v = jnp.dot(ckvn, wuv_ref[...], preferred_element_type=F32)
    one_lane = lax.broadcasted_iota(jnp.int32, v.shape, 1) % LANE == V_HEAD
    v_ref[...] = jnp.where(one_lane, 1.0, v).astype(BF16)
    y_ref[...] = jnp.dot(f.astype(BF16), wf_ref[...], preferred_element_type=F32).astype(BF16)


def _pre_odd(xa, mods, g_mix, w_in, gq, wuq, gkv, wuk, wuv, wf, tabs, li):
    b, s, d = xa.shape
    nt = s // TM
    nl = nt - 1
    hw = MLA_HEADS * LANE
    vw = hw
    tab_spec = pl.BlockSpec((TM, LANE), lambda i, bb: (i, 0))
    tok = lambda w: pl.BlockSpec((None, TM, w), lambda i, bb: (bb, i, 0))
    full = lambda a: pl.BlockSpec(a.shape, lambda i, bb: (0,) * a.ndim)
    return pl.pallas_call(
        _pre_odd_kernel,
        grid=(nt, b),
        in_specs=[tok(d),
                  pl.BlockSpec((None, None, None, 6, d), lambda i, bb: (li, bb, i // nl, 0, 0)),
                  pl.BlockSpec((None, 1, d), lambda i, bb: (li, 0, 0)),
                  full(w_in), full(gq), full(wuq), full(gkv), full(wuk), full(wuv), full(wf),
                  tab_spec, tab_spec, tab_spec],
        out_specs=[tok(hw), tok(hw), tok(vw), tok(2 * FOURIER_DIM)],
        out_shape=[jax.ShapeDtypeStruct((b, s, hw), BF16),
                   jax.ShapeDtypeStruct((b, s, hw), BF16),
                   jax.ShapeDtypeStruct((b, s, vw), BF16),
                   jax.ShapeDtypeStruct((b, s, 2 * FOURIER_DIM), BF16)],
        compiler_params=_cparams(("arbitrary", "arbitrary")),
        name="odd_in_proj",
    )(xa, mods, g_mix, w_in, gq, wuq, gkv, wuk, wuv, wf, *tabs)


def _mla_kernel(q_ref, k_ref, v_ref, o_ref):
    lane = lax.broadcasted_iota(jnp.int32, (q_ref.shape[0], LANE), 1)
    outs = []
    for pair in range(MLA_HEADS // 2):
        o_pair = []
        for h in (2 * pair, 2 * pair + 1):
            q = q_ref[:, h * LANE:(h + 1) * LANE]
            k = k_ref[:, h * LANE:(h + 1) * LANE]
            v = v_ref[:, h * LANE:(h + 1) * LANE]
            s = lax.dot_general(q, k, (((1,), (1,)), ((), ())), preferred_element_type=F32)
            m = jnp.max(s, axis=1, keepdims=True)
            p = jnp.exp2(s - m)
            o = jnp.dot(p.astype(BF16), v, preferred_element_type=F32)
            o_pair.append(o / o[:, V_HEAD:V_HEAD + 1])
        outs.append(jnp.where(lane < V_HEAD, o_pair[0], pltpu.roll(o_pair[1], V_HEAD, 1)))
    o_ref[...] = jnp.concatenate(outs, axis=1).astype(BF16)


def _mla(q, k, v, n_lat):
    b, s, hw = q.shape
    vw = MLA_HEADS * V_HEAD
    n_ctx = s - n_lat
    tq = TQ_MLA if n_lat % TQ_MLA == 0 else TM
    once = pl.Buffered(1)
    lat = pl.pallas_call(
        _mla_kernel,
        grid=(b, n_lat // tq),
        in_specs=[pl.BlockSpec((None, tq, hw), lambda bb, i: (bb, i, 0)),
                  pl.BlockSpec((None, s, hw), lambda bb, i: (bb, 0, 0), pipeline_mode=once),
                  pl.BlockSpec((None, s, hw), lambda bb, i: (bb, 0, 0), pipeline_mode=once)],
        out_specs=pl.BlockSpec((None, tq, vw), lambda bb, i: (bb, i, 0)),
        out_shape=jax.ShapeDtypeStruct((b, n_lat, vw), BF16),
        compiler_params=_cparams(("arbitrary", "arbitrary")),
        name="mla_attention",
    )(q, k, v)
    cblk = n_lat // n_ctx
    ctx_spec = pl.BlockSpec((None, n_ctx, hw), lambda bb: (bb, cblk, 0))
    ctx = pl.pallas_call(
        _mla_kernel,
        grid=(b,),
        in_specs=[ctx_spec, ctx_spec, ctx_spec],
        out_specs=pl.BlockSpec((None, n_ctx, vw), lambda bb: (bb, 0, 0)),
        out_shape=jax.ShapeDtypeStruct((b, n_ctx, vw), BF16),
        compiler_params=_cparams(("arbitrary",)),
        name="mla_attention_context",
    )(q, k, v)
    return jnp.concatenate([lat, ctx], axis=1)


def _dft_kernel(ac_ref, as_ref, cc_ref, cs_ref, yc_ref, ys_ref, ycc_ref, ysc_ref, o_ref, *, nl):
    i = pl.program_id(1)

    @pl.when(i < nl)
    def _():
        o_ref[...] = (jnp.dot(ac_ref[...], yc_ref[...], preferred_element_type=F32)
                      - jnp.dot(as_ref[...], ys_ref[...], preferred_element_type=F32)).astype(BF16)

    @pl.when(i >= nl)
    def _():
        o_ref[...] = (jnp.dot(cc_ref[...], ycc_ref[...], preferred_element_type=F32)
                      - jnp.dot(cs_ref[...], ysc_ref[...], preferred_element_type=F32)).astype(BF16)


def _fourier(ycs, dft_lat, dft_ctx, n_lat):
    b, s, _ = ycs.shape
    fd = FOURIER_DIM
    n_ctx = s - n_lat
    nl = n_lat // TM
    ac, as_ = dft_lat
    cc, cs = dft_ctx
    rows = pl.BlockSpec((TM, n_lat), lambda bb, i: (jnp.minimum(i, nl - 1), 0))
    small = pl.BlockSpec((n_ctx, n_ctx), lambda bb, i: (0, 0))
    return pl.pallas_call(
        functools.partial(_dft_kernel, nl=nl),
        grid=(b, s // TM),
        in_specs=[rows, rows, small, small,
                  pl.BlockSpec((None, n_lat, fd), lambda bb, i: (bb, 0, 0)),
                  pl.BlockSpec((None, n_lat, fd), lambda bb, i: (bb, 0, 1)),
                  pl.BlockSpec((None, n_ctx, fd), lambda bb, i: (bb, n_lat // n_ctx, 0)),
                  pl.BlockSpec((None, n_ctx, fd), lambda bb, i: (bb, n_lat // n_ctx, 1))],
        out_specs=pl.BlockSpec((None, TM, fd), lambda bb, i: (bb, i, 0)),
        out_shape=jax.ShapeDtypeStruct((b, s, fd), BF16),
        compiler_params=_cparams(("arbitrary", "arbitrary")),
        name="fourier_mix",
    )(ac, as_, cc, cs, ycs, ycs, ycs, ycs)


def _post_kernel(x_ref, ya_ref, yb_ref, mod_ref, wa_ref, wb_ref, g_ref, rw_ref, rb_ref,
                 xo_ref, hx_ref, rt_ref, cnt_ref, carry_ref):
    @pl.when((pl.program_id(0) == 0) & (pl.program_id(1) == 0))
    def _():
        carry_ref[...] = jnp.zeros_like(carry_ref)

    mod = mod_ref[...]
    y = (jnp.dot(ya_ref[...], wa_ref[...], preferred_element_type=F32)
         + jnp.dot(yb_ref[...], wb_ref[...], preferred_element_type=F32))
    x = x_ref[...] + mod[2:3] * y
    xo_ref[...] = x
    h2f = _norm_mod(x, g_ref[...], mod[3:4], mod[4:5])
    h2 = h2f.astype(BF16)
    logits = lax.dot_general(rw_ref[...], h2, (((1,), (1,)), ((), ())), preferred_element_type=F32)
    scores = jax.nn.sigmoid(logits)
    biased = scores + rb_ref[...]
    sc = [scores[e:e + 1, :] for e in range(N_EXPERTS)]
    bs = [biased[e:e + 1, :] for e in range(N_EXPERTS)]
    sel, gval = [], []
    for g in range(N_GROUPS):
        ids = range(g * EXPERTS_PER_GROUP, (g + 1) * EXPERTS_PER_GROUP)
        gsum = None
        for e in ids:
            rank = sum(((bs[j] > bs[e]) if j > e else (bs[j] >= bs[e])).astype(jnp.int32)
                       for j in ids if j != e)
            s_e = rank < 2
            sel.append(s_e)
            term = jnp.where(s_e, bs[e], 0.0)
            gsum = term if gsum is None else gsum + term
        gval.append(gsum)
    gsel = []
    for g in range(N_GROUPS):
        ok = None
        for j in range(N_GROUPS):
            if j == g:
                continue
            c = (gval[g] > gval[j]) if j < g else (gval[g] >= gval[j])
            ok = c if ok is None else (ok & c)
        gsel.append(ok)
    chosen = [sel[e] & gsel[e // EXPERTS_PER_GROUP] for e in range(N_EXPERTS)]
    wsum = sum(jnp.where(chosen[e], sc[e], 0.0) for e in range(N_EXPERTS))
    gates = [jnp.where(chosen[e], sc[e], 0.0) / wsum for e in range(N_EXPERTS)]
    bucket = jnp.zeros_like(wsum)
    w_lo = jnp.zeros_like(wsum)
    w_hi = jnp.zeros_like(wsum)
    for g in range(N_GROUPS):
        for pi, (ea, eb) in enumerate(PAIRS):
            ea, eb = g * EXPERTS_PER_GROUP + ea, g * EXPERTS_PER_GROUP + eb
            cond = chosen[ea] & chosen[eb]
            bucket = jnp.where(cond, float(g * len(PAIRS) + pi), bucket)
            w_lo = jnp.where(cond, gates[ea], w_lo)
            w_hi = jnp.where(cond, gates[eb], w_hi)
    kio = lax.broadcasted_iota(jnp.int32, (NB_PAD, TM), 0).astype(F32)
    onehot = (kio == bucket).astype(F32)
    tri = (lax.broadcasted_iota(jnp.int32, (TM, TM), 0)
           <= lax.broadcasted_iota(jnp.int32, (TM, TM), 1)).astype(BF16)
    prefix = jnp.dot(onehot.astype(BF16), tri, preferred_element_type=F32)
    carry = carry_ref[...]
    rank = jnp.sum(onehot * (prefix + carry[:, 0:1]), axis=0, keepdims=True) - 1.0
    carry = carry + jnp.sum(onehot, axis=1, keepdims=True)
    carry_ref[...] = carry
    cnt_ref[...] = carry
    zrow = jnp.zeros_like(wsum)
    rt_ref[...] = jnp.concatenate([bucket, rank] + [zrow] * 6, axis=0)
    ext_t = jnp.concatenate([w_lo, w_hi] + [zrow] * (LANE - 2), axis=0)
    hx_ref[...] = jnp.concatenate([h2f, ext_t.T], axis=1)


def _post(xa, ya, yb, mods, w_out_a, w_out_b, g_ffn, rw_t, rb, li):
    b, s, d = xa.shape
    nt = s // TM
    nl = nt - 1
    half = ya.shape[-1]
    tok = lambda w: pl.BlockSpec((None, TM, w), lambda bb, i: (bb, i, 0))
    return pl.pallas_call(
        _post_kernel,
        grid=(b, nt),
        in_specs=[tok(d), tok(half), tok(half),
                  pl.BlockSpec((None, None, None, 6, d), lambda bb, i: (li, bb, i // nl, 0, 0)),
                  pl.BlockSpec((half, d), lambda bb, i: (0, 0)),
                  pl.BlockSpec((half, d), lambda bb, i: (0, 0)),
                  pl.BlockSpec((None, 1, d), lambda bb, i: (li, 0, 0)),
                  pl.BlockSpec((N_EXPERTS, d), lambda bb, i: (0, 0)),
                  pl.BlockSpec((N_EXPERTS, 1), lambda bb, i: (0, 0))],
        out_specs=[tok(d), tok(d + LANE),
                   pl.BlockSpec((None, 8, TM), lambda bb, i: (bb, 0, i)),
                   pl.BlockSpec((NB_PAD, LANE), lambda bb, i: (0, 0))],
        out_shape=[jax.ShapeDtypeStruct((b, s, d), F32),
                   jax.ShapeDtypeStruct((b, s, d + LANE), F32),
                   jax.ShapeDtypeStruct((b, 8, s), F32),
                   jax.ShapeDtypeStruct((NB_PAD, LANE), F32)],
        scratch_shapes=[pltpu.VMEM((NB_PAD, LANE), F32)],
        input_output_aliases={0: 0},
        compiler_params=_cparams(("arbitrary", "arbitrary")),
        name="out_proj_router",
    )(xa, ya, yb, mods, w_out_a, w_out_b, g_ffn, rw_t, rb)


def _scatter_kernel(pos_ref, h_ref, init_ref, o_ref, sem):
    del init_ref

    def body(r2, c):
        for prio in range(DMA_QUEUES):
            r = r2 * DMA_QUEUES + prio
            p = pos_ref[0, r]
            pltpu.make_async_copy(h_ref.at[pl.ds(r, 1), :], o_ref.at[pl.ds(p, 1), :], sem).start(priority=prio)
        return c

    lax.fori_loop(0, TM // DMA_QUEUES, body, 0, unroll=4)
    pltpu.make_async_copy(h_ref, o_ref.at[pl.ds(0, TM), :], sem).wait()


def _scatter_rows(hx, pos3, init):
    t, w = hx.shape
    t_pad = init.shape[0]
    return pl.pallas_call(
        _scatter_kernel,
        grid=(t // TM,),
        in_specs=[pl.BlockSpec((None, 1, TM), lambda i: (i, 0, 0), memory_space=pltpu.SMEM),
                  pl.BlockSpec((TM, w), lambda i: (i, 0)),
                  pl.BlockSpec(memory_space=pl.ANY)],
        out_specs=pl.BlockSpec(memory_space=pl.ANY),
        out_shape=jax.ShapeDtypeStruct((t_pad, w), F32),
        scratch_shapes=[pltpu.SemaphoreType.DMA(())],
        input_output_aliases={2: 0},
        compiler_params=_cparams(("arbitrary",)),
        name="moe_scatter_rows",
    )(pos3, hx, init)


def _gmm_kernel(elo_ref, ehi_ref, valid_ref, hs_ref, wg0, wu0, wd0, wg1, wu1, wd1, o_ref):
    del elo_ref, ehi_ref
    j = pl.program_id(0)
    d = o_ref.shape[-1]

    @pl.when(valid_ref[j] == 1)
    def _():
        hx = hs_ref[...]
        h = hx[:, :d].astype(BF16)

        def ffn(wg, wu, wd):
            a = jnp.dot(h, wg[...], preferred_element_type=F32)
            u = jnp.dot(h, wu[...], preferred_element_type=F32)
            act = (jax.nn.silu(a) * u).astype(BF16)
            return jnp.dot(act, wd[...], preferred_element_type=F32)

        o_ref[...] = hx[:, d:d + 1] * ffn(wg0, wu0, wd0) + hx[:, d + 1:d + 2] * ffn(wg1, wu1, wd1)

    @pl.when(valid_ref[j] == 0)
    def _():
        o_ref[...] = jnp.zeros_like(o_ref)


def _gmm(hs, e_lo, e_hi, valid, wg, wu, wd, li):
    t_pad, w = hs.shape
    d = w - LANE
    dff = wg.shape[-1]
    up = lambda sel: pl.BlockSpec((None, None, d, dff), lambda j, lo, hi, v: (li, sel(lo, hi)[j], 0, 0))
    dn = lambda sel: pl.BlockSpec((None, None, dff, d), lambda j, lo, hi, v: (li, sel(lo, hi)[j], 0, 0))
    first = lambda lo, hi: lo
    second = lambda lo, hi: hi
    return pl.pallas_call(
        _gmm_kernel,
        grid_spec=pltpu.PrefetchScalarGridSpec(
            num_scalar_prefetch=3,
            grid=(t_pad // TME,),
            in_specs=[pl.BlockSpec((TME, w), lambda j, lo, hi, v: (j, 0)),
                      up(first), up(first), dn(first), up(second), up(second), dn(second)],
            out_specs=pl.BlockSpec((TME, d), lambda j, lo, hi, v: (j, 0))),
        out_shape=jax.ShapeDtypeStruct((t_pad, d), F32),
        compiler_params=_cparams(("arbitrary",)),
        name="moe_grouped_ffn",
    )(e_lo, e_hi, valid, hs, wg, wu, wd, wg, wu, wd)


def _route_tables(rt, cnt, t_pad):
    b, _, s = rt.shape
    nb = N_GROUPS * len(PAIRS)
    bucket = rt[:, 0, :].astype(jnp.int32)
    rank = rt[:, 1, :].astype(jnp.int32)
    counts = cnt[:nb, 0].astype(jnp.int32)
    padded = ((counts + TME - 1) // TME) * TME
    ids = jnp.arange(nb, dtype=jnp.int32)
    ends = jnp.sum(jnp.where(ids[None, :] <= ids[:, None], padded[None, :], 0), axis=1)
    starts = ends - padded
    start_of = jnp.sum(jnp.where(bucket[..., None] == ids, starts, 0), axis=-1)
    pos = (start_of + rank).reshape(b * s // TM, 1, TM)
    n_tiles = t_pad // TME
    n_used = ends[-1] // TME
    tile = jnp.arange(n_tiles, dtype=jnp.int32)
    valid = (tile < n_used).astype(jnp.int32)
    tile_c = jnp.minimum(tile, n_used - 1)
    tb = jnp.minimum(jnp.sum((ends[None, :] <= tile_c[:, None] * TME).astype(jnp.int32), axis=1), nb - 1)
    e_lo_of = jnp.asarray([g * EXPERTS_PER_GROUP + p[0] for g in range(N_GROUPS) for p in PAIRS], jnp.int32)
    e_hi_of = jnp.asarray([g * EXPERTS_PER_GROUP + p[1] for g in range(N_GROUPS) for p in PAIRS], jnp.int32)
    hit = tb[:, None] == ids
    e_lo = jnp.sum(jnp.where(hit, e_lo_of, 0), axis=1)
    e_hi = jnp.sum(jnp.where(hit, e_hi_of, 0), axis=1)
    return pos, e_lo, e_hi, valid


def _issue_gather(pos_ref, y_ref, buf_ref, sem, slot):
    def body(r2, c):
        for prio in range(DMA_QUEUES):
            r = r2 * DMA_QUEUES + prio
            p = pos_ref[0, r]
            pltpu.make_async_copy(y_ref.at[pl.ds(p, 1), :], buf_ref.at[slot, pl.ds(r, 1), :],
                                  sem.at[slot]).start(priority=prio)
        return c

    lax.fori_loop(0, TM // DMA_QUEUES, body, 0, unroll=4)


def _gather_tile(pos_ref, posn_ref, y_ref, buf_ref, sem):
    ni = pl.num_programs(1)
    lin = pl.program_id(0) * ni + pl.program_id(1)
    slot = lin % 2

    @pl.when(lin == 0)
    def _():
        _issue_gather(pos_ref, y_ref, buf_ref, sem, slot)

    @pl.when(lin + 1 < pl.num_programs(0) * ni)
    def _():
        _issue_gather(posn_ref, y_ref, buf_ref, sem, 1 - slot)

    pltpu.make_async_copy(y_ref.at[pl.ds(0, TM), :], buf_ref.at[slot], sem.at[slot]).wait()
    return buf_ref[slot]


def _combine_kernel(pos_ref, posn_ref, x_ref, mod_ref, y_ref, xo_ref, buf_ref, sem):
    f = _gather_tile(pos_ref, posn_ref, y_ref, buf_ref, sem)
    mod = mod_ref[...]
    xo_ref[...] = x_ref[...] + mod[5:6] * f


def _final_kernel(pos_ref, posn_ref, x_ref, mod_ref, g_ref, y_ref, o_ref, buf_ref, sem):
    f = _gather_tile(pos_ref, posn_ref, y_ref, buf_ref, sem)
    mod = mod_ref[...]
    x = x_ref[...] + mod[5:6] * f
    var = jnp.mean(x * x, axis=-1, keepdims=True)
    o_ref[...] = (x * lax.rsqrt(var + EPS)) * g_ref[...]


def _pos_specs(nt, ni, nb):
    def cur(bb, i):
        return (bb * nt + i, 0, 0)

    def nxt(bb, i):
        wrap = i + 1 == ni
        b2 = jnp.minimum(jnp.where(wrap, bb + 1, bb), nb - 1)
        i2 = jnp.where(wrap, jnp.where(bb + 1 == nb, i, 0), i + 1)
        return (b2 * nt + i2, 0, 0)

    return [pl.BlockSpec((None, 1, TM), cur, memory_space=pltpu.SMEM),
            pl.BlockSpec((None, 1, TM), nxt, memory_space=pltpu.SMEM)]


def _combine(xa, ys, pos3, mods, li):
    b, s, d = xa.shape
    nt = s // TM
    nl = nt - 1
    tok = pl.BlockSpec((None, TM, d), lambda bb, i: (bb, i, 0))
    return pl.pallas_call(
        _combine_kernel,
        grid=(b, nt),
        in_specs=_pos_specs(nt, nt, b) + [
            tok,
            pl.BlockSpec((None, None, None, 6, d), lambda bb, i: (li, bb, i // nl, 0, 0)),
            pl.BlockSpec(memory_space=pl.ANY)],
        out_specs=tok,
        out_shape=jax.ShapeDtypeStruct((b, s, d), F32),
        scratch_shapes=[pltpu.VMEM((2, TM, d), F32), pltpu.SemaphoreType.DMA((2,))],
        input_output_aliases={2: 0},
        compiler_params=_cparams(("arbitrary", "arbitrary")),
        name="moe_gather_residual",
    )(pos3, pos3, xa, mods, ys)


def _final(xa, ys, pos3, mods, g_final, li, n_lat):
    b, s, d = xa.shape
    nt = s // TM
    nl = n_lat // TM
    tok = pl.BlockSpec((None, TM, d), lambda bb, i: (bb, i, 0))
    return pl.pallas_call(
        _final_kernel,
        grid=(b, nl),
        in_specs=_pos_specs(nt, nl, b) + [
            tok,
            pl.BlockSpec((None, None, None, 6, d), lambda bb, i: (li, bb, 0, 0, 0)),
            pl.BlockSpec((1, d), lambda bb, i: (0, 0)),
            pl.BlockSpec(memory_space=pl.ANY)],
        out_specs=tok,
        out_shape=jax.ShapeDtypeStruct((b, n_lat, d), F32),
        scratch_shapes=[pltpu.VMEM((2, TM, d), F32), pltpu.SemaphoreType.DMA((2,))],
        compiler_params=_cparams(("arbitrary", "arbitrary")),
        name="moe_gather_residual_final_norm",
    )(pos3, pos3, xa, mods, g_final, ys)


def _rope_tables(n_lat, n_ctx, rot_dim, lane_off):
    rows = n_lat // GRID_W
    r = jnp.repeat(jnp.arange(rows, dtype=F32), GRID_W)[:n_lat]
    c = jnp.tile(jnp.arange(GRID_W, dtype=F32), rows)[:n_lat]
    axis_dim = rot_dim // 2
    inv_freq = ROPE_BASE ** (-jnp.arange(0, axis_dim, 2, dtype=F32) / axis_dim)
    ang_r = r[:, None] * inv_freq
    ang_c = c[:, None] * inv_freq
    ang = jnp.concatenate([ang_r, ang_r, ang_c, ang_c], axis=-1)
    cos, sin = jnp.cos(ang), jnp.sin(ang)
    grp = (np.arange(rot_dim) // (rot_dim // 4)) % 2
    sin_a = jnp.where(grp == 0, -sin, 0.0)
    sin_b = jnp.where(grp == 1, sin, 0.0)

    def place(t, fill):
        reps = 2 if rot_dim == 64 else 1
        blk = jnp.full((n_lat, LANE), fill, F32)
        for k in range(reps):
            blk = blk.at[:, lane_off + k * rot_dim:lane_off + (k + 1) * rot_dim].set(t)
        ctx = jnp.full((n_ctx, LANE), fill, F32)
        return jnp.concatenate([blk, ctx], axis=0)

    return place(cos, 1.0), place(sin_a, 0.0), place(sin_b, 0.0)


def _dft_mats(n, scale):
    r = 1
    while r * r < n:
        r *= 2
    k = jnp.arange(n, dtype=jnp.int32)[:, None]
    a = jnp.arange(n // r, dtype=jnp.int32)[None, :]
    c = jnp.arange(r, dtype=jnp.int32)[None, :]
    ang_a = ((k * a * r) % n).astype(F32) * (2.0 * np.pi / n)
    ang_c = ((k * c) % n).astype(F32) * (2.0 * np.pi / n)
    ca, sa = (jnp.cos(ang_a) * scale)[:, :, None], (jnp.sin(ang_a) * scale)[:, :, None]
    cc, sc = jnp.cos(ang_c)[:, None, :], jnp.sin(ang_c)[:, None, :]
    cos = (ca * cc - sa * sc).reshape(n, n)
    sin = (sa * cc + ca * sc).reshape(n, n)
    return cos.astype(BF16), sin.astype(BF16)


def _even_w_in(w):
    d = w.shape[0]
    c3 = 3 * CONV_DIM
    qcols = []
    per_kv = N_Q_HEADS // N_KV_HEADS
    for h in range(N_Q_HEADS):
        wq = w[:, c3 + h * HEAD_DIM:c3 + (h + 1) * HEAD_DIM]
        z = jnp.zeros((d, HEAD_DIM), w.dtype)
        qcols.append(jnp.concatenate([wq, z] if h // per_kv == 0 else [z, wq], axis=1))
    return jnp.concatenate([w[:, :c3]] + qcols + [w[:, c3 + N_Q_HEADS * HEAD_DIM:]], axis=1).astype(BF16)


def _odd_w_in(w):
    d = w.shape[0]
    o = Q_LORA + KV_LORA
    kr = jnp.concatenate([jnp.zeros((d, QK_NOPE), w.dtype), w[:, o:o + QK_ROPE],
                          jnp.zeros((d, LANE - QK_NOPE - QK_ROPE), w.dtype)], axis=1)
    return jnp.concatenate([w[:, :o], kr, w[:, o + QK_ROPE:]], axis=1).astype(BF16)


def _pad_heads(w, per_head, keep_from, keep, heads):
    rows = w.shape[0]
    w3 = w.reshape(rows, heads, per_head)[:, :, keep_from:keep_from + keep]
    w3 = jnp.pad(w3, ((0, 0), (0, 0), (0, LANE - keep)))
    return w3.reshape(rows, heads * LANE).astype(BF16)


def _channel_dft():
    gd = FOURIER_DIM // FOURIER_GROUPS
    k = np.arange(gd)
    ang = 2.0 * np.pi * ((k[:, None] * k[None, :]) % gd) / gd
    eye = np.eye(FOURIER_GROUPS)
    wc = np.kron(eye, np.cos(ang))
    ws = np.kron(eye, np.sin(ang))
    return jnp.asarray(np.concatenate([wc, ws], axis=1), BF16)


def kernel(x, c, ctx, c_ctx, ada_w, ada_b, norm_mix_g, norm_ffn_g, ev_w_in, ev_conv_w, ev_sink, ev_w_out,
           od_w_in, od_q_norm_g, od_w_uq, od_kv_norm_g, od_w_ukv, od_w_out, router_w, router_bias,
           ex_w_gate, ex_w_up, ex_w_down, final_norm_g):
    b, n, d = x.shape
    l = ctx.shape[1]
    s = n + l
    depth = ada_w.shape[0]
    assert l == TM and n % 512 == 0

    rows = 8 * ((b + 1 + 7) // 8)
    cvec = jnp.concatenate([c, c_ctx[None, :], jnp.zeros((rows - b - 1, d), F32)], axis=0)
    mod_all = _modulation(cvec, ada_w, ada_b).reshape(depth, rows, 6, d)
    mods = jnp.stack([mod_all[:, :b], jnp.broadcast_to(mod_all[:, b:b + 1], (depth, b, 6, d))], axis=2)

    tabs_even = _rope_tables(n, l, HEAD_DIM, 0)
    tabs_odd = _rope_tables(n, l, QK_ROPE, QK_NOPE)
    gd = FOURIER_DIM // FOURIER_GROUPS
    dft_lat = _dft_mats(n, (n * gd) ** -0.5)
    dft_ctx = _dft_mats(l, (l * gd) ** -0.5)
    wf = _channel_dft()

    g_mix = norm_mix_g.reshape(depth, 1, d)
    g_ffn = norm_ffn_g.reshape(depth, 1, d)
    rw_t = router_w.T.astype(BF16)
    rb = router_bias.reshape(N_EXPERTS, 1).astype(F32)
    wg, wu, wd = ex_w_gate.astype(BF16), ex_w_up.astype(BF16), ex_w_down.astype(BF16)

    xa = jnp.concatenate([x, ctx], axis=1)
    t_pad = (-(-(b * s) // TME) + N_GROUPS * len(PAIRS)) * TME
    hs = jnp.zeros((t_pad, d + LANE), F32)
    out = None
    for li in range(depth):
        j = li // 2
        if li % 2 == 0:
            u, ab, q, k, v = _pre_even(xa, mods, g_mix, _even_w_in(ev_w_in[j]), tabs_even, li)
            ya, yb = _even_mix(u, ab, q, k, v, ev_conv_w[j], ev_sink[j], n)
            w_out = ev_w_out[j].astype(BF16)
        else:
            q, k, v, ycs = _pre_odd(
                xa, mods, g_mix, _odd_w_in(od_w_in[j]),
                od_q_norm_g[j].reshape(1, Q_LORA),
                _pad_heads(od_w_uq[j], QK_NOPE + QK_ROPE, 0, QK_NOPE + QK_ROPE, MLA_HEADS),
                od_kv_norm_g[j].reshape(1, KV_LORA),
                _pad_heads(od_w_ukv[j], QK_NOPE + V_HEAD, 0, QK_NOPE, MLA_HEADS),
                _pad_heads(od_w_ukv[j], QK_NOPE + V_HEAD, QK_NOPE, V_HEAD, MLA_HEADS),
                wf, tabs_odd, li)
            ya = _mla(q, k, v, n)
            yb = _fourier(ycs, dft_lat, dft_ctx, n)
            w_out = od_w_out[j].astype(BF16)
        half = w_out.shape[0] // 2
        xa, hx, rt, cnt = _post(xa, ya, yb, mods, w_out[:half], w_out[half:], g_ffn, rw_t, rb, li)
        pos3, e_lo, e_hi, valid = _route_tables(rt, cnt, t_pad)
        hs = _scatter_rows(hx.reshape(b * s, d + LANE), pos3, hs)
        ys = _gmm(hs, e_lo, e_hi, valid, wg, wu, wd, li)
        if li == depth - 1:
            out = _final(xa, ys, pos3, mods, final_norm_g.reshape(1, d), li, n)
        else:
            xa = _combine(xa, ys, pos3, mods, li)
    return out
```

```python
import functools

import jax
import jax.numpy as jnp
import numpy as np
from jax import lax
from jax.experimental import pallas as pl
from jax.experimental.pallas import tpu as pltpu

F32 = jnp.float32
BF16 = jnp.bfloat16

EPS = 1e-6
ROPE_BASE = 10000.0
GRID_W = 64
NEG_INF = -1e30
LOG2E = 1.4426950408889634
LANE = 128
TM = 256
WIN = 128
N_Q_HEADS = 8
N_KV_HEADS = 2
HEAD_DIM = 64
CONV_DIM = 512
MLA_HEADS = 8
Q_LORA = 256
KV_LORA = 128
QK_NOPE = 64
QK_ROPE = 32
V_HEAD = 64
FOURIER_DIM = 512
FOURIER_GROUPS = 4
N_EXPERTS = 16
N_GROUPS = 4
EXPERTS_PER_GROUP = 4
PAIRS = ((0, 1), (0, 2), (0, 3), (1, 2), (1, 3), (2, 3))
NB_PAD = 32
TME = 512
DMA_QUEUES = 2
TQ_MLA = 512
PAIR = 2
VMEM_LIMIT = 56 * 1024 * 1024


def _cparams(sem):
    return pltpu.CompilerParams(dimension_semantics=sem, vmem_limit_bytes=VMEM_LIMIT)


def _norm_mod(x, g, shift, scale):
    var = jnp.mean(x * x, axis=-1, keepdims=True)
    return (x * lax.rsqrt(var + EPS)) * g * (1.0 + scale) + shift


def _rope(x, cos, sin_a, sin_b, shift):
    w = x.shape[-1]
    return x * cos + pltpu.roll(x, w - shift, 1) * sin_a + pltpu.roll(x, shift, 1) * sin_b


def _mod_kernel(c_ref, w_ref, b_ref, o_ref):
    c = c_ref[...]
    o_ref[...] = jnp.dot(jax.nn.silu(c), w_ref[...], preferred_element_type=F32,
                         precision=lax.Precision.HIGHEST) + b_ref[...]


def _modulation(cvec, ada_w, ada_b):
    depth, d, d6 = ada_w.shape
    tn = d6 // 4
    rows = cvec.shape[0]
    return pl.pallas_call(
        _mod_kernel,
        grid=(depth, d6 // tn),
        in_specs=[pl.BlockSpec((rows, d), lambda l, j: (0, 0)),
                  pl.BlockSpec((None, d, tn), lambda l, j: (l, 0, j)),
                  pl.BlockSpec((None, 1, tn), lambda l, j: (l, 0, j))],
        out_specs=pl.BlockSpec((None, rows, tn), lambda l, j: (l, 0, j)),
        out_shape=jax.ShapeDtypeStruct((depth, rows, d6), F32),
        compiler_params=_cparams(("arbitrary", "arbitrary")),
        name="adaln_modulation",
    )(cvec, ada_w, ada_b.reshape(depth, 1, d6))


def _pre_even_kernel(x_ref, mod_ref, g_ref, w_ref, cos_ref, sa_ref, sb_ref,
                     u_ref, ab_ref, q_ref, k_ref, v_ref):
    g = g_ref[...]
    h = jnp.concatenate([_norm_mod(x_ref[j], g, mod_ref[j, 0:1], mod_ref[j, 1:2]) for j in range(PAIR)], axis=0)
    p = jnp.dot(h.astype(BF16), w_ref[...], preferred_element_type=F32)
    c = CONV_DIM
    ax, ab, ac = p[:, 0:c], p[:, c:2 * c], p[:, 2 * c:3 * c]
    qw = N_Q_HEADS * LANE
    q = p[:, 3 * c:3 * c + qw]
    k = p[:, 3 * c + qw:3 * c + qw + LANE]
    v = p[:, 3 * c + qw + LANE:3 * c + qw + 2 * LANE]
    cos, sa, sb = (jnp.concatenate([t[...]] * PAIR, axis=0) for t in (cos_ref, sa_ref, sb_ref))
    cos8 = jnp.concatenate([cos] * N_Q_HEADS, axis=1)
    sa8 = jnp.concatenate([sa] * N_Q_HEADS, axis=1)
    sb8 = jnp.concatenate([sb] * N_Q_HEADS, axis=1)
    split = lambda a: a.reshape(PAIR, TM, a.shape[-1]).astype(BF16)
    u_ref[...] = split(ac * ax)
    ab_ref[...] = split(ab)
    q_ref[...] = split(_rope(q, cos8, sa8, sb8, 16) * (HEAD_DIM ** -0.5 * LOG2E))
    k_ref[...] = split(_rope(k, cos, sa, sb, 16))
    v_ref[...] = split(v)


def _pre_even(xa, mods, g_mix, w_in, tabs, li):
    b, s, d = xa.shape
    nt = s // TM
    nl = nt - 1
    wcols = w_in.shape[1]
    qw = N_Q_HEADS * LANE
    tab_spec = pl.BlockSpec((TM, LANE), lambda i, bb: (i, 0))
    tok = lambda w: pl.BlockSpec((PAIR, TM, w), lambda i, bb: (bb, i, 0))
    return pl.pallas_call(
        _pre_even_kernel,
        grid=(nt, b // PAIR),
        in_specs=[tok(d),
                  pl.BlockSpec((None, PAIR, None, 6, d), lambda i, bb: (li, bb, i // nl, 0, 0)),
                  pl.BlockSpec((None, 1, d), lambda i, bb: (li, 0, 0)),
                  pl.BlockSpec((d, wcols), lambda i, bb: (0, 0)),
                  tab_spec, tab_spec, tab_spec],
        out_specs=[tok(CONV_DIM), tok(CONV_DIM), tok(qw), tok(LANE), tok(LANE)],
        out_shape=[jax.ShapeDtypeStruct((b, s, CONV_DIM), BF16),
                   jax.ShapeDtypeStruct((b, s, CONV_DIM), BF16),
                   jax.ShapeDtypeStruct((b, s, qw), BF16),
                   jax.ShapeDtypeStruct((b, s, LANE), BF16),
                   jax.ShapeDtypeStruct((b, s, LANE), BF16)],
        compiler_params=_cparams(("arbitrary", "arbitrary")),
        name="even_in_proj",
    )(xa, mods, g_mix, w_in, *tabs)


def _even_mix_kernel(sink_ref, q_ref, k_ref, v_ref,
                     u_ref, up_ref, un_ref, ab_ref, cw_ref, yc_ref, ya_ref, *, nl, n_lat):
    i = pl.program_id(1)
    nt = pl.num_programs(1)
    is_lat = i < nl
    u = u_ref[...].astype(F32)
    row = lax.broadcasted_iota(jnp.int32, u.shape, 0)
    first = jnp.logical_or(i == 0, i == nl)
    last = jnp.logical_or(i == nl - 1, i == nt - 1)
    prev_row = jnp.where(first, 0.0, up_ref[15:16, :].astype(F32))
    next_row = jnp.where(last, 0.0, un_ref[0:1, :].astype(F32))
    u_prev = jnp.where(row == 0, prev_row, pltpu.roll(u, 1, 0))
    u_next = jnp.where(row == TM - 1, next_row, pltpu.roll(u, TM - 1, 0))
    cw = cw_ref[...]
    conv = u_prev * cw[0:1] + u * cw[1:2] + u_next * cw[2:3]
    yc_ref[...] = (ab_ref[...].astype(F32) * conv).astype(BF16)

    s_all = k_ref.shape[0]
    nwin = TM + 2 * WIN
    w0 = pl.multiple_of(jnp.clip(i * TM - WIN, 0, s_all - nwin), WIN)
    k_all = jnp.concatenate([k_ref[pl.ds(w0, nwin), :], k_ref[n_lat:, :]], axis=0)
    v_all = jnp.concatenate([v_ref[pl.ds(w0, nwin), :], v_ref[n_lat:, :]], axis=0)
    nk = k_all.shape[0]
    qpos = i * TM + lax.broadcasted_iota(jnp.int32, (TM, nk), 0)
    kj = lax.broadcasted_iota(jnp.int32, (TM, nk), 1)
    kpos = w0 + kj
    valid_win = (jnp.abs(kpos - qpos) <= WIN) & (kpos < n_lat) & is_lat
    bias = jnp.where((kj >= nwin) | valid_win, 0.0, NEG_INF).astype(F32)
    lane = lax.broadcasted_iota(jnp.int32, (TM, LANE), 1)
    outs = []
    for pair in range(N_Q_HEADS // 2):
        kvh = (2 * pair) // (N_Q_HEADS // N_KV_HEADS)
        o_pair = []
        for h in (2 * pair, 2 * pair + 1):
            q = q_ref[:, h * LANE:(h + 1) * LANE]
            s = lax.dot_general(q, k_all, (((1,), (1,)), ((), ())), preferred_element_type=F32) + bias
            sink = sink_ref[h] * LOG2E
            m = jnp.maximum(jnp.max(s, axis=1, keepdims=True), sink)
            p = jnp.exp2(s - m)
            l = jnp.sum(p, axis=1, keepdims=True) + jnp.exp2(sink - m)
            o = jnp.dot(p.astype(BF16), v_all, preferred_element_type=F32) / l
            if kvh == 1:
                o = pltpu.roll(o, HEAD_DIM, 1)
            o_pair.append(o)
        outs.append(jnp.where(lane < HEAD_DIM, o_pair[0], pltpu.roll(o_pair[1], HEAD_DIM, 1)))
    ya_ref[...] = jnp.concatenate(outs, axis=1).astype(BF16)


def _even_mix(u, ab, q, k, v, conv_w, sink, n_lat):
    b, s, _ = u.shape
    nt = s // TM
    nl = n_lat // TM
    qw = q.shape[-1]
    whole = pl.BlockSpec((None, s, LANE), lambda bb, i: (bb, 0, 0))
    halo = TM // 16
    u16 = u.reshape(b, s // 16, 16, CONV_DIM)
    kern = functools.partial(_even_mix_kernel, nl=nl, n_lat=n_lat)
    tok = lambda w: pl.BlockSpec((None, TM, w), lambda bb, i: (bb, i, 0))
    return pl.pallas_call(
        kern,
        grid=(b, nt),
        in_specs=[pl.BlockSpec(memory_space=pltpu.SMEM),
                  tok(qw), whole, whole,
                  tok(CONV_DIM),
                  pl.BlockSpec((None, None, 16, CONV_DIM),
                               lambda bb, i: (bb, jnp.maximum(i * halo - 1, 0), 0, 0)),
                  pl.BlockSpec((None, None, 16, CONV_DIM),
                               lambda bb, i: (bb, jnp.minimum((i + 1) * halo, s // 16 - 1), 0, 0)),
                  tok(CONV_DIM),
                  pl.BlockSpec((3, CONV_DIM), lambda bb, i: (0, 0))],
        out_specs=[tok(CONV_DIM), tok(N_Q_HEADS * HEAD_DIM)],
        out_shape=[jax.ShapeDtypeStruct((b, s, CONV_DIM), BF16),
                   jax.ShapeDtypeStruct((b, s, N_Q_HEADS * HEAD_DIM), BF16)],
        compiler_params=_cparams(("arbitrary", "arbitrary")),
        name="even_conv_window_attn",
    )(sink, q, k, v, u, u16, u16, ab, conv_w)


def _pre_odd_kernel(x_ref, mod_ref, g_ref, w_ref, gq_ref, wuq_ref, gkv_ref, wuk_ref, wuv_ref,
                    wf_ref, cos_ref, sa_ref, sb_ref, q_ref, k_ref, v_ref, y_ref):
    mod = mod_ref[...]
    h = _norm_mod(x_ref[...], g_ref[...], mod[0:1], mod[1:2])
    p = jnp.dot(h.astype(BF16), w_ref[...], preferred_element_type=F32)
    cq = p[:, 0:Q_LORA]
    ckv = p[:, Q_LORA:Q_LORA + KV_LORA]
    kr = p[:, Q_LORA + KV_LORA:Q_LORA + KV_LORA + LANE]
    f = p[:, Q_LORA + KV_LORA + LANE:]
    cos, sa, sb = cos_ref[...], sa_ref[...], sb_ref[...]
    cos8 = jnp.concatenate([cos] * MLA_HEADS, axis=1)
    sa8 = jnp.concatenate([sa] * MLA_HEADS, axis=1)
    sb8 = jnp.concatenate([sb] * MLA_HEADS, axis=1)

    cqn = cq * lax.rsqrt(jnp.mean(cq * cq, axis=-1, keepdims=True) + EPS) * gq_ref[...]
    q = jnp.dot(cqn.astype(BF16), wuq_ref[...], preferred_element_type=F32)
    q = _rope(q, cos8, sa8, sb8, 8) * ((QK_NOPE + QK_ROPE) ** -0.5 * LOG2E)
    q_ref[...] = q.astype(BF16)

    ckvn = (ckv * lax.rsqrt(jnp.mean(ckv * ckv, axis=-1, keepdims=True) + EPS) * gkv_ref[...]).astype(BF16)
    kn = jnp.dot(ckvn, wuk_ref[...], preferred_element_type=F32)
    krr = _rope(kr, cos, sa, sb, 8)
    k_ref[...] = (kn + jnp.concatenate([krr] * MLA_HEADS, axis=1)).astype(BF16)
    v = jnp.dot(ckvn, wuv_ref[...], preferred_element_type=F32)
    one_lane = lax.broadcasted_iota(jnp.int32, v.shape, 1) % LANE == V_HEAD
    v_ref[...] = jnp.where(one_lane, 1.0, v).astype(BF16)
    y_ref[...] = jnp.dot(f.astype(BF16), wf_ref[...], preferred_element_type=F32).astype(BF16)


def _pre_odd(xa, mods, g_mix, w_in, gq, wuq, gkv, wuk, wuv, wf, tabs, li):
    b, s, d = xa.shape
    nt = s // TM
    nl = nt - 1
    hw = MLA_HEADS * LANE
    vw = hw
    tab_spec = pl.BlockSpec((TM, LANE), lambda i, bb: (i, 0))
    tok = lambda w: pl.BlockSpec((None, TM, w), lambda i, bb: (bb, i, 0))
    full = lambda a: pl.BlockSpec(a.shape, lambda i, bb: (0,) * a.ndim)
    return pl.pallas_call(
        _pre_odd_kernel,
        grid=(nt, b),
        in_specs=[tok(d),
                  pl.BlockSpec((None, None, None, 6, d), lambda i, bb: (li, bb, i // nl, 0, 0)),
                  pl.BlockSpec((None, 1, d), lambda i, bb: (li, 0, 0)),
                  full(w_in), full(gq), full(wuq), full(gkv), full(wuk), full(wuv), full(wf),
                  tab_spec, tab_spec, tab_spec],
        out_specs=[tok(hw), tok(hw), tok(vw), tok(2 * FOURIER_DIM)],
        out_shape=[jax.ShapeDtypeStruct((b, s, hw), BF16),
                   jax.ShapeDtypeStruct((b, s, hw), BF16),
                   jax.ShapeDtypeStruct((b, s, vw), BF16),
                   jax.ShapeDtypeStruct((b, s, 2 * FOURIER_DIM), BF16)],
        compiler_params=_cparams(("arbitrary", "arbitrary")),
        name="odd_in_proj",
    )(xa, mods, g_mix, w_in, gq, wuq, gkv, wuk, wuv, wf, *tabs)


def _mla_kernel(q_ref, k_ref, v_ref, o_ref):
    lane = lax.broadcasted_iota(jnp.int32, (q_ref.shape[0], LANE), 1)
    outs = []
    for pair in range(MLA_HEADS // 2):
        o_pair = []
        for h in (2 * pair, 2 * pair + 1):
            q = q_ref[:, h * LANE:(h + 1) * LANE]
            k = k_ref[:, h * LANE:(h + 1) * LANE]
            v = v_ref[:, h * LANE:(h + 1) * LANE]
            s = lax.dot_general(q, k, (((1,), (1,)), ((), ())), preferred_element_type=F32)
            m = jnp.max(s, axis=1, keepdims=True)
            p = jnp.exp2(s - m)
            o = jnp.dot(p.astype(BF16), v, preferred_element_type=F32)
            o_pair.append(o / o[:, V_HEAD:V_HEAD + 1])
        outs.append(jnp.where(lane < V_HEAD, o_pair[0], pltpu.roll(o_pair[1], V_HEAD, 1)))
    o_ref[...] = jnp.concatenate(outs, axis=1).astype(BF16)


def _mla(q, k, v, n_lat):
    b, s, hw = q.shape
    vw = MLA_HEADS * V_HEAD
    n_ctx = s - n_lat
    tq = TQ_MLA if n_lat % TQ_MLA == 0 else TM
    once = pl.Buffered(1)
    lat = pl.pallas_call(
        _mla_kernel,
        grid=(b, n_lat // tq),
        in_specs=[pl.BlockSpec((None, tq, hw), lambda bb, i: (bb, i, 0)),
                  pl.BlockSpec((None, s, hw), lambda bb, i: (bb, 0, 0), pipeline_mode=once),
                  pl.BlockSpec((None, s, hw), lambda bb, i: (bb, 0, 0), pipeline_mode=once)],
        out_specs=pl.BlockSpec((None, tq, vw), lambda bb, i: (bb, i, 0)),
        out_shape=jax.ShapeDtypeStruct((b, n_lat, vw), BF16),
        compiler_params=_cparams(("arbitrary", "arbitrary")),
        name="mla_attention",
    )(q, k, v)
    cblk = n_lat // n_ctx
    ctx_spec = pl.BlockSpec((None, n_ctx, hw), lambda bb: (bb, cblk, 0))
    ctx = pl.pallas_call(
        _mla_kernel,
        grid=(b,),
        in_specs=[ctx_spec, ctx_spec, ctx_spec],
        out_specs=pl.BlockSpec((None, n_ctx, vw), lambda bb: (bb, 0, 0)),
        out_shape=jax.ShapeDtypeStruct((b, n_ctx, vw), BF16),
        compiler_params=_cparams(("arbitrary",)),
        name="mla_attention_context",
    )(q, k, v)
    return jnp.concatenate([lat, ctx], axis=1)


def _dft_kernel(ac_ref, as_ref, cc_ref, cs_ref, yc_ref, ys_ref, ycc_ref, ysc_ref, o_ref, *, nl):
    i = pl.program_id(1)

    @pl.when(i < nl)
    def _():
        o_ref[...] = (jnp.dot(ac_ref[...], yc_ref[...], preferred_element_type=F32)
                      - jnp.dot(as_ref[...], ys_ref[...], preferred_element_type=F32)).astype(BF16)

    @pl.when(i >= nl)
    def _():
        o_ref[...] = (jnp.dot(cc_ref[...], ycc_ref[...], preferred_element_type=F32)
                      - jnp.dot(cs_ref[...], ysc_ref[...], preferred_element_type=F32)).astype(BF16)


def _fourier(ycs, dft_lat, dft_ctx, n_lat):
    b, s, _ = ycs.shape
    fd = FOURIER_DIM
    n_ctx = s - n_lat
    nl = n_lat // TM
    ac, as_ = dft_lat
    cc, cs = dft_ctx
    rows = pl.BlockSpec((TM, n_lat), lambda bb, i: (jnp.minimum(i, nl - 1), 0))
    small = pl.BlockSpec((n_ctx, n_ctx), lambda bb, i: (0, 0))
    return pl.pallas_call(
        functools.partial(_dft_kernel, nl=nl),
        grid=(b, s // TM),
        in_specs=[rows, rows, small, small,
                  pl.BlockSpec((None, n_lat, fd), lambda bb, i: (bb, 0, 0)),
                  pl.BlockSpec((None, n_lat, fd), lambda bb, i: (bb, 0, 1)),
                  pl.BlockSpec((None, n_ctx, fd), lambda bb, i: (bb, n_lat // n_ctx, 0)),
                  pl.BlockSpec((None, n_ctx, fd), lambda bb, i: (bb, n_lat // n_ctx, 1))],
        out_specs=pl.BlockSpec((None, TM, fd), lambda bb, i: (bb, i, 0)),
        out_shape=jax.ShapeDtypeStruct((b, s, fd), BF16),
        compiler_params=_cparams(("arbitrary", "arbitrary")),
        name="fourier_mix",
    )(ac, as_, cc, cs, ycs, ycs, ycs, ycs)


def _post_kernel(x_ref, ya_ref, yb_ref, mod_ref, wa_ref, wb_ref, g_ref, rw_ref, rb_ref,
                 xo_ref, hx_ref, rt_ref, cnt_ref, carry_ref):
    @pl.when((pl.program_id(0) == 0) & (pl.program_id(1) == 0))
    def _():
        carry_ref[...] = jnp.zeros_like(carry_ref)

    mod = mod_ref[...]
    y = (jnp.dot(ya_ref[...], wa_ref[...], preferred_element_type=F32)
         + jnp.dot(yb_ref[...], wb_ref[...], preferred_element_type=F32))
    x = x_ref[...] + mod[2:3] * y
    xo_ref[...] = x
    h2f = _norm_mod(x, g_ref[...], mod[3:4], mod[4:5])
    h2 = h2f.astype(BF16)
    logits = lax.dot_general(rw_ref[...], h2, (((1,), (1,)), ((), ())), preferred_element_type=F32)
    scores = jax.nn.sigmoid(logits)
    biased = scores + rb_ref[...]
    sc = [scores[e:e + 1, :] for e in range(N_EXPERTS)]
    bs = [biased[e:e + 1, :] for e in range(N_EXPERTS)]
    sel, gval = [], []
    for g in range(N_GROUPS):
        ids = range(g * EXPERTS_PER_GROUP, (g + 1) * EXPERTS_PER_GROUP)
        gsum = None
        for e in ids:
            rank = sum(((bs[j] > bs[e]) if j > e else (bs[j] >= bs[e])).astype(jnp.int32)
                       for j in ids if j != e)
            s_e = rank < 2
            sel.append(s_e)
            term = jnp.where(s_e, bs[e], 0.0)
            gsum = term if gsum is None else gsum + term
        gval.append(gsum)
    gsel = []
    for g in range(N_GROUPS):
        ok = None
        for j in range(N_GROUPS):
            if j == g:
                continue
            c = (gval[g] > gval[j]) if j < g else (gval[g] >= gval[j])
            ok = c if ok is None else (ok & c)
        gsel.append(ok)
    chosen = [sel[e] & gsel[e // EXPERTS_PER_GROUP] for e in range(N_EXPERTS)]
    wsum = sum(jnp.where(chosen[e], sc[e], 0.0) for e in range(N_EXPERTS))
    gates = [jnp.where(chosen[e], sc[e], 0.0) / wsum for e in range(N_EXPERTS)]
    bucket = jnp.zeros_like(wsum)
    w_lo = jnp.zeros_like(wsum)
    w_hi = jnp.zeros_like(wsum)
    for g in range(N_GROUPS):
        for pi, (ea, eb) in enumerate(PAIRS):
            ea, eb = g * EXPERTS_PER_GROUP + ea, g * EXPERTS_PER_GROUP + eb
            cond = chosen[ea] & chosen[eb]
            bucket = jnp.where(cond, float(g * len(PAIRS) + pi), bucket)
            w_lo = jnp.where(cond, gates[ea], w_lo)
            w_hi = jnp.where(cond, gates[eb], w_hi)
    kio = lax.broadcasted_iota(jnp.int32, (NB_PAD, TM), 0).astype(F32)
    onehot = (kio == bucket).astype(F32)
    tri = (lax.broadcasted_iota(jnp.int32, (TM, TM), 0)
           <= lax.broadcasted_iota(jnp.int32, (TM, TM), 1)).astype(BF16)
    prefix = jnp.dot(onehot.astype(BF16), tri, preferred_element_type=F32)
    carry = carry_ref[...]
    rank = jnp.sum(onehot * (prefix + carry[:, 0:1]), axis=0, keepdims=True) - 1.0
    carry = carry + jnp.sum(onehot, axis=1, keepdims=True)
    carry_ref[...] = carry
    cnt_ref[...] = carry
    zrow = jnp.zeros_like(wsum)
    rt_ref[...] = jnp.concatenate([bucket, rank] + [zrow] * 6, axis=0)
    ext_t = jnp.concatenate([w_lo, w_hi] + [zrow] * (LANE - 2), axis=0)
    hx_ref[...] = jnp.concatenate([h2f, ext_t.T], axis=1)


def _post(xa, ya, yb, mods, w_out_a, w_out_b, g_ffn, rw_t, rb, li):
    b, s, d = xa.shape
    nt = s // TM
    nl = nt - 1
    half = ya.shape[-1]
    tok = lambda w: pl.BlockSpec((None, TM, w), lambda bb, i: (bb, i, 0))
    return pl.pallas_call(
        _post_kernel,
        grid=(b, nt),
        in_specs=[tok(d), tok(half), tok(half),
                  pl.BlockSpec((None, None, None, 6, d), lambda bb, i: (li, bb, i // nl, 0, 0)),
                  pl.BlockSpec((half, d), lambda bb, i: (0, 0)),
                  pl.BlockSpec((half, d), lambda bb, i: (0, 0)),
                  pl.BlockSpec((None, 1, d), lambda bb, i: (li, 0, 0)),
                  pl.BlockSpec((N_EXPERTS, d), lambda bb, i: (0, 0)),
                  pl.BlockSpec((N_EXPERTS, 1), lambda bb, i: (0, 0))],
        out_specs=[tok(d), tok(d + LANE),
                   pl.BlockSpec((None, 8, TM), lambda bb, i: (bb, 0, i)),
                   pl.BlockSpec((NB_PAD, LANE), lambda bb, i: (0, 0))],
        out_shape=[jax.ShapeDtypeStruct((b, s, d), F32),
                   jax.ShapeDtypeStruct((b, s, d + LANE), F32),
                   jax.ShapeDtypeStruct((b, 8, s), F32),
                   jax.ShapeDtypeStruct((NB_PAD, LANE), F32)],
        scratch_shapes=[pltpu.VMEM((NB_PAD, LANE), F32)],
        input_output_aliases={0: 0},
        compiler_params=_cparams(("arbitrary", "arbitrary")),
        name="out_proj_router",
    )(xa, ya, yb, mods, w_out_a, w_out_b, g_ffn, rw_t, rb)


def _scatter_kernel(pos_ref, h_ref, init_ref, o_ref, sem):
    del init_ref

    def body(r2, c):
        for prio in range(DMA_QUEUES):
            r = r2 * DMA_QUEUES + prio
            p = pos_ref[0, r]
            pltpu.make_async_copy(h_ref.at[pl.ds(r, 1), :], o_ref.at[pl.ds(p, 1), :], sem).start(priority=prio)
        return c

    lax.fori_loop(0, TM // DMA_QUEUES, body, 0, unroll=4)
    pltpu.make_async_copy(h_ref, o_ref.at[pl.ds(0, TM), :], sem).wait()


def _scatter_rows(hx, pos3, init):
    t, w = hx.shape
    t_pad = init.shape[0]
    return pl.pallas_call(
        _scatter_kernel,
        grid=(t // TM,),
        in_specs=[pl.BlockSpec((None, 1, TM), lambda i: (i, 0, 0), memory_space=pltpu.SMEM),
                  pl.BlockSpec((TM, w), lambda i: (i, 0)),
                  pl.BlockSpec(memory_space=pl.ANY)],
        out_specs=pl.BlockSpec(memory_space=pl.ANY),
        out_shape=jax.ShapeDtypeStruct((t_pad, w), F32),
        scratch_shapes=[pltpu.SemaphoreType.DMA(())],
        input_output_aliases={2: 0},
        compiler_params=_cparams(("arbitrary",)),
        name="moe_scatter_rows",
    )(pos3, hx, init)


def _gmm_kernel(elo_ref, ehi_ref, valid_ref, hs_ref, wg0, wu0, wd0, wg1, wu1, wd1, o_ref):
    del elo_ref, ehi_ref
    j = pl.program_id(0)
    d = o_ref.shape[-1]

    @pl.when(valid_ref[j] == 1)
    def _():
        hx = hs_ref[...]
        h = hx[:, :d].astype(BF16)

        def ffn(wg, wu, wd):
            a = jnp.dot(h, wg[...], preferred_element_type=F32)
            u = jnp.dot(h, wu[...], preferred_element_type=F32)
            act = (jax.nn.silu(a) * u).astype(BF16)
            return jnp.dot(act, wd[...], preferred_element_type=F32)

        o_ref[...] = hx[:, d:d + 1] * ffn(wg0, wu0, wd0) + hx[:, d + 1:d + 2] * ffn(wg1, wu1, wd1)

    @pl.when(valid_ref[j] == 0)
    def _():
        o_ref[...] = jnp.zeros_like(o_ref)


def _gmm(hs, e_lo, e_hi, valid, wg, wu, wd, li):
    t_pad, w = hs.shape
    d = w - LANE
    dff = wg.shape[-1]
    up = lambda sel: pl.BlockSpec((None, None, d, dff), lambda j, lo, hi, v: (li, sel(lo, hi)[j], 0, 0))
    dn = lambda sel: pl.BlockSpec((None, None, dff, d), lambda j, lo, hi, v: (li, sel(lo, hi)[j], 0, 0))
    first = lambda lo, hi: lo
    second = lambda lo, hi: hi
    return pl.pallas_call(
        _gmm_kernel,
        grid_spec=pltpu.PrefetchScalarGridSpec(
            num_scalar_prefetch=3,
            grid=(t_pad // TME,),
            in_specs=[pl.BlockSpec((TME, w), lambda j, lo, hi, v: (j, 0)),
                      up(first), up(first), dn(first), up(second), up(second), dn(second)],
            out_specs=pl.BlockSpec((TME, d), lambda j, lo, hi, v: (j, 0))),
        out_shape=jax.ShapeDtypeStruct((t_pad, d), F32),
        compiler_params=_cparams(("arbitrary",)),
        name="moe_grouped_ffn",
    )(e_lo, e_hi, valid, hs, wg, wu, wd, wg, wu, wd)


def _route_tables(rt, cnt, t_pad):
    b, _, s = rt.shape
    nb = N_GROUPS * len(PAIRS)
    bucket = rt[:, 0, :].astype(jnp.int32)
    rank = rt[:, 1, :].astype(jnp.int32)
    counts = cnt[:nb, 0].astype(jnp.int32)
    padded = ((counts + TME - 1) // TME) * TME
    ids = jnp.arange(nb, dtype=jnp.int32)
    ends = jnp.sum(jnp.where(ids[None, :] <= ids[:, None], padded[None, :], 0), axis=1)
    starts = ends - padded
    start_of = jnp.sum(jnp.where(bucket[..., None] == ids, starts, 0), axis=-1)
    pos = (start_of + rank).reshape(b * s // TM, 1, TM)
    n_tiles = t_pad // TME
    n_used = ends[-1] // TME
    tile = jnp.arange(n_tiles, dtype=jnp.int32)
    valid = (tile < n_used).astype(jnp.int32)
    tile_c = jnp.minimum(tile, n_used - 1)
    tb = jnp.minimum(jnp.sum((ends[None, :] <= tile_c[:, None] * TME).astype(jnp.int32), axis=1), nb - 1)
    e_lo_of = jnp.asarray([g * EXPERTS_PER_GROUP + p[0] for g in range(N_GROUPS) for p in PAIRS], jnp.int32)
    e_hi_of = jnp.asarray([g * EXPERTS_PER_GROUP + p[1] for g in range(N_GROUPS) for p in PAIRS], jnp.int32)
    hit = tb[:, None] == ids
    e_lo = jnp.sum(jnp.where(hit, e_lo_of, 0), axis=1)
    e_hi = jnp.sum(jnp.where(hit, e_hi_of, 0), axis=1)
    return pos, e_lo, e_hi, valid


def _issue_gather(pos_ref, y_ref, buf_ref, sem, slot):
    def body(r2, c):
        for prio in range(DMA_QUEUES):
            r = r2 * DMA_QUEUES + prio
            p = pos_ref[0, r]
            pltpu.make_async_copy(y_ref.at[pl.ds(p, 1), :], buf_ref.at[slot, pl.ds(r, 1), :],
                                  sem.at[slot]).start(priority=prio)
        return c

    lax.fori_loop(0, TM // DMA_QUEUES, body, 0, unroll=4)


def _gather_tile(pos_ref, posn_ref, y_ref, buf_ref, sem):
    ni = pl.num_programs(1)
    lin = pl.program_id(0) * ni + pl.program_id(1)
    slot = lin % 2

    @pl.when(lin == 0)
    def _():
        _issue_gather(pos_ref, y_ref, buf_ref, sem, slot)

    @pl.when(lin + 1 < pl.num_programs(0) * ni)
    def _():
        _issue_gather(posn_ref, y_ref, buf_ref, sem, 1 - slot)

    pltpu.make_async_copy(y_ref.at[pl.ds(0, TM), :], buf_ref.at[slot], sem.at[slot]).wait()
    return buf_ref[slot]


def _combine_kernel(pos_ref, posn_ref, x_ref, mod_ref, y_ref, xo_ref, buf_ref, sem):
    f = _gather_tile(pos_ref, posn_ref, y_ref, buf_ref, sem)
    mod = mod_ref[...]
    xo_ref[...] = x_ref[...] + mod[5:6] * f


def _final_kernel(pos_ref, posn_ref, x_ref, mod_ref, g_ref, y_ref, o_ref, buf_ref, sem):
    f = _gather_tile(pos_ref, posn_ref, y_ref, buf_ref, sem)
    mod = mod_ref[...]
    x = x_ref[...] + mod[5:6] * f
    var = jnp.mean(x * x, axis=-1, keepdims=True)
    o_ref[...] = (x * lax.rsqrt(var + EPS)) * g_ref[...]


def _pos_specs(nt, ni, nb):
    def cur(bb, i):
        return (bb * nt + i, 0, 0)

    def nxt(bb, i):
        wrap = i + 1 == ni
        b2 = jnp.minimum(jnp.where(wrap, bb + 1, bb), nb - 1)
        i2 = jnp.where(wrap, jnp.where(bb + 1 == nb, i, 0), i + 1)
        return (b2 * nt + i2, 0, 0)

    return [pl.BlockSpec((None, 1, TM), cur, memory_space=pltpu.SMEM),
            pl.BlockSpec((None, 1, TM), nxt, memory_space=pltpu.SMEM)]


def _combine(xa, ys, pos3, mods, li):
    b, s, d = xa.shape
    nt = s // TM
    nl = nt - 1
    tok = pl.BlockSpec((None, TM, d), lambda bb, i: (bb, i, 0))
    return pl.pallas_call(
        _combine_kernel,
        grid=(b, nt),
        in_specs=_pos_specs(nt, nt, b) + [
            tok,
            pl.BlockSpec((None, None, None, 6, d), lambda bb, i: (li, bb, i // nl, 0, 0)),
            pl.BlockSpec(memory_space=pl.ANY)],
        out_specs=tok,
        out_shape=jax.ShapeDtypeStruct((b, s, d), F32),
        scratch_shapes=[pltpu.VMEM((2, TM, d), F32), pltpu.SemaphoreType.DMA((2,))],
        input_output_aliases={2: 0},
        compiler_params=_cparams(("arbitrary", "arbitrary")),
        name="moe_gather_residual",
    )(pos3, pos3, xa, mods, ys)


def _final(xa, ys, pos3, mods, g_final, li, n_lat):
    b, s, d = xa.shape
    nt = s // TM
    nl = n_lat // TM
    tok = pl.BlockSpec((None, TM, d), lambda bb, i: (bb, i, 0))
    return pl.pallas_call(
        _final_kernel,
        grid=(b, nl),
        in_specs=_pos_specs(nt, nl, b) + [
            tok,
            pl.BlockSpec((None, None, None, 6, d), lambda bb, i: (li, bb, 0, 0, 0)),
            pl.BlockSpec((1, d), lambda bb, i: (0, 0)),
            pl.BlockSpec(memory_space=pl.ANY)],
        out_specs=tok,
        out_shape=jax.ShapeDtypeStruct((b, n_lat, d), F32),
        scratch_shapes=[pltpu.VMEM((2, TM, d), F32), pltpu.SemaphoreType.DMA((2,))],
        compiler_params=_cparams(("arbitrary", "arbitrary")),
        name="moe_gather_residual_final_norm",
    )(pos3, pos3, xa, mods, g_final, ys)


def _rope_tables(n_lat, n_ctx, rot_dim, lane_off):
    rows = n_lat // GRID_W
    r = jnp.repeat(jnp.arange(rows, dtype=F32), GRID_W)[:n_lat]
    c = jnp.tile(jnp.arange(GRID_W, dtype=F32), rows)[:n_lat]
    axis_dim = rot_dim // 2
    inv_freq = ROPE_BASE ** (-jnp.arange(0, axis_dim, 2, dtype=F32) / axis_dim)
    ang_r = r[:, None] * inv_freq
    ang_c = c[:, None] * inv_freq
    ang = jnp.concatenate([ang_r, ang_r, ang_c, ang_c], axis=-1)
    cos, sin = jnp.cos(ang), jnp.sin(ang)
    grp = (np.arange(rot_dim) // (rot_dim // 4)) % 2
    sin_a = jnp.where(grp == 0, -sin, 0.0)
    sin_b = jnp.where(grp == 1, sin, 0.0)

    def place(t, fill):
        reps = 2 if rot_dim == 64 else 1
        blk = jnp.full((n_lat, LANE), fill, F32)
        for k in range(reps):
            blk = blk.at[:, lane_off + k * rot_dim:lane_off + (k + 1) * rot_dim].set(t)
        ctx = jnp.full((n_ctx, LANE), fill, F32)
        return jnp.concatenate([blk, ctx], axis=0)

    return place(cos, 1.0), place(sin_a, 0.0), place(sin_b, 0.0)


def _dft_mats(n, scale):
    r = 1
    while r * r < n:
        r *= 2
    k = jnp.arange(n, dtype=jnp.int32)[:, None]
    a = jnp.arange(n // r, dtype=jnp.int32)[None, :]
    c = jnp.arange(r, dtype=jnp.int32)[None, :]
    ang_a = ((k * a * r) % n).astype(F32) * (2.0 * np.pi / n)
    ang_c = ((k * c) % n).astype(F32) * (2.0 * np.pi / n)
    ca, sa = (jnp.cos(ang_a) * scale)[:, :, None], (jnp.sin(ang_a) * scale)[:, :, None]
    cc, sc = jnp.cos(ang_c)[:, None, :], jnp.sin(ang_c)[:, None, :]
    cos = (ca * cc - sa * sc).reshape(n, n)
    sin = (sa * cc + ca * sc).reshape(n, n)
    return cos.astype(BF16), sin.astype(BF16)


def _even_w_in(w):
    d = w.shape[0]
    c3 = 3 * CONV_DIM
    qcols = []
    per_kv = N_Q_HEADS // N_KV_HEADS
    for h in range(N_Q_HEADS):
        wq = w[:, c3 + h * HEAD_DIM:c3 + (h + 1) * HEAD_DIM]
        z = jnp.zeros((d, HEAD_DIM), w.dtype)
        qcols.append(jnp.concatenate([wq, z] if h // per_kv == 0 else [z, wq], axis=1))
    return jnp.concatenate([w[:, :c3]] + qcols + [w[:, c3 + N_Q_HEADS * HEAD_DIM:]], axis=1).astype(BF16)


def _odd_w_in(w):
    d = w.shape[0]
    o = Q_LORA + KV_LORA
    kr = jnp.concatenate([jnp.zeros((d, QK_NOPE), w.dtype), w[:, o:o + QK_ROPE],
                          jnp.zeros((d, LANE - QK_NOPE - QK_ROPE), w.dtype)], axis=1)
    return jnp.concatenate([w[:, :o], kr, w[:, o + QK_ROPE:]], axis=1).astype(BF16)


def _pad_heads(w, per_head, keep_from, keep, heads):
    rows = w.shape[0]
    w3 = w.reshape(rows, heads, per_head)[:, :, keep_from:keep_from + keep]
    w3 = jnp.pad(w3, ((0, 0), (0, 0), (0, LANE - keep)))
    return w3.reshape(rows, heads * LANE).astype(BF16)


def _channel_dft():
    gd = FOURIER_DIM // FOURIER_GROUPS
    k = np.arange(gd)
    ang = 2.0 * np.pi * ((k[:, None] * k[None, :]) % gd) / gd
    eye = np.eye(FOURIER_GROUPS)
    wc = np.kron(eye, np.cos(ang))
    ws = np.kron(eye, np.sin(ang))
    return jnp.asarray(np.concatenate([wc, ws], axis=1), BF16)


def kernel(x, c, ctx, c_ctx, ada_w, ada_b, norm_mix_g, norm_ffn_g, ev_w_in, ev_conv_w, ev_sink, ev_w_out,
           od_w_in, od_q_norm_g, od_w_uq, od_kv_norm_g, od_w_ukv, od_w_out, router_w, router_bias,
           ex_w_gate, ex_w_up, ex_w_down, final_norm_g):
    b, n, d = x.shape
    l = ctx.shape[1]
    s = n + l
    depth = ada_w.shape[0]
    assert l == TM and n % 512 == 0 and b % PAIR == 0

    rows = 8 * ((b + 1 + 7) // 8)
    cvec = jnp.concatenate([c, c_ctx[None, :], jnp.zeros((rows - b - 1, d), F32)], axis=0)
    mod_all = _modulation(cvec, ada_w, ada_b).reshape(depth, rows, 6, d)
    mods = jnp.stack([mod_all[:, :b], jnp.broadcast_to(mod_all[:, b:b + 1], (depth, b, 6, d))], axis=2)

    tabs_even = _rope_tables(n, l, HEAD_DIM, 0)
    tabs_odd = _rope_tables(n, l, QK_ROPE, QK_NOPE)
    gd = FOURIER_DIM // FOURIER_GROUPS
    dft_lat = _dft_mats(n, (n * gd) ** -0.5)
    dft_ctx = _dft_mats(l, (l * gd) ** -0.5)
    wf = _channel_dft()

    g_mix = norm_mix_g.reshape(depth, 1, d)
    g_ffn = norm_ffn_g.reshape(depth, 1, d)
    rw_t = router_w.T.astype(BF16)
    rb = router_bias.reshape(N_EXPERTS, 1).astype(F32)
    wg, wu, wd = ex_w_gate.astype(BF16), ex_w_up.astype(BF16), ex_w_down.astype(BF16)

    xa = jnp.concatenate([x, ctx], axis=1)
    t_pad = (-(-(b * s) // TME) + N_GROUPS * len(PAIRS)) * TME
    hs = jnp.zeros((t_pad, d + LANE), F32)
    out = None
    for li in range(depth):
        j = li // 2
        if li % 2 == 0:
            u, ab, q, k, v = _pre_even(xa, mods, g_mix, _even_w_in(ev_w_in[j]), tabs_even, li)
            ya, yb = _even_mix(u, ab, q, k, v, ev_conv_w[j], ev_sink[j], n)
            w_out = ev_w_out[j].astype(BF16)
        else:
            q, k, v, ycs = _pre_odd(
                xa, mods, g_mix, _odd_w_in(od_w_in[j]),
                od_q_norm_g[j].reshape(1, Q_LORA),
                _pad_heads(od_w_uq[j], QK_NOPE + QK_ROPE, 0, QK_NOPE + QK_ROPE, MLA_HEADS),
                od_kv_norm_g[j].reshape(1, KV_LORA),
                _pad_heads(od_w_ukv[j], QK_NOPE + V_HEAD, 0, QK_NOPE, MLA_HEADS),
                _pad_heads(od_w_ukv[j], QK_NOPE + V_HEAD, QK_NOPE, V_HEAD, MLA_HEADS),
                wf, tabs_odd, li)
            ya = _mla(q, k, v, n)
            yb = _fourier(ycs, dft_lat, dft_ctx, n)
            w_out = od_w_out[j].astype(BF16)
        half = w_out.shape[0] // 2
        xa, hx, rt, cnt = _post(xa, ya, yb, mods, w_out[:half], w_out[half:], g_ffn, rw_t, rb, li)
        pos3, e_lo, e_hi, valid = _route_tables(rt, cnt, t_pad)
        hs = _scatter_rows(hx.reshape(b * s, d + LANE), pos3, hs)
        ys = _gmm(hs, e_lo, e_hi, valid, wg, wu, wd, li)
        if li == depth - 1:
            out = _final(xa, ys, pos3, mods, final_norm_g.reshape(1, d), li, n)
        else:
            xa = _combine(xa, ys, pos3, mods, li)
    return out
```

```python
import functools

import jax
import jax.numpy as jnp
import numpy as np
from jax import lax
from jax.experimental import pallas as pl
from jax.experimental.pallas import tpu as pltpu

F32 = jnp.float32
BF16 = jnp.bfloat16

EPS = 1e-6
ROPE_BASE = 10000.0
GRID_W = 64
NEG_INF = -1e30
LOG2E = 1.4426950408889634
LANE = 128
TM = 256
WIN = 128
N_Q_HEADS = 8
N_KV_HEADS = 2
HEAD_DIM = 64
CONV_DIM = 512
MLA_HEADS = 8
Q_LORA = 256
KV_LORA = 128
QK_NOPE = 64
QK_ROPE = 32
V_HEAD = 64
FOURIER_DIM = 512
FOURIER_GROUPS = 4
N_EXPERTS = 16
N_GROUPS = 4
EXPERTS_PER_GROUP = 4
PAIRS = ((0, 1), (0, 2), (0, 3), (1, 2), (1, 3), (2, 3))
NB_PAD = 32
TME = 512
DMA_QUEUES = 2
TQ_MLA = 512
PAIR = 2
VMEM_LIMIT = 56 * 1024 * 1024


def _cparams(sem):
    return pltpu.CompilerParams(dimension_semantics=sem, vmem_limit_bytes=VMEM_LIMIT)


def _norm_mod(x, g, shift, scale):
    var = jnp.mean(x * x, axis=-1, keepdims=True)
    return (x * lax.rsqrt(var + EPS)) * g * (1.0 + scale) + shift


def _rope(x, cos, sin_a, sin_b, shift):
    w = x.shape[-1]
    return x * cos + pltpu.roll(x, w - shift, 1) * sin_a + pltpu.roll(x, shift, 1) * sin_b


def _mod_kernel(c_ref, w_ref, b_ref, o_ref):
    c = c_ref[...]
    o_ref[...] = jnp.dot(jax.nn.silu(c), w_ref[...], preferred_element_type=F32,
                         precision=lax.Precision.HIGHEST) + b_ref[...]


def _modulation(cvec, ada_w, ada_b):
    depth, d, d6 = ada_w.shape
    tn = d6 // 4
    rows = cvec.shape[0]
    return pl.pallas_call(
        _mod_kernel,
        grid=(depth, d6 // tn),
        in_specs=[pl.BlockSpec((rows, d), lambda l, j: (0, 0)),
                  pl.BlockSpec((None, d, tn), lambda l, j: (l, 0, j)),
                  pl.BlockSpec((None, 1, tn), lambda l, j: (l, 0, j))],
        out_specs=pl.BlockSpec((None, rows, tn), lambda l, j: (l, 0, j)),
        out_shape=jax.ShapeDtypeStruct((depth, rows, d6), F32),
        compiler_params=_cparams(("arbitrary", "arbitrary")),
        name="adaln_modulation",
    )(cvec, ada_w, ada_b.reshape(depth, 1, d6))


def _pre_even_kernel(x_ref, mod_ref, g_ref, w_ref, cos_ref, sa_ref, sb_ref,
                     u_ref, ab_ref, q_ref, k_ref, v_ref):
    g = g_ref[...]
    h = jnp.concatenate([_norm_mod(x_ref[j], g, mod_ref[j, 0:1], mod_ref[j, 1:2]) for j in range(PAIR)], axis=0)
    p = jnp.dot(h.astype(BF16), w_ref[...], preferred_element_type=F32)
    c = CONV_DIM
    ax, ab, ac = p[:, 0:c], p[:, c:2 * c], p[:, 2 * c:3 * c]
    qw = N_Q_HEADS * LANE
    q = p[:, 3 * c:3 * c + qw]
    k = p[:, 3 * c + qw:3 * c + qw + LANE]
    v = p[:, 3 * c + qw + LANE:3 * c + qw + 2 * LANE]
    cos, sa, sb = (jnp.concatenate([t[...]] * PAIR, axis=0) for t in (cos_ref, sa_ref, sb_ref))
    cos8 = jnp.concatenate([cos] * N_Q_HEADS, axis=1)
    sa8 = jnp.concatenate([sa] * N_Q_HEADS, axis=1)
    sb8 = jnp.concatenate([sb] * N_Q_HEADS, axis=1)
    split = lambda a: a.reshape(PAIR, TM, a.shape[-1]).astype(BF16)
    u_ref[...] = split(ac * ax)
    ab_ref[...] = split(ab)
    q_ref[...] = split(_rope(q, cos8, sa8, sb8, 16) * (HEAD_DIM ** -0.5 * LOG2E))
    k_ref[...] = split(_rope(k, cos, sa, sb, 16))
    v_ref[...] = split(v)


def _pre_even(xa, mods, g_mix, w_in, tabs, li):
    b, s, d = xa.shape
    nt = s // TM
    nl = nt - 1
    wcols = w_in.shape[1]
    qw = N_Q_HEADS * LANE
    tab_spec = pl.BlockSpec((TM, LANE), lambda i, bb: (i, 0))
    tok = lambda w: pl.BlockSpec((PAIR, TM, w), lambda i, bb: (bb, i, 0))
    return pl.pallas_call(
        _pre_even_kernel,
        grid=(nt, b // PAIR),
        in_specs=[tok(d),
                  pl.BlockSpec((None, PAIR, None, 6, d), lambda i, bb: (li, bb, i // nl, 0, 0)),
                  pl.BlockSpec((None, 1, d), lambda i, bb: (li, 0, 0)),
                  pl.BlockSpec((d, wcols), lambda i, bb: (0, 0)),
                  tab_spec, tab_spec, tab_spec],
        out_specs=[tok(CONV_DIM), tok(CONV_DIM), tok(qw), tok(LANE), tok(LANE)],
        out_shape=[jax.ShapeDtypeStruct((b, s, CONV_DIM), BF16),
                   jax.ShapeDtypeStruct((b, s, CONV_DIM), BF16),
                   jax.ShapeDtypeStruct((b, s, qw), BF16),
                   jax.ShapeDtypeStruct((b, s, LANE), BF16),
                   jax.ShapeDtypeStruct((b, s, LANE), BF16)],
        compiler_params=_cparams(("arbitrary", "arbitrary")),
        name="even_in_proj",
    )(xa, mods, g_mix, w_in, *tabs)


def _even_mix_kernel(sink_ref, q_ref, k_ref, v_ref,
                     u_ref, up_ref, un_ref, ab_ref, cw_ref, yc_ref, ya_ref, *, nl, n_lat):
    i = pl.program_id(1)
    nt = pl.num_programs(1)
    is_lat = i < nl
    u = u_ref[...].astype(F32)
    row = lax.broadcasted_iota(jnp.int32, u.shape, 0)
    first = jnp.logical_or(i == 0, i == nl)
    last = jnp.logical_or(i == nl - 1, i == nt - 1)
    prev_row = jnp.where(first, 0.0, up_ref[15:16, :].astype(F32))
    next_row = jnp.where(last, 0.0, un_ref[0:1, :].astype(F32))
    u_prev = jnp.where(row == 0, prev_row, pltpu.roll(u, 1, 0))
    u_next = jnp.where(row == TM - 1, next_row, pltpu.roll(u, TM - 1, 0))
    cw = cw_ref[...]
    conv = u_prev * cw[0:1] + u * cw[1:2] + u_next * cw[2:3]
    yc_ref[...] = (ab_ref[...].astype(F32) * conv).astype(BF16)

    s_all = k_ref.shape[0]
    nwin = TM + 2 * WIN
    w0 = pl.multiple_of(jnp.clip(i * TM - WIN, 0, s_all - nwin), WIN)
    k_all = jnp.concatenate([k_ref[pl.ds(w0, nwin), :], k_ref[n_lat:, :]], axis=0)
    v_all = jnp.concatenate([v_ref[pl.ds(w0, nwin), :], v_ref[n_lat:, :]], axis=0)
    nk = k_all.shape[0]
    qpos = i * TM + lax.broadcasted_iota(jnp.int32, (TM, nk), 0)
    kj = lax.broadcasted_iota(jnp.int32, (TM, nk), 1)
    kpos = w0 + kj
    valid_win = (jnp.abs(kpos - qpos) <= WIN) & (kpos < n_lat) & is_lat
    bias = jnp.where((kj >= nwin) | valid_win, 0.0, NEG_INF).astype(F32)
    lane = lax.broadcasted_iota(jnp.int32, (TM, LANE), 1)
    outs = []
    for pair in range(N_Q_HEADS // 2):
        kvh = (2 * pair) // (N_Q_HEADS // N_KV_HEADS)
        o_pair = []
        for h in (2 * pair, 2 * pair + 1):
            q = q_ref[:, h * LANE:(h + 1) * LANE]
            s = lax.dot_general(q, k_all, (((1,), (1,)), ((), ())), preferred_element_type=F32) + bias
            sink = sink_ref[h] * LOG2E
            m = jnp.maximum(jnp.max(s, axis=1, keepdims=True), sink)
            p = jnp.exp2(s - m)
            l = jnp.sum(p, axis=1, keepdims=True) + jnp.exp2(sink - m)
            o = jnp.dot(p.astype(BF16), v_all, preferred_element_type=F32) / l
            if kvh == 1:
                o = pltpu.roll(o, HEAD_DIM, 1)
            o_pair.append(o)
        outs.append(jnp.where(lane < HEAD_DIM, o_pair[0], pltpu.roll(o_pair[1], HEAD_DIM, 1)))
    ya_ref[...] = jnp.concatenate(outs, axis=1).astype(BF16)


def _even_mix(u, ab, q, k, v, conv_w, sink, n_lat):
    b, s, _ = u.shape
    nt = s // TM
    nl = n_lat // TM
    qw = q.shape[-1]
    whole = pl.BlockSpec((None, s, LANE), lambda bb, i: (bb, 0, 0))
    halo = TM // 16
    u16 = u.reshape(b, s // 16, 16, CONV_DIM)
    kern = functools.partial(_even_mix_kernel, nl=nl, n_lat=n_lat)
    tok = lambda w: pl.BlockSpec((None, TM, w), lambda bb, i: (bb, i, 0))
    return pl.pallas_call(
        kern,
        grid=(b, nt),
        in_specs=[pl.BlockSpec(memory_space=pltpu.SMEM),
                  tok(qw), whole, whole,
                  tok(CONV_DIM),
                  pl.BlockSpec((None, None, 16, CONV_DIM),
                               lambda bb, i: (bb, jnp.maximum(i * halo - 1, 0), 0, 0)),
                  pl.BlockSpec((None, None, 16, CONV_DIM),
                               lambda bb, i: (bb, jnp.minimum((i + 1) * halo, s // 16 - 1), 0, 0)),
                  tok(CONV_DIM),
                  pl.BlockSpec((3, CONV_DIM), lambda bb, i: (0, 0))],
        out_specs=[tok(CONV_DIM), tok(N_Q_HEADS * HEAD_DIM)],
        out_shape=[jax.ShapeDtypeStruct((b, s, CONV_DIM), BF16),
                   jax.ShapeDtypeStruct((b, s, N_Q_HEADS * HEAD_DIM), BF16)],
        compiler_params=_cparams(("arbitrary", "arbitrary")),
        name="even_conv_window_attn",
    )(sink, q, k, v, u, u16, u16, ab, conv_w)


def _pre_odd_kernel(x_ref, mod_ref, g_ref, w_ref, gq_ref, wuq_ref, gkv_ref, wuk_ref, wuv_ref,
                    wf_ref, cos_ref, sa_ref, sb_ref, q_ref, k_ref, v_ref, y_ref):
    g = g_ref[...]
    h = jnp.concatenate([_norm_mod(x_ref[j], g, mod_ref[j, 0:1], mod_ref[j, 1:2]) for j in range(PAIR)], axis=0)
    p = jnp.dot(h.astype(BF16), w_ref[...], preferred_element_type=F32)
    cq = p[:, 0:Q_LORA]
    ckv = p[:, Q_LORA:Q_LORA + KV_LORA]
    kr = p[:, Q_LORA + KV_LORA:Q_LORA + KV_LORA + LANE]
    f = p[:, Q_LORA + KV_LORA + LANE:]
    cos, sa, sb = (jnp.concatenate([t[...]] * PAIR, axis=0) for t in (cos_ref, sa_ref, sb_ref))
    split = lambda a: a.reshape(PAIR, TM, a.shape[-1]).astype(BF16)
    cos8 = jnp.concatenate([cos] * MLA_HEADS, axis=1)
    sa8 = jnp.concatenate([sa] * MLA_HEADS, axis=1)
    sb8 = jnp.concatenate([sb] * MLA_HEADS, axis=1)

    cqn = cq * lax.rsqrt(jnp.mean(cq * cq, axis=-1, keepdims=True) + EPS) * gq_ref[...]
    q = jnp.dot(cqn.astype(BF16), wuq_ref[...], preferred_element_type=F32)
    q = _rope(q, cos8, sa8, sb8, 8) * ((QK_NOPE + QK_ROPE) ** -0.5 * LOG2E)
    q_ref[...] = split(q)

    ckvn = (ckv * lax.rsqrt(jnp.mean(ckv * ckv, axis=-1, keepdims=True) + EPS) * gkv_ref[...]).astype(BF16)
    kn = jnp.dot(ckvn, wuk_ref[...], preferred_element_type=F32)
    krr = _rope(kr, cos, sa, sb, 8)
    k_ref[...] = split(kn + jnp.concatenate([krr] * MLA_HEADS, axis=1))
    v = jnp.dot(ckvn, wuv_ref[...], preferred_element_type=F32)
    one_lane = lax.broadcasted_iota(jnp.int32, v.shape, 1) % LANE == V_HEAD
    v_ref[...] = split(jnp.where(one_lane, 1.0, v))
    y_ref[...] = split(jnp.dot(f.astype(BF16), wf_ref[...], preferred_element_type=F32))


def _pre_odd(xa, mods, g_mix, w_in, gq, wuq, gkv, wuk, wuv, wf, tabs, li):
    b, s, d = xa.shape
    nt = s // TM
    nl = nt - 1
    hw = MLA_HEADS * LANE
    vw = hw
    tab_spec = pl.BlockSpec((TM, LANE), lambda i, bb: (i, 0))
    tok = lambda w: pl.BlockSpec((PAIR, TM, w), lambda i, bb: (bb, i, 0))
    full = lambda a: pl.BlockSpec(a.shape, lambda i, bb: (0,) * a.ndim)
    return pl.pallas_call(
        _pre_odd_kernel,
        grid=(nt, b // PAIR),
        in_specs=[tok(d),
                  pl.BlockSpec((None, PAIR, None, 6, d), lambda i, bb: (li, bb, i // nl, 0, 0)),
                  pl.BlockSpec((None, 1, d), lambda i, bb: (li, 0, 0)),
                  full(w_in), full(gq), full(wuq), full(gkv), full(wuk), full(wuv), full(wf),
                  tab_spec, tab_spec, tab_spec],
        out_specs=[tok(hw), tok(hw), tok(vw), tok(2 * FOURIER_DIM)],
        out_shape=[jax.ShapeDtypeStruct((b, s, hw), BF16),
                   jax.ShapeDtypeStruct((b, s, hw), BF16),
                   jax.ShapeDtypeStruct((b, s, vw), BF16),
                   jax.ShapeDtypeStruct((b, s, 2 * FOURIER_DIM), BF16)],
        compiler_params=_cparams(("arbitrary", "arbitrary")),
        name="odd_in_proj",
    )(xa, mods, g_mix, w_in, gq, wuq, gkv, wuk, wuv, wf, *tabs)


def _mla_kernel(q_ref, k_ref, v_ref, o_ref):
    lane = lax.broadcasted_iota(jnp.int32, (q_ref.shape[0], LANE), 1)
    outs = []
    for pair in range(MLA_HEADS // 2):
        o_pair = []
        for h in (2 * pair, 2 * pair + 1):
            q = q_ref[:, h * LANE:(h + 1) * LANE]
            k = k_ref[:, h * LANE:(h + 1) * LANE]
            v = v_ref[:, h * LANE:(h + 1) * LANE]
            s = lax.dot_general(q, k, (((1,), (1,)), ((), ())), preferred_element_type=F32)
            m = jnp.max(s, axis=1, keepdims=True)
            p = jnp.exp2(s - m)
            o = jnp.dot(p.astype(BF16), v, preferred_element_type=F32)
            o_pair.append(o / o[:, V_HEAD:V_HEAD + 1])
        outs.append(jnp.where(lane < V_HEAD, o_pair[0], pltpu.roll(o_pair[1], V_HEAD, 1)))
    o_ref[...] = jnp.concatenate(outs, axis=1).astype(BF16)


def _mla(q, k, v, n_lat):
    b, s, hw = q.shape
    vw = MLA_HEADS * V_HEAD
    n_ctx = s - n_lat
    tq = TQ_MLA if n_lat % TQ_MLA == 0 else TM
    once = pl.Buffered(1)
    lat = pl.pallas_call(
        _mla_kernel,
        grid=(b, n_lat // tq),
        in_specs=[pl.BlockSpec((None, tq, hw), lambda bb, i: (bb, i, 0)),
                  pl.BlockSpec((None, s, hw), lambda bb, i: (bb, 0, 0), pipeline_mode=once),
                  pl.BlockSpec((None, s, hw), lambda bb, i: (bb, 0, 0), pipeline_mode=once)],
        out_specs=pl.BlockSpec((None, tq, vw), lambda bb, i: (bb, i, 0)),
        out_shape=jax.ShapeDtypeStruct((b, n_lat, vw), BF16),
        compiler_params=_cparams(("arbitrary", "arbitrary")),
        name="mla_attention",
    )(q, k, v)
    cblk = n_lat // n_ctx
    ctx_spec = pl.BlockSpec((None, n_ctx, hw), lambda bb: (bb, cblk, 0))
    ctx = pl.pallas_call(
        _mla_kernel,
        grid=(b,),
        in_specs=[ctx_spec, ctx_spec, ctx_spec],
        out_specs=pl.BlockSpec((None, n_ctx, vw), lambda bb: (bb, 0, 0)),
        out_shape=jax.ShapeDtypeStruct((b, n_ctx, vw), BF16),
        compiler_params=_cparams(("arbitrary",)),
        name="mla_attention_context",
    )(q, k, v)
    return jnp.concatenate([lat, ctx], axis=1)


def _dft_kernel(ac_ref, as_ref, cc_ref, cs_ref, yc_ref, ys_ref, ycc_ref, ysc_ref, o_ref, *, nl):
    i = pl.program_id(1)

    @pl.when(i < nl)
    def _():
        o_ref[...] = (jnp.dot(ac_ref[...], yc_ref[...], preferred_element_type=F32)
                      - jnp.dot(as_ref[...], ys_ref[...], preferred_element_type=F32)).astype(BF16)

    @pl.when(i >= nl)
    def _():
        o_ref[...] = (jnp.dot(cc_ref[...], ycc_ref[...], preferred_element_type=F32)
                      - jnp.dot(cs_ref[...], ysc_ref[...], preferred_element_type=F32)).astype(BF16)


def _fourier(ycs, dft_lat, dft_ctx, n_lat):
    b, s, _ = ycs.shape
    fd = FOURIER_DIM
    n_ctx = s - n_lat
    nl = n_lat // TM
    ac, as_ = dft_lat
    cc, cs = dft_ctx
    rows = pl.BlockSpec((TM, n_lat), lambda bb, i: (jnp.minimum(i, nl - 1), 0))
    small = pl.BlockSpec((n_ctx, n_ctx), lambda bb, i: (0, 0))
    return pl.pallas_call(
        functools.partial(_dft_kernel, nl=nl),
        grid=(b, s // TM),
        in_specs=[rows, rows, small, small,
                  pl.BlockSpec((None, n_lat, fd), lambda bb, i: (bb, 0, 0)),
                  pl.BlockSpec((None, n_lat, fd), lambda bb, i: (bb, 0, 1)),
                  pl.BlockSpec((None, n_ctx, fd), lambda bb, i: (bb, n_lat // n_ctx, 0)),
                  pl.BlockSpec((None, n_ctx, fd), lambda bb, i: (bb, n_lat // n_ctx, 1))],
        out_specs=pl.BlockSpec((None, TM, fd), lambda bb, i: (bb, i, 0)),
        out_shape=jax.ShapeDtypeStruct((b, s, fd), BF16),
        compiler_params=_cparams(("arbitrary", "arbitrary")),
        name="fourier_mix",
    )(ac, as_, cc, cs, ycs, ycs, ycs, ycs)


def _post_kernel(x_ref, ya_ref, yb_ref, mod_ref, wa_ref, wb_ref, g_ref, rw_ref, rb_ref,
                 xo_ref, hx_ref, rt_ref, cnt_ref, carry_ref):
    @pl.when((pl.program_id(0) == 0) & (pl.program_id(1) == 0))
    def _():
        carry_ref[...] = jnp.zeros_like(carry_ref)

    mod = mod_ref[...]
    y = (jnp.dot(ya_ref[...], wa_ref[...], preferred_element_type=F32)
         + jnp.dot(yb_ref[...], wb_ref[...], preferred_element_type=F32))
    x = x_ref[...] + mod[2:3] * y
    xo_ref[...] = x
    h2f = _norm_mod(x, g_ref[...], mod[3:4], mod[4:5])
    h2 = h2f.astype(BF16)
    logits = lax.dot_general(rw_ref[...], h2, (((1,), (1,)), ((), ())), preferred_element_type=F32)
    scores = jax.nn.sigmoid(logits)
    biased = scores + rb_ref[...]
    sc = [scores[e:e + 1, :] for e in range(N_EXPERTS)]
    bs = [biased[e:e + 1, :] for e in range(N_EXPERTS)]
    sel, gval = [], []
    for g in range(N_GROUPS):
        ids = range(g * EXPERTS_PER_GROUP, (g + 1) * EXPERTS_PER_GROUP)
        gsum = None
        for e in ids:
            rank = sum(((bs[j] > bs[e]) if j > e else (bs[j] >= bs[e])).astype(jnp.int32)
                       for j in ids if j != e)
            s_e = rank < 2
            sel.append(s_e)
            term = jnp.where(s_e, bs[e], 0.0)
            gsum = term if gsum is None else gsum + term
        gval.append(gsum)
    gsel = []
    for g in range(N_GROUPS):
        ok = None
        for j in range(N_GROUPS):
            if j == g:
                continue
            c = (gval[g] > gval[j]) if j < g else (gval[g] >= gval[j])
            ok = c if ok is None else (ok & c)
        gsel.append(ok)
    chosen = [sel[e] & gsel[e // EXPERTS_PER_GROUP] for e in range(N_EXPERTS)]
    wsum = sum(jnp.where(chosen[e], sc[e], 0.0) for e in range(N_EXPERTS))
    gates = [jnp.where(chosen[e], sc[e], 0.0) / wsum for e in range(N_EXPERTS)]
    bucket = jnp.zeros_like(wsum)
    w_lo = jnp.zeros_like(wsum)
    w_hi = jnp.zeros_like(wsum)
    for g in range(N_GROUPS):
        for pi, (ea, eb) in enumerate(PAIRS):
            ea, eb = g * EXPERTS_PER_GROUP + ea, g * EXPERTS_PER_GROUP + eb
            cond = chosen[ea] & chosen[eb]
            bucket = jnp.where(cond, float(g * len(PAIRS) + pi), bucket)
            w_lo = jnp.where(cond, gates[ea], w_lo)
            w_hi = jnp.where(cond, gates[eb], w_hi)
    kio = lax.broadcasted_iota(jnp.int32, (NB_PAD, TM), 0).astype(F32)
    onehot = (kio == bucket).astype(F32)
    tri = (lax.broadcasted_iota(jnp.int32, (TM, TM), 0)
           <= lax.broadcasted_iota(jnp.int32, (TM, TM), 1)).astype(BF16)
    prefix = jnp.dot(onehot.astype(BF16), tri, preferred_element_type=F32)
    carry = carry_ref[...]
    rank = jnp.sum(onehot * (prefix + carry[:, 0:1]), axis=0, keepdims=True) - 1.0
    carry = carry + jnp.sum(onehot, axis=1, keepdims=True)
    carry_ref[...] = carry
    cnt_ref[...] = carry
    zrow = jnp.zeros_like(wsum)
    rt_ref[...] = jnp.concatenate([bucket, rank] + [zrow] * 6, axis=0)
    ext_t = jnp.concatenate([w_lo, w_hi] + [zrow] * (LANE - 2), axis=0)
    hx_ref[...] = jnp.concatenate([h2f, ext_t.T], axis=1)


def _post(xa, ya, yb, mods, w_out_a, w_out_b, g_ffn, rw_t, rb, li):
    b, s, d = xa.shape
    nt = s // TM
    nl = nt - 1
    half = ya.shape[-1]
    tok = lambda w: pl.BlockSpec((None, TM, w), lambda bb, i: (bb, i, 0))
    return pl.pallas_call(
        _post_kernel,
        grid=(b, nt),
        in_specs=[tok(d), tok(half), tok(half),
                  pl.BlockSpec((None, None, None, 6, d), lambda bb, i: (li, bb, i // nl, 0, 0)),
                  pl.BlockSpec((half, d), lambda bb, i: (0, 0)),
                  pl.BlockSpec((half, d), lambda bb, i: (0, 0)),
                  pl.BlockSpec((None, 1, d), lambda bb, i: (li, 0, 0)),
                  pl.BlockSpec((N_EXPERTS, d), lambda bb, i: (0, 0)),
                  pl.BlockSpec((N_EXPERTS, 1), lambda bb, i: (0, 0))],
        out_specs=[tok(d), tok(d + LANE),
                   pl.BlockSpec((None, 8, TM), lambda bb, i: (bb, 0, i)),
                   pl.BlockSpec((NB_PAD, LANE), lambda bb, i: (0, 0))],
        out_shape=[jax.ShapeDtypeStruct((b, s, d), F32),
                   jax.ShapeDtypeStruct((b, s, d + LANE), F32),
                   jax.ShapeDtypeStruct((b, 8, s), F32),
                   jax.ShapeDtypeStruct((NB_PAD, LANE), F32)],
        scratch_shapes=[pltpu.VMEM((NB_PAD, LANE), F32)],
        input_output_aliases={0: 0},
        compiler_params=_cparams(("arbitrary", "arbitrary")),
        name="out_proj_router",
    )(xa, ya, yb, mods, w_out_a, w_out_b, g_ffn, rw_t, rb)


def _scatter_kernel(pos_ref, h_ref, init_ref, o_ref, sem):
    del init_ref

    def body(r2, c):
        for prio in range(DMA_QUEUES):
            r = r2 * DMA_QUEUES + prio
            p = pos_ref[0, r]
            pltpu.make_async_copy(h_ref.at[pl.ds(r, 1), :], o_ref.at[pl.ds(p, 1), :], sem).start(priority=prio)
        return c

    lax.fori_loop(0, TM // DMA_QUEUES, body, 0, unroll=4)
    pltpu.make_async_copy(h_ref, o_ref.at[pl.ds(0, TM), :], sem).wait()


def _scatter_rows(hx, pos3, init):
    t, w = hx.shape
    t_pad = init.shape[0]
    return pl.pallas_call(
        _scatter_kernel,
        grid=(t // TM,),
        in_specs=[pl.BlockSpec((None, 1, TM), lambda i: (i, 0, 0), memory_space=pltpu.SMEM),
                  pl.BlockSpec((TM, w), lambda i: (i, 0)),
                  pl.BlockSpec(memory_space=pl.ANY)],
        out_specs=pl.BlockSpec(memory_space=pl.ANY),
        out_shape=jax.ShapeDtypeStruct((t_pad, w), F32),
        scratch_shapes=[pltpu.SemaphoreType.DMA(())],
        input_output_aliases={2: 0},
        compiler_params=_cparams(("arbitrary",)),
        name="moe_scatter_rows",
    )(pos3, hx, init)


def _gmm_kernel(elo_ref, ehi_ref, valid_ref, hs_ref, wg0, wu0, wd0, wg1, wu1, wd1, o_ref):
    del elo_ref, ehi_ref
    j = pl.program_id(0)
    d = o_ref.shape[-1]

    @pl.when(valid_ref[j] == 1)
    def _():
        hx = hs_ref[...]
        h = hx[:, :d].astype(BF16)

        def ffn(wg, wu, wd):
            a = jnp.dot(h, wg[...], preferred_element_type=F32)
            u = jnp.dot(h, wu[...], preferred_element_type=F32)
            act = (jax.nn.silu(a) * u).astype(BF16)
            return jnp.dot(act, wd[...], preferred_element_type=F32)

        o_ref[...] = hx[:, d:d + 1] * ffn(wg0, wu0, wd0) + hx[:, d + 1:d + 2] * ffn(wg1, wu1, wd1)

    @pl.when(valid_ref[j] == 0)
    def _():
        o_ref[...] = jnp.zeros_like(o_ref)


def _gmm(hs, e_lo, e_hi, valid, wg, wu, wd, li):
    t_pad, w = hs.shape
    d = w - LANE
    dff = wg.shape[-1]
    up = lambda sel: pl.BlockSpec((None, None, d, dff), lambda j, lo, hi, v: (li, sel(lo, hi)[j], 0, 0))
    dn = lambda sel: pl.BlockSpec((None, None, dff, d), lambda j, lo, hi, v: (li, sel(lo, hi)[j], 0, 0))
    first = lambda lo, hi: lo
    second = lambda lo, hi: hi
    return pl.pallas_call(
        _gmm_kernel,
        grid_spec=pltpu.PrefetchScalarGridSpec(
            num_scalar_prefetch=3,
            grid=(t_pad // TME,),
            in_specs=[pl.BlockSpec((TME, w), lambda j, lo, hi, v: (j, 0)),
                      up(first), up(first), dn(first), up(second), up(second), dn(second)],
            out_specs=pl.BlockSpec((TME, d), lambda j, lo, hi, v: (j, 0))),
        out_shape=jax.ShapeDtypeStruct((t_pad, d), F32),
        compiler_params=_cparams(("arbitrary",)),
        name="moe_grouped_ffn",
    )(e_lo, e_hi, valid, hs, wg, wu, wd, wg, wu, wd)


def _route_tables(rt, cnt, t_pad):
    b, _, s = rt.shape
    nb = N_GROUPS * len(PAIRS)
    bucket = rt[:, 0, :].astype(jnp.int32)
    rank = rt[:, 1, :].astype(jnp.int32)
    counts = cnt[:nb, 0].astype(jnp.int32)
    padded = ((counts + TME - 1) // TME) * TME
    ids = jnp.arange(nb, dtype=jnp.int32)
    ends = jnp.sum(jnp.where(ids[None, :] <= ids[:, None], padded[None, :], 0), axis=1)
    starts = ends - padded
    start_of = jnp.sum(jnp.where(bucket[..., None] == ids, starts, 0), axis=-1)
    pos = (start_of + rank).reshape(b * s // TM, 1, TM)
    n_tiles = t_pad // TME
    n_used = ends[-1] // TME
    tile = jnp.arange(n_tiles, dtype=jnp.int32)
    valid = (tile < n_used).astype(jnp.int32)
    tile_c = jnp.minimum(tile, n_used - 1)
    tb = jnp.minimum(jnp.sum((ends[None, :] <= tile_c[:, None] * TME).astype(jnp.int32), axis=1), nb - 1)
    e_lo_of = jnp.asarray([g * EXPERTS_PER_GROUP + p[0] for g in range(N_GROUPS) for p in PAIRS], jnp.int32)
    e_hi_of = jnp.asarray([g * EXPERTS_PER_GROUP + p[1] for g in range(N_GROUPS) for p in PAIRS], jnp.int32)
    hit = tb[:, None] == ids
    e_lo = jnp.sum(jnp.where(hit, e_lo_of, 0), axis=1)
    e_hi = jnp.sum(jnp.where(hit, e_hi_of, 0), axis=1)
    return pos, e_lo, e_hi, valid


def _issue_gather(pos_ref, y_ref, buf_ref, sem, slot):
    def body(r2, c):
        for prio in range(DMA_QUEUES):
            r = r2 * DMA_QUEUES + prio
            p = pos_ref[0, r]
            pltpu.make_async_copy(y_ref.at[pl.ds(p, 1), :], buf_ref.at[slot, pl.ds(r, 1), :],
                                  sem.at[slot]).start(priority=prio)
        return c

    lax.fori_loop(0, TM // DMA_QUEUES, body, 0, unroll=4)


def _gather_tile(pos_ref, posn_ref, y_ref, buf_ref, sem):
    ni = pl.num_programs(1)
    lin = pl.program_id(0) * ni + pl.program_id(1)
    slot = lin % 2

    @pl.when(lin == 0)
    def _():
        _issue_gather(pos_ref, y_ref, buf_ref, sem, slot)

    @pl.when(lin + 1 < pl.num_programs(0) * ni)
    def _():
        _issue_gather(posn_ref, y_ref, buf_ref, sem, 1 - slot)

    pltpu.make_async_copy(y_ref.at[pl.ds(0, TM), :], buf_ref.at[slot], sem.at[slot]).wait()
    return buf_ref[slot]


def _combine_kernel(pos_ref, posn_ref, x_ref, mod_ref, y_ref, xo_ref, buf_ref, sem):
    f = _gather_tile(pos_ref, posn_ref, y_ref, buf_ref, sem)
    mod = mod_ref[...]
    xo_ref[...] = x_ref[...] + mod[5:6] * f


def _final_kernel(pos_ref, posn_ref, x_ref, mod_ref, g_ref, y_ref, o_ref, buf_ref, sem):
    f = _gather_tile(pos_ref, posn_ref, y_ref, buf_ref, sem)
    mod = mod_ref[...]
    x = x_ref[...] + mod[5:6] * f
    var = jnp.mean(x * x, axis=-1, keepdims=True)
    o_ref[...] = (x * lax.rsqrt(var + EPS)) * g_ref[...]


def _pos_specs(nt, ni, nb):
    def cur(bb, i):
        return (bb * nt + i, 0, 0)

    def nxt(bb, i):
        wrap = i + 1 == ni
        b2 = jnp.minimum(jnp.where(wrap, bb + 1, bb), nb - 1)
        i2 = jnp.where(wrap, jnp.where(bb + 1 == nb, i, 0), i + 1)
        return (b2 * nt + i2, 0, 0)

    return [pl.BlockSpec((None, 1, TM), cur, memory_space=pltpu.SMEM),
            pl.BlockSpec((None, 1, TM), nxt, memory_space=pltpu.SMEM)]


def _combine(xa, ys, pos3, mods, li):
    b, s, d = xa.shape
    nt = s // TM
    nl = nt - 1
    tok = pl.BlockSpec((None, TM, d), lambda bb, i: (bb, i, 0))
    return pl.pallas_call(
        _combine_kernel,
        grid=(b, nt),
        in_specs=_pos_specs(nt, nt, b) + [
            tok,
            pl.BlockSpec((None, None, None, 6, d), lambda bb, i: (li, bb, i // nl, 0, 0)),
            pl.BlockSpec(memory_space=pl.ANY)],
        out_specs=tok,
        out_shape=jax.ShapeDtypeStruct((b, s, d), F32),
        scratch_shapes=[pltpu.VMEM((2, TM, d), F32), pltpu.SemaphoreType.DMA((2,))],
        input_output_aliases={2: 0},
        compiler_params=_cparams(("arbitrary", "arbitrary")),
        name="moe_gather_residual",
    )(pos3, pos3, xa, mods, ys)


def _final(xa, ys, pos3, mods, g_final, li, n_lat):
    b, s, d = xa.shape
    nt = s // TM
    nl = n_lat // TM
    tok = pl.BlockSpec((None, TM, d), lambda bb, i: (bb, i, 0))
    return pl.pallas_call(
        _final_kernel,
        grid=(b, nl),
        in_specs=_pos_specs(nt, nl, b) + [
            tok,
            pl.BlockSpec((None, None, None, 6, d), lambda bb, i: (li, bb, 0, 0, 0)),
            pl.BlockSpec((1, d), lambda bb, i: (0, 0)),
            pl.BlockSpec(memory_space=pl.ANY)],
        out_specs=tok,
        out_shape=jax.ShapeDtypeStruct((b, n_lat, d), F32),
        scratch_shapes=[pltpu.VMEM((2, TM, d), F32), pltpu.SemaphoreType.DMA((2,))],
        compiler_params=_cparams(("arbitrary", "arbitrary")),
        name="moe_gather_residual_final_norm",
    )(pos3, pos3, xa, mods, g_final, ys)


def _rope_tables(n_lat, n_ctx, rot_dim, lane_off):
    rows = n_lat // GRID_W
    r = jnp.repeat(jnp.arange(rows, dtype=F32), GRID_W)[:n_lat]
    c = jnp.tile(jnp.arange(GRID_W, dtype=F32), rows)[:n_lat]
    axis_dim = rot_dim // 2
    inv_freq = ROPE_BASE ** (-jnp.arange(0, axis_dim, 2, dtype=F32) / axis_dim)
    ang_r = r[:, None] * inv_freq
    ang_c = c[:, None] * inv_freq
    ang = jnp.concatenate([ang_r, ang_r, ang_c, ang_c], axis=-1)
    cos, sin = jnp.cos(ang), jnp.sin(ang)
    grp = (np.arange(rot_dim) // (rot_dim // 4)) % 2
    sin_a = jnp.where(grp == 0, -sin, 0.0)
    sin_b = jnp.where(grp == 1, sin, 0.0)

    def place(t, fill):
        reps = 2 if rot_dim == 64 else 1
        blk = jnp.full((n_lat, LANE), fill, F32)
        for k in range(reps):
            blk = blk.at[:, lane_off + k * rot_dim:lane_off + (k + 1) * rot_dim].set(t)
        ctx = jnp.full((n_ctx, LANE), fill, F32)
        return jnp.concatenate([blk, ctx], axis=0)

    return place(cos, 1.0), place(sin_a, 0.0), place(sin_b, 0.0)


def _dft_mats(n, scale):
    r = 1
    while r * r < n:
        r *= 2
    k = jnp.arange(n, dtype=jnp.int32)[:, None]
    a = jnp.arange(n // r, dtype=jnp.int32)[None, :]
    c = jnp.arange(r, dtype=jnp.int32)[None, :]
    ang_a = ((k * a * r) % n).astype(F32) * (2.0 * np.pi / n)
    ang_c = ((k * c) % n).astype(F32) * (2.0 * np.pi / n)
    ca, sa = (jnp.cos(ang_a) * scale)[:, :, None], (jnp.sin(ang_a) * scale)[:, :, None]
    cc, sc = jnp.cos(ang_c)[:, None, :], jnp.sin(ang_c)[:, None, :]
    cos = (ca * cc - sa * sc).reshape(n, n)
    sin = (sa * cc + ca * sc).reshape(n, n)
    return cos.astype(BF16), sin.astype(BF16)


def _even_w_in(w):
    d = w.shape[0]
    c3 = 3 * CONV_DIM
    qcols = []
    per_kv = N_Q_HEADS // N_KV_HEADS
    for h in range(N_Q_HEADS):
        wq = w[:, c3 + h * HEAD_DIM:c3 + (h + 1) * HEAD_DIM]
        z = jnp.zeros((d, HEAD_DIM), w.dtype)
        qcols.append(jnp.concatenate([wq, z] if h // per_kv == 0 else [z, wq], axis=1))
    return jnp.concatenate([w[:, :c3]] + qcols + [w[:, c3 + N_Q_HEADS * HEAD_DIM:]], axis=1).astype(BF16)


def _odd_w_in(w):
    d = w.shape[0]
    o = Q_LORA + KV_LORA
    kr = jnp.concatenate([jnp.zeros((d, QK_NOPE), w.dtype), w[:, o:o + QK_ROPE],
                          jnp.zeros((d, LANE - QK_NOPE - QK_ROPE), w.dtype)], axis=1)
    return jnp.concatenate([w[:, :o], kr, w[:, o + QK_ROPE:]], axis=1).astype(BF16)


def _pad_heads(w, per_head, keep_from, keep, heads):
    rows = w.shape[0]
    w3 = w.reshape(rows, heads, per_head)[:, :, keep_from:keep_from + keep]
    w3 = jnp.pad(w3, ((0, 0), (0, 0), (0, LANE - keep)))
    return w3.reshape(rows, heads * LANE).astype(BF16)


def _channel_dft():
    gd = FOURIER_DIM // FOURIER_GROUPS
    k = np.arange(gd)
    ang = 2.0 * np.pi * ((k[:, None] * k[None, :]) % gd) / gd
    eye = np.eye(FOURIER_GROUPS)
    wc = np.kron(eye, np.cos(ang))
    ws = np.kron(eye, np.sin(ang))
    return jnp.asarray(np.concatenate([wc, ws], axis=1), BF16)


def kernel(x, c, ctx, c_ctx, ada_w, ada_b, norm_mix_g, norm_ffn_g, ev_w_in, ev_conv_w, ev_sink, ev_w_out,
           od_w_in, od_q_norm_g, od_w_uq, od_kv_norm_g, od_w_ukv, od_w_out, router_w, router_bias,
           ex_w_gate, ex_w_up, ex_w_down, final_norm_g):
    b, n, d = x.shape
    l = ctx.shape[1]
    s = n + l
    depth = ada_w.shape[0]
    assert l == TM and n % 512 == 0 and b % PAIR == 0

    rows = 8 * ((b + 1 + 7) // 8)
    cvec = jnp.concatenate([c, c_ctx[None, :], jnp.zeros((rows - b - 1, d), F32)], axis=0)
    mod_all = _modulation(cvec, ada_w, ada_b).reshape(depth, rows, 6, d)
    mods = jnp.stack([mod_all[:, :b], jnp.broadcast_to(mod_all[:, b:b + 1], (depth, b, 6, d))], axis=2)

    tabs_even = _rope_tables(n, l, HEAD_DIM, 0)
    tabs_odd = _rope_tables(n, l, QK_ROPE, QK_NOPE)
    gd = FOURIER_DIM // FOURIER_GROUPS
    dft_lat = _dft_mats(n, (n * gd) ** -0.5)
    dft_ctx = _dft_mats(l, (l * gd) ** -0.5)
    wf = _channel_dft()

    g_mix = norm_mix_g.reshape(depth, 1, d)
    g_ffn = norm_ffn_g.reshape(depth, 1, d)
    rw_t = router_w.T.astype(BF16)
    rb = router_bias.reshape(N_EXPERTS, 1).astype(F32)
    wg, wu, wd = ex_w_gate.astype(BF16), ex_w_up.astype(BF16), ex_w_down.astype(BF16)

    xa = jnp.concatenate([x, ctx], axis=1)
    t_pad = (-(-(b * s) // TME) + N_GROUPS * len(PAIRS)) * TME
    hs = jnp.zeros((t_pad, d + LANE), F32)
    out = None
    for li in range(depth):
        j = li // 2
        if li % 2 == 0:
            u, ab, q, k, v = _pre_even(xa, mods, g_mix, _even_w_in(ev_w_in[j]), tabs_even, li)
            ya, yb = _even_mix(u, ab, q, k, v, ev_conv_w[j], ev_sink[j], n)
            w_out = ev_w_out[j].astype(BF16)
        else:
            q, k, v, ycs = _pre_odd(
                xa, mods, g_mix, _odd_w_in(od_w_in[j]),
                od_q_norm_g[j].reshape(1, Q_LORA),
                _pad_heads(od_w_uq[j], QK_NOPE + QK_ROPE, 0, QK_NOPE + QK_ROPE, MLA_HEADS),
                od_kv_norm_g[j].reshape(1, KV_LORA),
                _pad_heads(od_w_ukv[j], QK_NOPE + V_HEAD, 0, QK_NOPE, MLA_HEADS),
                _pad_heads(od_w_ukv[j], QK_NOPE + V_HEAD, QK_NOPE, V_HEAD, MLA_HEADS),
                wf, tabs_odd, li)
            ya = _mla(q, k, v, n)
            yb = _fourier(ycs, dft_lat, dft_ctx, n)
            w_out = od_w_out[j].astype(BF16)
        half = w_out.shape[0] // 2
        xa, hx, rt, cnt = _post(xa, ya, yb, mods, w_out[:half], w_out[half:], g_ffn, rw_t, rb, li)
        pos3, e_lo, e_hi, valid = _route_tables(rt, cnt, t_pad)
        hs = _scatter_rows(hx.reshape(b * s, d + LANE), pos3, hs)
        ys = _gmm(hs, e_lo, e_hi, valid, wg, wu, wd, li)
        if li == depth - 1:
            out = _final(xa, ys, pos3, mods, final_norm_g.reshape(1, d), li, n)
        else:
            xa = _combine(xa, ys, pos3, mods, li)
    return out
```

```python
import functools

import jax
import jax.numpy as jnp
import numpy as np
from jax import lax
from jax.experimental import pallas as pl
from jax.experimental.pallas import tpu as pltpu

F32 = jnp.float32
BF16 = jnp.bfloat16

EPS = 1e-6
ROPE_BASE = 10000.0
GRID_W = 64
NEG_INF = -1e30
LOG2E = 1.4426950408889634
LANE = 128
TM = 256
WIN = 128
N_Q_HEADS = 8
N_KV_HEADS = 2
HEAD_DIM = 64
CONV_DIM = 512
MLA_HEADS = 8
Q_LORA = 256
KV_LORA = 128
QK_NOPE = 64
QK_ROPE = 32
V_HEAD = 64
FOURIER_DIM = 512
FOURIER_GROUPS = 4
N_EXPERTS = 16
N_GROUPS = 4
EXPERTS_PER_GROUP = 4
PAIRS = ((0, 1), (0, 2), (0, 3), (1, 2), (1, 3), (2, 3))
NB_PAD = 32
TME = 512
DMA_QUEUES = 2
TQ_MLA = 512
PAIR = 2
VMEM_LIMIT = 56 * 1024 * 1024


def _cparams(sem):
    return pltpu.CompilerParams(dimension_semantics=sem, vmem_limit_bytes=VMEM_LIMIT)


def _norm_mod(x, g, shift, scale):
    var = jnp.mean(x * x, axis=-1, keepdims=True)
    return (x * lax.rsqrt(var + EPS)) * g * (1.0 + scale) + shift


def _rope(x, cos, sin_a, sin_b, shift):
    w = x.shape[-1]
    return x * cos + pltpu.roll(x, w - shift, 1) * sin_a + pltpu.roll(x, shift, 1) * sin_b


def _mod_kernel(c_ref, w_ref, b_ref, o_ref):
    c = c_ref[...]
    o_ref[...] = jnp.dot(jax.nn.silu(c), w_ref[...], preferred_element_type=F32,
                         precision=lax.Precision.HIGHEST) + b_ref[...]


def _modulation(cvec, ada_w, ada_b):
    depth, d, d6 = ada_w.shape
    tn = d6 // 4
    rows = cvec.shape[0]
    return pl.pallas_call(
        _mod_kernel,
        grid=(depth, d6 // tn),
        in_specs=[pl.BlockSpec((rows, d), lambda l, j: (0, 0)),
                  pl.BlockSpec((None, d, tn), lambda l, j: (l, 0, j)),
                  pl.BlockSpec((None, 1, tn), lambda l, j: (l, 0, j))],
        out_specs=pl.BlockSpec((None, rows, tn), lambda l, j: (l, 0, j)),
        out_shape=jax.ShapeDtypeStruct((depth, rows, d6), F32),
        compiler_params=_cparams(("arbitrary", "arbitrary")),
        name="adaln_modulation",
    )(cvec, ada_w, ada_b.reshape(depth, 1, d6))


def _pre_even_kernel(x_ref, mod_ref, g_ref, w_ref, cos_ref, sa_ref, sb_ref,
                     u_ref, ab_ref, q_ref, k_ref, v_ref):
    g = g_ref[...]
    h = jnp.concatenate([_norm_mod(x_ref[j], g, mod_ref[j, 0:1], mod_ref[j, 1:2]) for j in range(PAIR)], axis=0)
    p = jnp.dot(h.astype(BF16), w_ref[...], preferred_element_type=F32)
    c = CONV_DIM
    ax, ab, ac = p[:, 0:c], p[:, c:2 * c], p[:, 2 * c:3 * c]
    qw = N_Q_HEADS * LANE
    q = p[:, 3 * c:3 * c + qw]
    k = p[:, 3 * c + qw:3 * c + qw + LANE]
    v = p[:, 3 * c + qw + LANE:3 * c + qw + 2 * LANE]
    cos, sa, sb = (jnp.concatenate([t[...]] * PAIR, axis=0) for t in (cos_ref, sa_ref, sb_ref))
    cos8 = jnp.concatenate([cos] * N_Q_HEADS, axis=1)
    sa8 = jnp.concatenate([sa] * N_Q_HEADS, axis=1)
    sb8 = jnp.concatenate([sb] * N_Q_HEADS, axis=1)
    split = lambda a: a.reshape(PAIR, TM, a.shape[-1]).astype(BF16)
    u_ref[...] = split(ac * ax)
    ab_ref[...] = split(ab)
    q_ref[...] = split(_rope(q, cos8, sa8, sb8, 16) * (HEAD_DIM ** -0.5 * LOG2E))
    k_ref[...] = split(_rope(k, cos, sa, sb, 16))
    v_ref[...] = split(v)


def _pre_even(xa, mods, g_mix, w_in, tabs, li):
    b, s, d = xa.shape
    nt = s // TM
    nl = nt - 1
    wcols = w_in.shape[1]
    qw = N_Q_HEADS * LANE
    tab_spec = pl.BlockSpec((TM, LANE), lambda i, bb: (i, 0))
    tok = lambda w: pl.BlockSpec((PAIR, TM, w), lambda i, bb: (bb, i, 0))
    return pl.pallas_call(
        _pre_even_kernel,
        grid=(nt, b // PAIR),
        in_specs=[tok(d),
                  pl.BlockSpec((None, PAIR, None, 6, d), lambda i, bb: (li, bb, i // nl, 0, 0)),
                  pl.BlockSpec((None, 1, d), lambda i, bb: (li, 0, 0)),
                  pl.BlockSpec((d, wcols), lambda i, bb: (0, 0)),
                  tab_spec, tab_spec, tab_spec],
        out_specs=[tok(CONV_DIM), tok(CONV_DIM), tok(qw), tok(LANE), tok(LANE)],
        out_shape=[jax.ShapeDtypeStruct((b, s, CONV_DIM), BF16),
                   jax.ShapeDtypeStruct((b, s, CONV_DIM), BF16),
                   jax.ShapeDtypeStruct((b, s, qw), BF16),
                   jax.ShapeDtypeStruct((b, s, LANE), BF16),
                   jax.ShapeDtypeStruct((b, s, LANE), BF16)],
        compiler_params=_cparams(("arbitrary", "arbitrary")),
        name="even_in_proj",
    )(xa, mods, g_mix, w_in, *tabs)


def _even_mix_kernel(sink_ref, q_ref, k_ref, v_ref,
                     u_ref, up_ref, un_ref, ab_ref, cw_ref, yc_ref, ya_ref, *, nl, n_lat):
    i = pl.program_id(1)
    nt = pl.num_programs(1)
    is_lat = i < nl
    u = u_ref[...].astype(F32)
    row = lax.broadcasted_iota(jnp.int32, u.shape, 0)
    first = jnp.logical_or(i == 0, i == nl)
    last = jnp.logical_or(i == nl - 1, i == nt - 1)
    prev_row = jnp.where(first, 0.0, up_ref[15:16, :].astype(F32))
    next_row = jnp.where(last, 0.0, un_ref[0:1, :].astype(F32))
    u_prev = jnp.where(row == 0, prev_row, pltpu.roll(u, 1, 0))
    u_next = jnp.where(row == TM - 1, next_row, pltpu.roll(u, TM - 1, 0))
    cw = cw_ref[...]
    conv = u_prev * cw[0:1] + u * cw[1:2] + u_next * cw[2:3]
    yc_ref[...] = (ab_ref[...].astype(F32) * conv).astype(BF16)

    s_all = k_ref.shape[0]
    nwin = TM + 2 * WIN
    w0 = pl.multiple_of(jnp.clip(i * TM - WIN, 0, s_all - nwin), WIN)
    k_all = jnp.concatenate([k_ref[pl.ds(w0, nwin), :], k_ref[n_lat:, :]], axis=0)
    v_all = jnp.concatenate([v_ref[pl.ds(w0, nwin), :], v_ref[n_lat:, :]], axis=0)
    nk = k_all.shape[0]
    qpos = i * TM + lax.broadcasted_iota(jnp.int32, (TM, nk), 0)
    kj = lax.broadcasted_iota(jnp.int32, (TM, nk), 1)
    kpos = w0 + kj
    valid_win = (jnp.abs(kpos - qpos) <= WIN) & (kpos < n_lat) & is_lat
    bias = jnp.where((kj >= nwin) | valid_win, 0.0, NEG_INF).astype(F32)
    lane = lax.broadcasted_iota(jnp.int32, (TM, LANE), 1)
    outs = []
    for pair in range(N_Q_HEADS // 2):
        kvh = (2 * pair) // (N_Q_HEADS // N_KV_HEADS)
        o_pair = []
        for h in (2 * pair, 2 * pair + 1):
            q = q_ref[:, h * LANE:(h + 1) * LANE]
            s = lax.dot_general(q, k_all, (((1,), (1,)), ((), ())), preferred_element_type=F32) + bias
            sink = sink_ref[h] * LOG2E
            m = jnp.maximum(jnp.max(s, axis=1, keepdims=True), sink)
            p = jnp.exp2(s - m)
            l = jnp.sum(p, axis=1, keepdims=True) + jnp.exp2(sink - m)
            o = jnp.dot(p.astype(BF16), v_all, preferred_element_type=F32) / l
            if kvh == 1:
                o = pltpu.roll(o, HEAD_DIM, 1)
            o_pair.append(o)
        outs.append(jnp.where(lane < HEAD_DIM, o_pair[0], pltpu.roll(o_pair[1], HEAD_DIM, 1)))
    ya_ref[...] = jnp.concatenate(outs, axis=1).astype(BF16)


def _even_mix(u, ab, q, k, v, conv_w, sink, n_lat):
    b, s, _ = u.shape
    nt = s // TM
    nl = n_lat // TM
    qw = q.shape[-1]
    whole = pl.BlockSpec((None, s, LANE), lambda bb, i: (bb, 0, 0))
    halo = TM // 16
    u16 = u.reshape(b, s // 16, 16, CONV_DIM)
    kern = functools.partial(_even_mix_kernel, nl=nl, n_lat=n_lat)
    tok = lambda w: pl.BlockSpec((None, TM, w), lambda bb, i: (bb, i, 0))
    return pl.pallas_call(
        kern,
        grid=(b, nt),
        in_specs=[pl.BlockSpec(memory_space=pltpu.SMEM),
                  tok(qw), whole, whole,
                  tok(CONV_DIM),
                  pl.BlockSpec((None, None, 16, CONV_DIM),
                               lambda bb, i: (bb, jnp.maximum(i * halo - 1, 0), 0, 0)),
                  pl.BlockSpec((None, None, 16, CONV_DIM),
                               lambda bb, i: (bb, jnp.minimum((i + 1) * halo, s // 16 - 1), 0, 0)),
                  tok(CONV_DIM),
                  pl.BlockSpec((3, CONV_DIM), lambda bb, i: (0, 0))],
        out_specs=[tok(CONV_DIM), tok(N_Q_HEADS * HEAD_DIM)],
        out_shape=[jax.ShapeDtypeStruct((b, s, CONV_DIM), BF16),
                   jax.ShapeDtypeStruct((b, s, N_Q_HEADS * HEAD_DIM), BF16)],
        compiler_params=_cparams(("arbitrary", "arbitrary")),
        name="even_conv_window_attn",
    )(sink, q, k, v, u, u16, u16, ab, conv_w)


def _pre_odd_kernel(x_ref, mod_ref, g_ref, w_ref, gq_ref, wuq_ref, gkv_ref, wuk_ref, wuv_ref,
                    wf_ref, cos_ref, sa_ref, sb_ref, q_ref, k_ref, v_ref, y_ref):
    g = g_ref[...]
    h = jnp.concatenate([_norm_mod(x_ref[j], g, mod_ref[j, 0:1], mod_ref[j, 1:2]) for j in range(PAIR)], axis=0)
    p = jnp.dot(h.astype(BF16), w_ref[...], preferred_element_type=F32)
    cq = p[:, 0:Q_LORA]
    ckv = p[:, Q_LORA:Q_LORA + KV_LORA]
    kr = p[:, Q_LORA + KV_LORA:Q_LORA + KV_LORA + LANE]
    f = p[:, Q_LORA + KV_LORA + LANE:]
    cos, sa, sb = (jnp.concatenate([t[...]] * PAIR, axis=0) for t in (cos_ref, sa_ref, sb_ref))
    split = lambda a: a.reshape(PAIR, TM, a.shape[-1]).astype(BF16)
    cos8 = jnp.concatenate([cos] * MLA_HEADS, axis=1)
    sa8 = jnp.concatenate([sa] * MLA_HEADS, axis=1)
    sb8 = jnp.concatenate([sb] * MLA_HEADS, axis=1)

    cqn = cq * lax.rsqrt(jnp.mean(cq * cq, axis=-1, keepdims=True) + EPS) * gq_ref[...]
    q = jnp.dot(cqn.astype(BF16), wuq_ref[...], preferred_element_type=F32)
    q = _rope(q, cos8, sa8, sb8, 8) * ((QK_NOPE + QK_ROPE) ** -0.5 * LOG2E)
    q_ref[...] = split(q)

    ckvn = (ckv * lax.rsqrt(jnp.mean(ckv * ckv, axis=-1, keepdims=True) + EPS) * gkv_ref[...]).astype(BF16)
    kn = jnp.dot(ckvn, wuk_ref[...], preferred_element_type=F32)
    krr = _rope(kr, cos, sa, sb, 8)
    k_ref[...] = split(kn + jnp.concatenate([krr] * MLA_HEADS, axis=1))
    v = jnp.dot(ckvn, wuv_ref[...], preferred_element_type=F32)
    one_lane = lax.broadcasted_iota(jnp.int32, v.shape, 1) % LANE == V_HEAD
    v_ref[...] = split(jnp.where(one_lane, 1.0, v))
    y_ref[...] = split(jnp.dot(f.astype(BF16), wf_ref[...], preferred_element_type=F32))


def _pre_odd(xa, mods, g_mix, w_in, gq, wuq, gkv, wuk, wuv, wf, tabs, li):
    b, s, d = xa.shape
    nt = s // TM
    nl = nt - 1
    hw = MLA_HEADS * LANE
    vw = hw
    tab_spec = pl.BlockSpec((TM, LANE), lambda i, bb: (i, 0))
    tok = lambda w: pl.BlockSpec((PAIR, TM, w), lambda i, bb: (bb, i, 0))
    full = lambda a: pl.BlockSpec(a.shape, lambda i, bb: (0,) * a.ndim)
    return pl.pallas_call(
        _pre_odd_kernel,
        grid=(nt, b // PAIR),
        in_specs=[tok(d),
                  pl.BlockSpec((None, PAIR, None, 6, d), lambda i, bb: (li, bb, i // nl, 0, 0)),
                  pl.BlockSpec((None, 1, d), lambda i, bb: (li, 0, 0)),
                  full(w_in), full(gq), full(wuq), full(gkv), full(wuk), full(wuv), full(wf),
                  tab_spec, tab_spec, tab_spec],
        out_specs=[tok(hw), tok(hw), tok(vw), tok(2 * FOURIER_DIM)],
        out_shape=[jax.ShapeDtypeStruct((b, s, hw), BF16),
                   jax.ShapeDtypeStruct((b, s, hw), BF16),
                   jax.ShapeDtypeStruct((b, s, vw), BF16),
                   jax.ShapeDtypeStruct((b, s, 2 * FOURIER_DIM), BF16)],
        compiler_params=_cparams(("arbitrary", "arbitrary")),
        name="odd_in_proj",
    )(xa, mods, g_mix, w_in, gq, wuq, gkv, wuk, wuv, wf, *tabs)


def _mla_kernel(q_ref, k_ref, v_ref, o_ref):
    lane = lax.broadcasted_iota(jnp.int32, (q_ref.shape[0], LANE), 1)
    outs = []
    for pair in range(MLA_HEADS // 2):
        o_pair = []
        for h in (2 * pair, 2 * pair + 1):
            q = q_ref[:, h * LANE:(h + 1) * LANE]
            k = k_ref[:, h * LANE:(h + 1) * LANE]
            v = v_ref[:, h * LANE:(h + 1) * LANE]
            s = lax.dot_general(q, k, (((1,), (1,)), ((), ())), preferred_element_type=F32)
            m = jnp.max(s, axis=1, keepdims=True)
            p = jnp.exp2(s - m)
            o = jnp.dot(p.astype(BF16), v, preferred_element_type=F32)
            o_pair.append(o / o[:, V_HEAD:V_HEAD + 1])
        outs.append(jnp.where(lane < V_HEAD, o_pair[0], pltpu.roll(o_pair[1], V_HEAD, 1)))
    o_ref[...] = jnp.concatenate(outs, axis=1).astype(BF16)


def _mla(q, k, v, n_lat):
    b, s, hw = q.shape
    vw = MLA_HEADS * V_HEAD
    n_ctx = s - n_lat
    tq = TQ_MLA if n_lat % TQ_MLA == 0 else TM
    once = pl.Buffered(1)
    lat = pl.pallas_call(
        _mla_kernel,
        grid=(b, n_lat // tq),
        in_specs=[pl.BlockSpec((None, tq, hw), lambda bb, i: (bb, i, 0)),
                  pl.BlockSpec((None, s, hw), lambda bb, i: (bb, 0, 0), pipeline_mode=once),
                  pl.BlockSpec((None, s, hw), lambda bb, i: (bb, 0, 0), pipeline_mode=once)],
        out_specs=pl.BlockSpec((None, tq, vw), lambda bb, i: (bb, i, 0)),
        out_shape=jax.ShapeDtypeStruct((b, n_lat, vw), BF16),
        compiler_params=_cparams(("arbitrary", "arbitrary")),
        name="mla_attention",
    )(q, k, v)
    cblk = n_lat // n_ctx
    ctx_spec = pl.BlockSpec((None, n_ctx, hw), lambda bb: (bb, cblk, 0))
    ctx = pl.pallas_call(
        _mla_kernel,
        grid=(b,),
        in_specs=[ctx_spec, ctx_spec, ctx_spec],
        out_specs=pl.BlockSpec((None, n_ctx, vw), lambda bb: (bb, 0, 0)),
        out_shape=jax.ShapeDtypeStruct((b, n_ctx, vw), BF16),
        compiler_params=_cparams(("arbitrary",)),
        name="mla_attention_context",
    )(q, k, v)
    return jnp.concatenate([lat, ctx], axis=1)


def _dft_kernel(ac_ref, as_ref, cc_ref, cs_ref, yc_ref, ys_ref, ycc_ref, ysc_ref, o_ref, *, nl):
    i = pl.program_id(1)

    @pl.when(i < nl)
    def _():
        o_ref[...] = (jnp.dot(ac_ref[...], yc_ref[...], preferred_element_type=F32)
                      - jnp.dot(as_ref[...], ys_ref[...], preferred_element_type=F32)).astype(BF16)

    @pl.when(i >= nl)
    def _():
        o_ref[...] = (jnp.dot(cc_ref[...], ycc_ref[...], preferred_element_type=F32)
                      - jnp.dot(cs_ref[...], ysc_ref[...], preferred_element_type=F32)).astype(BF16)


def _fourier(ycs, dft_lat, dft_ctx, n_lat):
    b, s, _ = ycs.shape
    fd = FOURIER_DIM
    n_ctx = s - n_lat
    nl = n_lat // TM
    ac, as_ = dft_lat
    cc, cs = dft_ctx
    rows = pl.BlockSpec((TM, n_lat), lambda bb, i: (jnp.minimum(i, nl - 1), 0))
    small = pl.BlockSpec((n_ctx, n_ctx), lambda bb, i: (0, 0))
    return pl.pallas_call(
        functools.partial(_dft_kernel, nl=nl),
        grid=(b, s // TM),
        in_specs=[rows, rows, small, small,
                  pl.BlockSpec((None, n_lat, fd), lambda bb, i: (bb, 0, 0)),
                  pl.BlockSpec((None, n_lat, fd), lambda bb, i: (bb, 0, 1)),
                  pl.BlockSpec((None, n_ctx, fd), lambda bb, i: (bb, n_lat // n_ctx, 0)),
                  pl.BlockSpec((None, n_ctx, fd), lambda bb, i: (bb, n_lat // n_ctx, 1))],
        out_specs=pl.BlockSpec((None, TM, fd), lambda bb, i: (bb, i, 0)),
        out_shape=jax.ShapeDtypeStruct((b, s, fd), BF16),
        compiler_params=_cparams(("arbitrary", "arbitrary")),
        name="fourier_mix",
    )(ac, as_, cc, cs, ycs, ycs, ycs, ycs)


def _post_kernel(x_ref, ya_ref, yb_ref, mod_ref, wa_ref, wb_ref, g_ref, rw_ref, rb_ref,
                 xo_ref, hx_ref, rt_ref, cnt_ref, carry_ref):
    @pl.when((pl.program_id(0) == 0) & (pl.program_id(1) == 0))
    def _():
        carry_ref[...] = jnp.zeros_like(carry_ref)

    mod = mod_ref[...]
    y = (jnp.dot(ya_ref[...], wa_ref[...], preferred_element_type=F32)
         + jnp.dot(yb_ref[...], wb_ref[...], preferred_element_type=F32))
    x = x_ref[...] + mod[2:3] * y
    xo_ref[...] = x
    h2f = _norm_mod(x, g_ref[...], mod[3:4], mod[4:5])
    h2 = h2f.astype(BF16)
    logits = lax.dot_general(rw_ref[...], h2, (((1,), (1,)), ((), ())), preferred_element_type=F32)
    scores = jax.nn.sigmoid(logits)
    biased = scores + rb_ref[...]
    sc = [scores[e:e + 1, :] for e in range(N_EXPERTS)]
    bs = [biased[e:e + 1, :] for e in range(N_EXPERTS)]
    sel, gval = [], []
    for g in range(N_GROUPS):
        ids = range(g * EXPERTS_PER_GROUP, (g + 1) * EXPERTS_PER_GROUP)
        gsum = None
        for e in ids:
            rank = sum(((bs[j] > bs[e]) if j > e else (bs[j] >= bs[e])).astype(jnp.int32)
                       for j in ids if j != e)
            s_e = rank < 2
            sel.append(s_e)
            term = jnp.where(s_e, bs[e], 0.0)
            gsum = term if gsum is None else gsum + term
        gval.append(gsum)
    gsel = []
    for g in range(N_GROUPS):
        ok = None
        for j in range(N_GROUPS):
            if j == g:
                continue
            c = (gval[g] > gval[j]) if j < g else (gval[g] >= gval[j])
            ok = c if ok is None else (ok & c)
        gsel.append(ok)
    chosen = [sel[e] & gsel[e // EXPERTS_PER_GROUP] for e in range(N_EXPERTS)]
    wsum = sum(jnp.where(chosen[e], sc[e], 0.0) for e in range(N_EXPERTS))
    gates = [jnp.where(chosen[e], sc[e], 0.0) / wsum for e in range(N_EXPERTS)]
    bucket = jnp.zeros_like(wsum)
    w_lo = jnp.zeros_like(wsum)
    w_hi = jnp.zeros_like(wsum)
    for g in range(N_GROUPS):
        for pi, (ea, eb) in enumerate(PAIRS):
            ea, eb = g * EXPERTS_PER_GROUP + ea, g * EXPERTS_PER_GROUP + eb
            cond = chosen[ea] & chosen[eb]
            bucket = jnp.where(cond, float(g * len(PAIRS) + pi), bucket)
            w_lo = jnp.where(cond, gates[ea], w_lo)
            w_hi = jnp.where(cond, gates[eb], w_hi)
    kio = lax.broadcasted_iota(jnp.int32, (NB_PAD, TM), 0).astype(F32)
    onehot = (kio == bucket).astype(F32)
    tri = (lax.broadcasted_iota(jnp.int32, (TM, TM), 0)
           <= lax.broadcasted_iota(jnp.int32, (TM, TM), 1)).astype(BF16)
    prefix = jnp.dot(onehot.astype(BF16), tri, preferred_element_type=F32)
    carry = carry_ref[...]
    rank = jnp.sum(onehot * (prefix + carry[:, 0:1]), axis=0, keepdims=True) - 1.0
    carry = carry + jnp.sum(onehot, axis=1, keepdims=True)
    carry_ref[...] = carry
    cnt_ref[...] = carry
    zrow = jnp.zeros_like(wsum)
    rt_ref[...] = jnp.concatenate([bucket, rank] + [zrow] * 6, axis=0)
    ext_t = jnp.concatenate([w_lo, w_hi] + [zrow] * (LANE - 2), axis=0)
    hx_ref[...] = jnp.concatenate([h2f, ext_t.T], axis=1)


def _post(xa, ya, yb, mods, w_out_a, w_out_b, g_ffn, rw_t, rb, li):
    b, s, d = xa.shape
    nt = s // TM
    nl = nt - 1
    half = ya.shape[-1]
    tok = lambda w: pl.BlockSpec((None, TM, w), lambda bb, i: (bb, i, 0))
    return pl.pallas_call(
        _post_kernel,
        grid=(b, nt),
        in_specs=[tok(d), tok(half), tok(half),
                  pl.BlockSpec((None, None, None, 6, d), lambda bb, i: (li, bb, i // nl, 0, 0)),
                  pl.BlockSpec((half, d), lambda bb, i: (0, 0)),
                  pl.BlockSpec((half, d), lambda bb, i: (0, 0)),
                  pl.BlockSpec((None, 1, d), lambda bb, i: (li, 0, 0)),
                  pl.BlockSpec((N_EXPERTS, d), lambda bb, i: (0, 0)),
                  pl.BlockSpec((N_EXPERTS, 1), lambda bb, i: (0, 0))],
        out_specs=[tok(d), tok(d + LANE),
                   pl.BlockSpec((None, 8, TM), lambda bb, i: (bb, 0, i)),
                   pl.BlockSpec((NB_PAD, LANE), lambda bb, i: (0, 0))],
        out_shape=[jax.ShapeDtypeStruct((b, s, d), F32),
                   jax.ShapeDtypeStruct((b, s, d + LANE), F32),
                   jax.ShapeDtypeStruct((b, 8, s), F32),
                   jax.ShapeDtypeStruct((NB_PAD, LANE), F32)],
        scratch_shapes=[pltpu.VMEM((NB_PAD, LANE), F32)],
        input_output_aliases={0: 0},
        compiler_params=_cparams(("arbitrary", "arbitrary")),
        name="out_proj_router",
    )(xa, ya, yb, mods, w_out_a, w_out_b, g_ffn, rw_t, rb)


def _scatter_kernel(pos_ref, h_ref, init_ref, o_ref, sem):
    del init_ref

    def body(r2, c):
        for prio in range(DMA_QUEUES):
            r = r2 * DMA_QUEUES + prio
            p = pos_ref[0, r]
            pltpu.make_async_copy(h_ref.at[pl.ds(r, 1), :], o_ref.at[pl.ds(p, 1), :], sem).start(priority=prio)
        return c

    lax.fori_loop(0, TM // DMA_QUEUES, body, 0, unroll=16)
    pltpu.make_async_copy(h_ref, o_ref.at[pl.ds(0, TM), :], sem).wait()


def _scatter_rows(hx, pos3, init):
    t, w = hx.shape
    t_pad = init.shape[0]
    return pl.pallas_call(
        _scatter_kernel,
        grid=(t // TM,),
        in_specs=[pl.BlockSpec((None, 1, TM), lambda i: (i, 0, 0), memory_space=pltpu.SMEM),
                  pl.BlockSpec((TM, w), lambda i: (i, 0)),
                  pl.BlockSpec(memory_space=pl.ANY)],
        out_specs=pl.BlockSpec(memory_space=pl.ANY),
        out_shape=jax.ShapeDtypeStruct((t_pad, w), F32),
        scratch_shapes=[pltpu.SemaphoreType.DMA(())],
        input_output_aliases={2: 0},
        compiler_params=_cparams(("arbitrary",)),
        name="moe_scatter_rows",
    )(pos3, hx, init)


def _gmm_kernel(elo_ref, ehi_ref, valid_ref, hs_ref, wg0, wu0, wd0, wg1, wu1, wd1, o_ref):
    del elo_ref, ehi_ref
    j = pl.program_id(0)
    d = o_ref.shape[-1]

    @pl.when(valid_ref[j] == 1)
    def _():
        hx = hs_ref[...]
        h = hx[:, :d].astype(BF16)

        def ffn(wg, wu, wd):
            a = jnp.dot(h, wg[...], preferred_element_type=F32)
            u = jnp.dot(h, wu[...], preferred_element_type=F32)
            act = (jax.nn.silu(a) * u).astype(BF16)
            return jnp.dot(act, wd[...], preferred_element_type=F32)

        o_ref[...] = hx[:, d:d + 1] * ffn(wg0, wu0, wd0) + hx[:, d + 1:d + 2] * ffn(wg1, wu1, wd1)

    @pl.when(valid_ref[j] == 0)
    def _():
        o_ref[...] = jnp.zeros_like(o_ref)


def _gmm(hs, e_lo, e_hi, valid, wg, wu, wd, li):
    t_pad, w = hs.shape
    d = w - LANE
    dff = wg.shape[-1]
    up = lambda sel: pl.BlockSpec((None, None, d, dff), lambda j, lo, hi, v: (li, sel(lo, hi)[j], 0, 0))
    dn = lambda sel: pl.BlockSpec((None, None, dff, d), lambda j, lo, hi, v: (li, sel(lo, hi)[j], 0, 0))
    first = lambda lo, hi: lo
    second = lambda lo, hi: hi
    return pl.pallas_call(
        _gmm_kernel,
        grid_spec=pltpu.PrefetchScalarGridSpec(
            num_scalar_prefetch=3,
            grid=(t_pad // TME,),
            in_specs=[pl.BlockSpec((TME, w), lambda j, lo, hi, v: (j, 0)),
                      up(first), up(first), dn(first), up(second), up(second), dn(second)],
            out_specs=pl.BlockSpec((TME, d), lambda j, lo, hi, v: (j, 0))),
        out_shape=jax.ShapeDtypeStruct((t_pad, d), F32),
        compiler_params=_cparams(("arbitrary",)),
        name="moe_grouped_ffn",
    )(e_lo, e_hi, valid, hs, wg, wu, wd, wg, wu, wd)


def _route_tables(rt, cnt, t_pad):
    b, _, s = rt.shape
    nb = N_GROUPS * len(PAIRS)
    bucket = rt[:, 0, :].astype(jnp.int32)
    rank = rt[:, 1, :].astype(jnp.int32)
    counts = cnt[:nb, 0].astype(jnp.int32)
    padded = ((counts + TME - 1) // TME) * TME
    ids = jnp.arange(nb, dtype=jnp.int32)
    ends = jnp.sum(jnp.where(ids[None, :] <= ids[:, None], padded[None, :], 0), axis=1)
    starts = ends - padded
    start_of = jnp.sum(jnp.where(bucket[..., None] == ids, starts, 0), axis=-1)
    pos = (start_of + rank).reshape(b * s // TM, 1, TM)
    n_tiles = t_pad // TME
    n_used = ends[-1] // TME
    tile = jnp.arange(n_tiles, dtype=jnp.int32)
    valid = (tile < n_used).astype(jnp.int32)
    tile_c = jnp.minimum(tile, n_used - 1)
    tb = jnp.minimum(jnp.sum((ends[None, :] <= tile_c[:, None] * TME).astype(jnp.int32), axis=1), nb - 1)
    e_lo_of = jnp.asarray([g * EXPERTS_PER_GROUP + p[0] for g in range(N_GROUPS) for p in PAIRS], jnp.int32)
    e_hi_of = jnp.asarray([g * EXPERTS_PER_GROUP + p[1] for g in range(N_GROUPS) for p in PAIRS], jnp.int32)
    hit = tb[:, None] == ids
    e_lo = jnp.sum(jnp.where(hit, e_lo_of, 0), axis=1)
    e_hi = jnp.sum(jnp.where(hit, e_hi_of, 0), axis=1)
    return pos, e_lo, e_hi, valid


def _issue_gather(pos_ref, y_ref, buf_ref, sem, slot):
    def body(r2, c):
        for prio in range(DMA_QUEUES):
            r = r2 * DMA_QUEUES + prio
            p = pos_ref[0, r]
            pltpu.make_async_copy(y_ref.at[pl.ds(p, 1), :], buf_ref.at[slot, pl.ds(r, 1), :],
                                  sem.at[slot]).start(priority=prio)
        return c

    lax.fori_loop(0, TM // DMA_QUEUES, body, 0, unroll=16)


def _gather_tile(pos_ref, posn_ref, y_ref, buf_ref, sem):
    ni = pl.num_programs(1)
    lin = pl.program_id(0) * ni + pl.program_id(1)
    slot = lin % 2

    @pl.when(lin == 0)
    def _():
        _issue_gather(pos_ref, y_ref, buf_ref, sem, slot)

    @pl.when(lin + 1 < pl.num_programs(0) * ni)
    def _():
        _issue_gather(posn_ref, y_ref, buf_ref, sem, 1 - slot)

    pltpu.make_async_copy(y_ref.at[pl.ds(0, TM), :], buf_ref.at[slot], sem.at[slot]).wait()
    return buf_ref[slot]


def _combine_kernel(pos_ref, posn_ref, x_ref, mod_ref, y_ref, xo_ref, buf_ref, sem):
    f = _gather_tile(pos_ref, posn_ref, y_ref, buf_ref, sem)
    mod = mod_ref[...]
    xo_ref[...] = x_ref[...] + mod[5:6] * f


def _final_kernel(pos_ref, posn_ref, x_ref, mod_ref, g_ref, y_ref, o_ref, buf_ref, sem):
    f = _gather_tile(pos_ref, posn_ref, y_ref, buf_ref, sem)
    mod = mod_ref[...]
    x = x_ref[...] + mod[5:6] * f
    var = jnp.mean(x * x, axis=-1, keepdims=True)
    o_ref[...] = (x * lax.rsqrt(var + EPS)) * g_ref[...]


def _pos_specs(nt, ni, nb):
    def cur(bb, i):
        return (bb * nt + i, 0, 0)

    def nxt(bb, i):
        wrap = i + 1 == ni
        b2 = jnp.minimum(jnp.where(wrap, bb + 1, bb), nb - 1)
        i2 = jnp.where(wrap, jnp.where(bb + 1 == nb, i, 0), i + 1)
        return (b2 * nt + i2, 0, 0)

    return [pl.BlockSpec((None, 1, TM), cur, memory_space=pltpu.SMEM),
            pl.BlockSpec((None, 1, TM), nxt, memory_space=pltpu.SMEM)]


def _combine(xa, ys, pos3, mods, li):
    b, s, d = xa.shape
    nt = s // TM
    nl = nt - 1
    tok = pl.BlockSpec((None, TM, d), lambda bb, i: (bb, i, 0))
    return pl.pallas_call(
        _combine_kernel,
        grid=(b, nt),
        in_specs=_pos_specs(nt, nt, b) + [
            tok,
            pl.BlockSpec((None, None, None, 6, d), lambda bb, i: (li, bb, i // nl, 0, 0)),
            pl.BlockSpec(memory_space=pl.ANY)],
        out_specs=tok,
        out_shape=jax.ShapeDtypeStruct((b, s, d), F32),
        scratch_shapes=[pltpu.VMEM((2, TM, d), F32), pltpu.SemaphoreType.DMA((2,))],
        input_output_aliases={2: 0},
        compiler_params=_cparams(("arbitrary", "arbitrary")),
        name="moe_gather_residual",
    )(pos3, pos3, xa, mods, ys)


def _final(xa, ys, pos3, mods, g_final, li, n_lat):
    b, s, d = xa.shape
    nt = s // TM
    nl = n_lat // TM
    tok = pl.BlockSpec((None, TM, d), lambda bb, i: (bb, i, 0))
    return pl.pallas_call(
        _final_kernel,
        grid=(b, nl),
        in_specs=_pos_specs(nt, nl, b) + [
            tok,
            pl.BlockSpec((None, None, None, 6, d), lambda bb, i: (li, bb, 0, 0, 0)),
            pl.BlockSpec((1, d), lambda bb, i: (0, 0)),
            pl.BlockSpec(memory_space=pl.ANY)],
        out_specs=tok,
        out_shape=jax.ShapeDtypeStruct((b, n_lat, d), F32),
        scratch_shapes=[pltpu.VMEM((2, TM, d), F32), pltpu.SemaphoreType.DMA((2,))],
        compiler_params=_cparams(("arbitrary", "arbitrary")),
        name="moe_gather_residual_final_norm",
    )(pos3, pos3, xa, mods, g_final, ys)


def _rope_tables(n_lat, n_ctx, rot_dim, lane_off):
    rows = n_lat // GRID_W
    r = jnp.repeat(jnp.arange(rows, dtype=F32), GRID_W)[:n_lat]
    c = jnp.tile(jnp.arange(GRID_W, dtype=F32), rows)[:n_lat]
    axis_dim = rot_dim // 2
    inv_freq = ROPE_BASE ** (-jnp.arange(0, axis_dim, 2, dtype=F32) / axis_dim)
    ang_r = r[:, None] * inv_freq
    ang_c = c[:, None] * inv_freq
    ang = jnp.concatenate([ang_r, ang_r, ang_c, ang_c], axis=-1)
    cos, sin = jnp.cos(ang), jnp.sin(ang)
    grp = (np.arange(rot_dim) // (rot_dim // 4)) % 2
    sin_a = jnp.where(grp == 0, -sin, 0.0)
    sin_b = jnp.where(grp == 1, sin, 0.0)

    def place(t, fill):
        reps = 2 if rot_dim == 64 else 1
        blk = jnp.full((n_lat, LANE), fill, F32)
        for k in range(reps):
            blk = blk.at[:, lane_off + k * rot_dim:lane_off + (k + 1) * rot_dim].set(t)
        ctx = jnp.full((n_ctx, LANE), fill, F32)
        return jnp.concatenate([blk, ctx], axis=0)

    return place(cos, 1.0), place(sin_a, 0.0), place(sin_b, 0.0)


def _dft_mats(n, scale):
    r = 1
    while r * r < n:
        r *= 2
    k = jnp.arange(n, dtype=jnp.int32)[:, None]
    a = jnp.arange(n // r, dtype=jnp.int32)[None, :]
    c = jnp.arange(r, dtype=jnp.int32)[None, :]
    ang_a = ((k * a * r) % n).astype(F32) * (2.0 * np.pi / n)
    ang_c = ((k * c) % n).astype(F32) * (2.0 * np.pi / n)
    ca, sa = (jnp.cos(ang_a) * scale)[:, :, None], (jnp.sin(ang_a) * scale)[:, :, None]
    cc, sc = jnp.cos(ang_c)[:, None, :], jnp.sin(ang_c)[:, None, :]
    cos = (ca * cc - sa * sc).reshape(n, n)
    sin = (sa * cc + ca * sc).reshape(n, n)
    return cos.astype(BF16), sin.astype(BF16)


def _even_w_in(w):
    d = w.shape[0]
    c3 = 3 * CONV_DIM
    qcols = []
    per_kv = N_Q_HEADS // N_KV_HEADS
    for h in range(N_Q_HEADS):
        wq = w[:, c3 + h * HEAD_DIM:c3 + (h + 1) * HEAD_DIM]
        z = jnp.zeros((d, HEAD_DIM), w.dtype)
        qcols.append(jnp.concatenate([wq, z] if h // per_kv == 0 else [z, wq], axis=1))
    return jnp.concatenate([w[:, :c3]] + qcols + [w[:, c3 + N_Q_HEADS * HEAD_DIM:]], axis=1).astype(BF16)


def _odd_w_in(w):
    d = w.shape[0]
    o = Q_LORA + KV_LORA
    kr = jnp.concatenate([jnp.zeros((d, QK_NOPE), w.dtype), w[:, o:o + QK_ROPE],
                          jnp.zeros((d, LANE - QK_NOPE - QK_ROPE), w.dtype)], axis=1)
    return jnp.concatenate([w[:, :o], kr, w[:, o + QK_ROPE:]], axis=1).astype(BF16)


def _pad_heads(w, per_head, keep_from, keep, heads):
    rows = w.shape[0]
    w3 = w.reshape(rows, heads, per_head)[:, :, keep_from:keep_from + keep]
    w3 = jnp.pad(w3, ((0, 0), (0, 0), (0, LANE - keep)))
    return w3.reshape(rows, heads * LANE).astype(BF16)


def _channel_dft():
    gd = FOURIER_DIM // FOURIER_GROUPS
    k = np.arange(gd)
    ang = 2.0 * np.pi * ((k[:, None] * k[None, :]) % gd) / gd
    eye = np.eye(FOURIER_GROUPS)
    wc = np.kron(eye, np.cos(ang))
    ws = np.kron(eye, np.sin(ang))
    return jnp.asarray(np.concatenate([wc, ws], axis=1), BF16)


def kernel(x, c, ctx, c_ctx, ada_w, ada_b, norm_mix_g, norm_ffn_g, ev_w_in, ev_conv_w, ev_sink, ev_w_out,
           od_w_in, od_q_norm_g, od_w_uq, od_kv_norm_g, od_w_ukv, od_w_out, router_w, router_bias,
           ex_w_gate, ex_w_up, ex_w_down, final_norm_g):
    b, n, d = x.shape
    l = ctx.shape[1]
    s = n + l
    depth = ada_w.shape[0]
    assert l == TM and n % 512 == 0 and b % PAIR == 0

    rows = 8 * ((b + 1 + 7) // 8)
    cvec = jnp.concatenate([c, c_ctx[None, :], jnp.zeros((rows - b - 1, d), F32)], axis=0)
    mod_all = _modulation(cvec, ada_w, ada_b).reshape(depth, rows, 6, d)
    mods = jnp.stack([mod_all[:, :b], jnp.broadcast_to(mod_all[:, b:b + 1], (depth, b, 6, d))], axis=2)

    tabs_even = _rope_tables(n, l, HEAD_DIM, 0)
    tabs_odd = _rope_tables(n, l, QK_ROPE, QK_NOPE)
    gd = FOURIER_DIM // FOURIER_GROUPS
    dft_lat = _dft_mats(n, (n * gd) ** -0.5)
    dft_ctx = _dft_mats(l, (l * gd) ** -0.5)
    wf = _channel_dft()

    g_mix = norm_mix_g.reshape(depth, 1, d)
    g_ffn = norm_ffn_g.reshape(depth, 1, d)
    rw_t = router_w.T.astype(BF16)
    rb = router_bias.reshape(N_EXPERTS, 1).astype(F32)
    wg, wu, wd = ex_w_gate.astype(BF16), ex_w_up.astype(BF16), ex_w_down.astype(BF16)

    xa = jnp.concatenate([x, ctx], axis=1)
    t_pad = (-(-(b * s) // TME) + N_GROUPS * len(PAIRS)) * TME
    hs = jnp.zeros((t_pad, d + LANE), F32)
    out = None
    for li in range(depth):
        j = li // 2
        if li % 2 == 0:
            u, ab, q, k, v = _pre_even(xa, mods, g_mix, _even_w_in(ev_w_in[j]), tabs_even, li)
            ya, yb = _even_mix(u, ab, q, k, v, ev_conv_w[j], ev_sink[j], n)
            w_out = ev_w_out[j].astype(BF16)
        else:
            q, k, v, ycs = _pre_odd(
                xa, mods, g_mix, _odd_w_in(od_w_in[j]),
                od_q_norm_g[j].reshape(1, Q_LORA),
                _pad_heads(od_w_uq[j], QK_NOPE + QK_ROPE, 0, QK_NOPE + QK_ROPE, MLA_HEADS),
                od_kv_norm_g[j].reshape(1, KV_LORA),
                _pad_heads(od_w_ukv[j], QK_NOPE + V_HEAD, 0, QK_NOPE, MLA_HEADS),
                _pad_heads(od_w_ukv[j], QK_NOPE + V_HEAD, QK_NOPE, V_HEAD, MLA_HEADS),
                wf, tabs_odd, li)
            ya = _mla(q, k, v, n)
            yb = _fourier(ycs, dft_lat, dft_ctx, n)
            w_out = od_w_out[j].astype(BF16)
        half = w_out.shape[0] // 2
        xa, hx, rt, cnt = _post(xa, ya, yb, mods, w_out[:half], w_out[half:], g_ffn, rw_t, rb, li)
        pos3, e_lo, e_hi, valid = _route_tables(rt, cnt, t_pad)
        hs = _scatter_rows(hx.reshape(b * s, d + LANE), pos3, hs)
        ys = _gmm(hs, e_lo, e_hi, valid, wg, wu, wd, li)
        if li == depth - 1:
            out = _final(xa, ys, pos3, mods, final_norm_g.reshape(1, d), li, n)
        else:
            xa = _combine(xa, ys, pos3, mods, li)
    return out
```
